```python
import jax, jax.numpy as jnp
from jax import lax
import numpy as np

D_MODEL = 1024
BATCH = 8
SEQ = 2048
DEPTH = 4

GRID_W = 64
CTX_LEN = 256

A_HEADS = 4
A_DK = 128
A_DV = 128
A_WIDTH = A_HEADS * A_DK
B_HEADS = 8
B_HD = 64
B_WIDTH = B_HEADS * B_HD
D_MIX = A_WIDTH + B_WIDTH

A_Q = 0
A_FF = A_Q + A_WIDTH
A_FB = A_FF + A_WIDTH
A_I = A_FB + A_WIDTH
A_G = A_I + A_WIDTH
B_Q = A_G + A_WIDTH
B_K = B_Q + B_WIDTH
B_V = B_K + B_WIDTH
D_IN = B_V + B_WIDTH

CHUNK = 64
NA_ROWS = 8
NA_COLS = 16
ROPE_BASE = 10000.0

N_EXPERTS = 16
N_GROUPS = 4
EXPERTS_PER_GROUP = N_EXPERTS // N_GROUPS
TOP_K = 2
D_EXPERT = 512

ALPHA = (2 * DEPTH) ** 0.25
BETA = (8 * DEPTH) ** -0.25
LN_EPS = 1e-5
RMS_EPS = 1e-6
F_FLOOR = 1e-6
MASK_VALUE = -1e30

kernel_name = 'hybrid_hgrn2_natten_grouped_moe_dit'

f32 = jnp.float32


def cols(u, start, width):
    return u[..., start:start + width]


def split_heads(t, n_heads):
    b, L, w = t.shape
    return t.reshape(b, L, n_heads, w // n_heads).transpose(0, 2, 1, 3)


def merge_heads(t):
    b, h, L, d = t.shape
    return t.transpose(0, 2, 1, 3).reshape(b, L, h * d)


def layer_norm(x, g, b):
    xf = x.astype(f32)
    mu = jnp.mean(xf, axis=-1, keepdims=True)
    var = jnp.mean(jnp.square(xf - mu), axis=-1, keepdims=True)
    return ((xf - mu) * lax.rsqrt(var + LN_EPS) * g + b).astype(x.dtype)


def forget_gate(z, lb):
    z = z.astype(f32)
    f = lb + (1.0 - lb) * jax.nn.sigmoid(z)
    log_f = jnp.log(jnp.maximum(f, F_FLOOR))
    k = (1.0 - lb) * jax.nn.sigmoid(-z)
    return split_heads(k, A_HEADS), split_heads(log_f, A_HEADS)


def hgrn2_chunk_scan(q, k, v, log_f, s0):
    b, h, L, _ = q.shape
    nc = L // CHUNK

    def to_chunks(t):
        return jnp.moveaxis(t.reshape(b, h, nc, CHUNK, t.shape[-1]), 2, 0)

    causal = jnp.tril(jnp.ones((CHUNK, CHUNK), dtype=bool))[:, :, None]

    def step(S, inp):
        qc, kc, vc, lfc = inp
        cum = jnp.cumsum(lfc, axis=2)
        o_inter = jnp.einsum('bhck,bhkv->bhcv', qc * jnp.exp(cum), S)
        diff = cum[:, :, :, None, :] - cum[:, :, None, :, :]
        decay = jnp.where(causal, jnp.exp(jnp.where(causal, diff, 0.0)), 0.0)
        scores = jnp.einsum('bhtk,bhsk,bhtsk->bhts', qc, kc, decay)
        o_intra = jnp.einsum('bhts,bhsv->bhtv', scores, vc)
        last = cum[:, :, -1:, :]
        k_dec = kc * jnp.exp(last - cum)
        S_new = jnp.exp(last[:, :, 0, :])[..., None] * S + jnp.einsum('bhck,bhcv->bhkv', k_dec, vc)
        return S_new, o_inter + o_intra

    xs = (to_chunks(q.astype(f32)), to_chunks(k), to_chunks(v.astype(f32)), to_chunks(log_f))
    S_fin, o = lax.scan(step, s0, xs)
    o = jnp.moveaxis(o, 0, 2).reshape(b, h, L, -1)
    return o, S_fin


def hgrn2_final_state(k, v, log_f):
    cum = jnp.cumsum(log_f, axis=2)
    k_dec = k * jnp.exp(cum[:, :, -1:] - cum)
    return jnp.einsum('bhlk,bhlv->bhkv', k_dec, v.astype(f32))


def hgrn2_readout(o, g, norm_g, dtype):
    of = o.astype(f32)
    of = of * lax.rsqrt(jnp.mean(jnp.square(of), axis=-1, keepdims=True) + RMS_EPS) * norm_g
    return (merge_heads(of) * jax.nn.silu(g.astype(f32))).astype(dtype)


def hgrn2_mixer(u_ctx, u_lat, lb_f, lb_b, norm_g, ctx_out):
    flip = lambda t: jnp.flip(t, axis=2)
    b = u_lat.shape[0]
    zeros = jnp.zeros((b, A_HEADS, A_DK, A_DV), f32)
    ckf, clff = forget_gate(cols(u_ctx, A_FF, A_WIDTH), lb_f)
    ckb, clfb = forget_gate(cols(u_ctx, A_FB, A_WIDTH), lb_b)
    cv = split_heads(cols(u_ctx, A_I, A_WIDTH), A_HEADS)
    if ctx_out:
        cq = split_heads(jax.nn.silu(cols(u_ctx, A_Q, A_WIDTH)), A_HEADS)
        oc_f, s_f = hgrn2_chunk_scan(cq, ckf, cv, clff, zeros)
        oc_b, s_b = hgrn2_chunk_scan(flip(cq), flip(ckb), flip(cv), flip(clfb), zeros)
        a_ctx = hgrn2_readout(oc_f + flip(oc_b), cols(u_ctx, A_G, A_WIDTH), norm_g, u_ctx.dtype)
    else:
        s_f = hgrn2_final_state(ckf, cv, clff)
        s_b = hgrn2_final_state(flip(ckb), flip(cv), flip(clfb))
        a_ctx = None
    lq = split_heads(jax.nn.silu(cols(u_lat, A_Q, A_WIDTH)), A_HEADS)
    lkf, llff = forget_gate(cols(u_lat, A_FF, A_WIDTH), lb_f)
    lkb, llfb = forget_gate(cols(u_lat, A_FB, A_WIDTH), lb_b)
    lv = split_heads(cols(u_lat, A_I, A_WIDTH), A_HEADS)
    o_f, _ = hgrn2_chunk_scan(lq, lkf, lv, llff, s_f)
    o_b, _ = hgrn2_chunk_scan(flip(lq), flip(lkb), flip(lv), flip(llfb), s_b)
    a_lat = hgrn2_readout(o_f + flip(o_b), cols(u_lat, A_G, A_WIDTH), norm_g, u_lat.dtype)
    return a_ctx, a_lat


def rope_1d(t, pos):
    d = t.shape[-1] // 2
    inv = ROPE_BASE ** (-jnp.arange(d, dtype=f32) / d)
    ang = pos.astype(f32)[:, None] * inv[None, :]
    cos, sin = jnp.cos(ang), jnp.sin(ang)
    t1, t2 = t[..., :d].astype(f32), t[..., d:].astype(f32)
    return jnp.concatenate([t1 * cos - t2 * sin, t1 * sin + t2 * cos], axis=-1)


def axial_rope(t, pos_r, pos_c):
    half = t.shape[-1] // 2
    out = jnp.concatenate([rope_1d(t[..., :half], pos_r), rope_1d(t[..., half:], pos_c)], axis=-1)
    return out.astype(t.dtype)


def na_indices(rows):
    kr = min(NA_ROWS, rows)
    r = np.arange(rows)
    rs = np.clip(r - kr // 2, 0, rows - kr)
    key_rows = rs[:, None] + np.arange(kr)[None, :]
    dr = key_rows - r[:, None] + (NA_ROWS - 1)
    c = np.arange(GRID_W)
    cs = np.clip(c - NA_COLS // 2, 0, GRID_W - NA_COLS)
    in_win = (c[None, :] >= cs[:, None]) & (c[None, :] < cs[:, None] + NA_COLS)
    dc = np.clip(c[None, :] - c[:, None], -(NA_COLS - 1), NA_COLS - 1) + (NA_COLS - 1)
    return kr, key_rows, dr, in_win, dc


def neighbourhood_attention(q_rot, k_rot, v, q_plain, k_ctx, v_ctx, rpb_l):
    b, h, L, d = q_rot.shape
    rows = L // GRID_W
    kr, key_rows, dr, in_win, dc = na_indices(rows)
    n_lat = kr * GRID_W
    grid = lambda t: t.reshape(b, h, rows, GRID_W, d)
    k_blk = grid(k_rot)[:, :, key_rows].reshape(b, h, rows, n_lat, d)
    v_blk = grid(v)[:, :, key_rows].reshape(b, h, rows, n_lat, d)
    scale = d ** -0.5
    s_lat = jnp.einsum('bhrqd,bhrkd->bhrqk', grid(q_rot), k_blk).astype(f32) * scale
    bias = rpb_l[:, dr[:, None, :, None], dc[None, :, None, :]].reshape(h, rows, GRID_W, n_lat)
    mask = np.broadcast_to(in_win[:, None, :], (GRID_W, kr, GRID_W)).reshape(GRID_W, n_lat)
    s_lat = jnp.where(mask, s_lat + bias.astype(f32), MASK_VALUE)
    s_ctx = jnp.einsum('bhrqd,bhkd->bhrqk', grid(q_plain), k_ctx).astype(f32) * scale
    p = jax.nn.softmax(jnp.concatenate([s_lat, s_ctx], axis=-1), axis=-1).astype(v.dtype)
    o = (jnp.einsum('bhrqk,bhrkd->bhrqd', p[..., :n_lat], v_blk)
         + jnp.einsum('bhrqk,bhkd->bhrqd', p[..., n_lat:], v_ctx))
    return o.reshape(b, h, L, d)


def context_attention(q, k, v):
    s = jnp.einsum('bhqd,bhkd->bhqk', q, k).astype(f32) * (q.shape[-1] ** -0.5)
    p = jax.nn.softmax(s, axis=-1).astype(v.dtype)
    return jnp.einsum('bhqk,bhkd->bhqd', p, v)


def na_mixer(u_ctx, u_lat, rpb_l, pos_r, pos_c, ctx_out):
    kc = split_heads(cols(u_ctx, B_K, B_WIDTH), B_HEADS)
    vc = split_heads(cols(u_ctx, B_V, B_WIDTH), B_HEADS)
    q = split_heads(cols(u_lat, B_Q, B_WIDTH), B_HEADS)
    k = split_heads(cols(u_lat, B_K, B_WIDTH), B_HEADS)
    v = split_heads(cols(u_lat, B_V, B_WIDTH), B_HEADS)
    o_lat = neighbourhood_attention(axial_rope(q, pos_r, pos_c), axial_rope(k, pos_r, pos_c),
                                    v, q, kc, vc, rpb_l)
    b_lat = merge_heads(o_lat).astype(u_lat.dtype)
    if ctx_out:
        qc = split_heads(cols(u_ctx, B_Q, B_WIDTH), B_HEADS)
        b_ctx = merge_heads(context_attention(qc, kc, vc)).astype(u_ctx.dtype)
    else:
        b_ctx = None
    return b_ctx, b_lat


def grouped_moe(h, w_router, b_router, w_gate_l, w_up_l, w_down_l):
    shp = h.shape
    t = h.reshape(-1, shp[-1])
    n = t.shape[0]
    aff = jax.nn.sigmoid((t @ w_router).astype(f32))
    sel = (aff + b_router.astype(f32)).reshape(n, N_GROUPS, EXPERTS_PER_GROUP)
    group_score = lax.top_k(sel, TOP_K)[0].sum(-1)
    g_idx = jnp.argmax(group_score, axis=-1)
    in_grp = jnp.take_along_axis(sel, g_idx[:, None, None], axis=1)[:, 0]
    local = lax.top_k(in_grp, TOP_K)[1]
    e_idx = g_idx[:, None] * EXPERTS_PER_GROUP + local
    a_sel = jnp.take_along_axis(aff, e_idx, axis=1)
    gates = a_sel / jnp.sum(a_sel, axis=-1, keepdims=True)
    combine = jnp.sum(jax.nn.one_hot(e_idx, N_EXPERTS, dtype=f32) * gates[..., None], axis=1)
    out = jnp.zeros((n, shp[-1]), f32)
    for e in range(N_EXPERTS):
        hid = jax.nn.silu(t @ w_gate_l[e]) * (t @ w_up_l[e])
        out = out + combine[:, e:e + 1] * (hid @ w_down_l[e])
    return out.reshape(shp).astype(h.dtype)


def setup_inputs(seed: int = 0) -> dict:
    key = jax.random.key(seed)
    ks = jax.random.split(key, 20)
    D = D_MODEL
    nrm = lambda k, shape, s: jax.random.normal(k, shape, f32) * s
    return {
        'x': nrm(ks[0], (BATCH, SEQ, D), 1.0),
        'c': nrm(ks[1], (BATCH, D), 1.0),
        'ctx': nrm(ks[2], (BATCH, CTX_LEN, D), 1.0),
        'c_ctx': nrm(ks[3], (D,), 1.0),
        'w_ada': nrm(ks[4], (DEPTH, D, 6 * D), 0.5 * D ** -0.5),
        'b_ada': nrm(ks[5], (DEPTH, 6 * D), 0.02),
        'w_in': nrm(ks[6], (DEPTH, D, D_IN), D ** -0.5),
        'lb_logits': nrm(ks[7], (2, DEPTH, A_WIDTH), 0.5),
        'a_norm_g': 1.0 + nrm(ks[8], (DEPTH, A_DV), 0.02),
        'rpb': nrm(ks[9], (DEPTH, B_HEADS, 2 * NA_ROWS - 1, 2 * NA_COLS - 1), 0.05),
        'w_out': nrm(ks[10], (DEPTH, D_MIX, D), BETA * D_MIX ** -0.5),
        'ln1_g': 1.0 + nrm(ks[11], (DEPTH, D), 0.02),
        'ln1_b': nrm(ks[12], (DEPTH, D), 0.02),
        'w_router': nrm(ks[13], (D, N_EXPERTS), D ** -0.5),
        'b_router': nrm(ks[14], (N_EXPERTS,), 0.01),
        'w_gate': nrm(ks[15], (DEPTH, N_EXPERTS, D, D_EXPERT), D ** -0.5),
        'w_up': nrm(ks[16], (DEPTH, N_EXPERTS, D, D_EXPERT), D ** -0.5),
        'w_down': nrm(ks[17], (DEPTH, N_EXPERTS, D_EXPERT, D), BETA * D_EXPERT ** -0.5),
        'ln2_g': 1.0 + nrm(ks[18], (DEPTH, D), 0.02),
        'ln2_b': nrm(ks[19], (DEPTH, D), 0.02),
    }


def reference(x, c, ctx, c_ctx, w_ada, b_ada, w_in, lb_logits, a_norm_g, rpb, w_out, ln1_g, ln1_b,
              w_router, b_router, w_gate, w_up, w_down, ln2_g, ln2_b):
    seq = x.shape[1]
    t = jnp.arange(seq, dtype=jnp.int32)
    pos_r = t // GRID_W
    pos_c = t % GRID_W
    sm = jax.nn.softmax(lb_logits.astype(f32), axis=1)
    lower = jnp.cumsum(sm, axis=1) - sm[:, :1]
    silu_c = jax.nn.silu(c)
    silu_cc = jax.nn.silu(c_ctx)
    x_lat, x_ctx = x, ctx
    for l in range(DEPTH):
        ctx_out = l < DEPTH - 1
        mod_lat = (silu_c @ w_ada[l] + b_ada[l])[:, None, :]
        mod_ctx = silu_cc @ w_ada[l] + b_ada[l]
        sh1, sc1, g1, sh2, sc2, g2 = jnp.split(mod_lat, 6, axis=-1)
        ch1, cc1, cg1, ch2, cc2, cg2 = jnp.split(mod_ctx, 6, axis=-1)
        u_lat = (x_lat * (1.0 + sc1) + sh1) @ w_in[l]
        u_ctx = (x_ctx * (1.0 + cc1) + ch1) @ w_in[l]
        a_ctx, a_lat = hgrn2_mixer(u_ctx, u_lat, lower[0, l], lower[1, l], a_norm_g[l], ctx_out)
        b_ctx, b_lat = na_mixer(u_ctx, u_lat, rpb[l], pos_r, pos_c, ctx_out)
        y_lat = jnp.concatenate([a_lat, b_lat], axis=-1) @ w_out[l]
        x_lat = layer_norm(ALPHA * x_lat + g1 * y_lat, ln1_g[l], ln1_b[l])
        f_lat = grouped_moe(x_lat * (1.0 + sc2) + sh2, w_router, b_router, w_gate[l], w_up[l], w_down[l])
        x_lat = layer_norm(ALPHA * x_lat + g2 * f_lat, ln2_g[l], ln2_b[l])
        if ctx_out:
            y_ctx = jnp.concatenate([a_ctx, b_ctx], axis=-1) @ w_out[l]
            x_ctx = layer_norm(ALPHA * x_ctx + cg1 * y_ctx, ln1_g[l], ln1_b[l])
            f_ctx = grouped_moe(x_ctx * (1.0 + cc2) + ch2, w_router, b_router, w_gate[l], w_up[l], w_down[l])
            x_ctx = layer_norm(ALPHA * x_ctx + cg2 * f_ctx, ln2_g[l], ln2_b[l])
    return x_lat
```

```python
import functools

import numpy as np
import jax
import jax.numpy as jnp
from jax import lax
from jax.experimental import pallas as pl
from jax.experimental.pallas import tpu as pltpu

D_MODEL = 1024
DEPTH = 4
GRID_W = 64
CTX_LEN = 256
A_HEADS = 4
A_DK = 128
A_WIDTH = A_HEADS * A_DK
B_HEADS = 8
B_HD = 64
B_WIDTH = B_HEADS * B_HD
D_IN = 5 * A_WIDTH + 3 * B_WIDTH
NA_ROWS = 8
NA_COLS = 16
ROPE_BASE = 10000.0
N_EXPERTS = 16
N_GROUPS = 4
EXPERTS_PER_GROUP = N_EXPERTS // N_GROUPS
D_EXPERT = 512
ALPHA = (2 * DEPTH) ** 0.25
LN_EPS = 1e-5
RMS_EPS = 1e-6
F_FLOOR = 1e-6
MASK_VALUE = -1e30

LANES = 128
ROW_TILE = 256
MOE_TILE = 1152
SCAN_CHUNK = 128
SCAN_LEVELS = (64, 32, 16, 8, 4)
SCAN_BASE = 4
VMEM_LIMIT = 56 * 1024 * 1024

f32 = jnp.float32
bf16 = jnp.bfloat16

_NT = (((1,), (1,)), ((), ()))
_TN = (((0,), (0,)), ((), ()))


def _params(*sem):
    return pltpu.CompilerParams(dimension_semantics=sem, vmem_limit_bytes=VMEM_LIMIT)


def _sigmoid_pair(z):
    e = jnp.exp(-jnp.abs(z))
    r = 1.0 / (1.0 + e)
    er = e * r
    pos = z >= 0
    return jnp.where(pos, r, er), jnp.where(pos, er, r)


def _layer_norm(h, g, b):
    mu = jnp.mean(h, axis=-1, keepdims=True)
    d = h - mu
    var = jnp.mean(d * d, axis=-1, keepdims=True)
    return d * lax.rsqrt(var + LN_EPS) * g + b


def _ada_kernel(c_ref, w_ref, b_ref, o_ref):
    c = c_ref[...]
    s, _ = _sigmoid_pair(c)
    o_ref[...] = jnp.dot(c * s, w_ref[...], preferred_element_type=f32,
                         precision=lax.Precision.HIGHEST) + b_ref[...]


def _ada_mods(cc, w_ada, b_ada):
    n_col = 4
    tn = 6 * D_MODEL // n_col
    return pl.pallas_call(
        _ada_kernel,
        grid=(DEPTH, n_col),
        in_specs=[
            pl.BlockSpec((16, D_MODEL), lambda l, j: (0, 0)),
            pl.BlockSpec((None, D_MODEL, tn), lambda l, j: (l, 0, j)),
            pl.BlockSpec((None, 1, tn), lambda l, j: (l, 0, j)),
        ],
        out_specs=pl.BlockSpec((None, 16, tn), lambda l, j: (l, 0, j)),
        out_shape=jax.ShapeDtypeStruct((DEPTH, 16, 6 * D_MODEL), f32),
        compiler_params=_params("parallel", "parallel"),
        name="ada_mods",
    )(cc, w_ada, b_ada.reshape(DEPTH, 1, 6 * D_MODEL))


def _in_kernel(*refs, with_ln):
    if with_ln:
        x_ref, f_ref, pmod_ref, g_ref, b_ref, mod_ref, w_ref, xo_ref, u_ref = refs
        h = ALPHA * x_ref[...] + pmod_ref[5:6, :] * f_ref[...]
        x = _layer_norm(h, g_ref[...], b_ref[...])
        xo_ref[...] = x
    else:
        x_ref, mod_ref, w_ref, u_ref = refs
        x = x_ref[...]
    xm = x * (1.0 + mod_ref[1:2, :]) + mod_ref[0:1, :]
    u_ref[...] = jnp.dot(xm.astype(bf16), w_ref[...], preferred_element_type=f32)


def _mod_spec():
    return pl.BlockSpec((None, None, 6, D_MODEL), lambda b, j: (b, jnp.minimum(j, 1), 0, 0))


def _row_spec(width):
    return pl.BlockSpec((None, ROW_TILE, width), lambda b, j: (b, j, 0))


def _vec_spec(l):
    return pl.BlockSpec((None, 1, D_MODEL), lambda b, j: (l, 0, 0))


def _in_proj(l, x, f, pmod, ln_g, ln_b, mod, w_in):
    B, L, _ = x.shape
    w_spec = pl.BlockSpec((None, D_MODEL, D_IN), lambda b, j: (l, 0, 0))
    u_shape = jax.ShapeDtypeStruct((B, L, D_IN), f32)
    if f is None:
        u = pl.pallas_call(
            functools.partial(_in_kernel, with_ln=False),
            grid=(B, L // ROW_TILE),
            in_specs=[_row_spec(D_MODEL), _mod_spec(), w_spec],
            out_specs=_row_spec(D_IN),
            out_shape=u_shape,
            compiler_params=_params("parallel", "parallel"),
            name="in_proj0",
        )(x, mod, w_in)
        return x, u
    return pl.pallas_call(
        functools.partial(_in_kernel, with_ln=True),
        grid=(B, L // ROW_TILE),
        in_specs=[_row_spec(D_MODEL), _row_spec(D_MODEL), _mod_spec(), _vec_spec(l - 1), _vec_spec(l - 1),
                  _mod_spec(), w_spec],
        out_specs=[_row_spec(D_MODEL), _row_spec(D_IN)],
        out_shape=[jax.ShapeDtypeStruct(x.shape, f32), u_shape],
        compiler_params=_params("parallel", "parallel"),
        name="in_proj",
    )(x, f, pmod, ln_g, ln_b, mod, w_in)


def _scan_constants():
    C = SCAN_CHUNK
    p = np.arange(C)
    r = np.arange(C)
    mats = [(r[None, :] <= p[:, None]).astype(np.float32)]
    masks = []
    for m in SCAN_LEVELS:
        ref = (p // (2 * m)) * 2 * m + m - 1
        plus = (r[None, :] > ref[:, None]) & (r[None, :] <= p[:, None])
        minus = (r[None, :] > p[:, None]) & (r[None, :] <= ref[:, None])
        mats.append(plus.astype(np.float32) - minus.astype(np.float32))
        same = (p[:, None] // (2 * m)) == (p[None, :] // (2 * m))
        masks.append(same & ((p[:, None] // m) % 2 == 1) & ((p[None, :] // m) % 2 == 0))
    start = (p // SCAN_BASE) * SCAN_BASE
    mats.append(((r[None, :] > start[:, None]) & (r[None, :] <= p[:, None])).astype(np.float32))
    masks.append(((p[:, None] // SCAN_BASE) == (p[None, :] // SCAN_BASE)) & (p[None, :] <= p[:, None]))
    t_fwd = np.concatenate(mats, axis=0)
    m_fwd = np.stack(masks).astype(np.float32)
    t_bwd = np.concatenate([m_[::-1, ::-1] for m_ in mats], axis=0)
    m_bwd = m_fwd[:, ::-1, ::-1]
    return np.stack([t_fwd, t_bwd]), np.stack([m_fwd, m_bwd])


def _hgrn_kernel(q_ref, ff_ref, fb_ref, v_ref, g_ref, lbf_ref, lbb_ref, ng_ref, t_ref, m_ref,
                 a_ref, of_scr, *, n_ctx, n_all):
    C = SCAN_CHUNK
    n_lv = len(SCAN_LEVELS)

    def chunk_step(rows, z_ref, lb, S, d):
        sp, sn = _sigmoid_pair(z_ref[rows, :])
        lf = jnp.log(jnp.maximum(lb + (1.0 - lb) * sp, F_FLOOR))
        k = (1.0 - lb) * sn
        qz = q_ref[rows, :]
        q = qz * _sigmoid_pair(qz)[0]
        v = v_ref[rows, :].astype(bf16)
        hi = lf.astype(bf16)
        r1 = lf - hi.astype(f32)
        mid = r1.astype(bf16)
        lo = (r1 - mid.astype(f32)).astype(bf16)
        d3 = jnp.dot(t_ref[d], jnp.concatenate([hi, mid, lo], axis=1), preferred_element_type=f32)
        dd = d3[:, :LANES] + d3[:, LANES:2 * LANES] + d3[:, 2 * LANES:]
        cum = dd[0:C]
        last = cum[C - 1:C, :] if d == 0 else cum[0:1, :]
        o = lax.dot_general((q * jnp.exp(cum)).astype(bf16), S.astype(bf16), _NT,
                            preferred_element_type=f32)
        scores = jnp.zeros((C, C), f32)
        for li in range(n_lv + 1):
            dl = dd[(1 + li) * C:(2 + li) * C]
            if li < n_lv:
                wq = jnp.exp(-jnp.abs(dl))
                wk = wq
            else:
                wq = jnp.exp(dl)
                wk = jnp.exp(-dl)
            a = lax.dot_general((q * wq).astype(bf16), (k * wk).astype(bf16), _NT,
                                preferred_element_type=f32)
            scores = scores + jnp.where(m_ref[d, li] > 0.5, a, 0.0)
        o = o + jnp.dot(scores.astype(bf16), v, preferred_element_type=f32)
        kdec = (k * jnp.exp(last - cum)).astype(bf16)
        S = S * jnp.exp(last) + lax.dot_general(v, kdec, _TN, preferred_element_type=f32)
        return o, S

    lbf = lbf_ref[...]
    lbb = lbb_ref[...]

    def fwd_body(i, S):
        rows = pl.ds(pl.multiple_of(i * C, C), C)
        o, S = chunk_step(rows, ff_ref, lbf, S, 0)
        of_scr[rows, :] = o
        return S

    lax.fori_loop(0, n_all, fwd_body, jnp.zeros((A_DK, A_DK), f32))

    ng = ng_ref[...]

    def bwd_body(i, S):
        c = jnp.where(i < n_ctx, n_ctx - 1 - i, n_all - 1 - (i - n_ctx))
        rows = pl.ds(pl.multiple_of(c * C, C), C)
        o, S = chunk_step(rows, fb_ref, lbb, S, 1)
        o = o + of_scr[rows, :]
        o = o * lax.rsqrt(jnp.mean(o * o, axis=-1, keepdims=True) + RMS_EPS) * ng
        gz = g_ref[rows, :]
        a_ref[rows, :] = (o * (gz * _sigmoid_pair(gz)[0])).astype(a_ref.dtype)
        return S

    lax.fori_loop(0, n_all, bwd_body, jnp.zeros((A_DK, A_DK), f32))


def _hgrn(u, lb_f, lb_b, norm_g, t_all, m_all):
    B, L, _ = u.shape
    hb = A_WIDTH // LANES

    def col(sec):
        return pl.BlockSpec((None, L, LANES), lambda b, h: (b, 0, sec * hb + h))

    lb_spec = pl.BlockSpec((None, 1, LANES), lambda b, h: (h, 0, 0))
    n_rows_t = t_all.shape[1]
    n_masks = m_all.shape[1]
    return pl.pallas_call(
        functools.partial(_hgrn_kernel, n_ctx=CTX_LEN // SCAN_CHUNK, n_all=L // SCAN_CHUNK),
        grid=(B, A_HEADS),
        in_specs=[col(0), col(1), col(2), col(3), col(4), lb_spec, lb_spec,
                  pl.BlockSpec((1, LANES), lambda b, h: (0, 0)),
                  pl.BlockSpec((2, n_rows_t, SCAN_CHUNK), lambda b, h: (0, 0, 0)),
                  pl.BlockSpec((2, n_masks, SCAN_CHUNK, SCAN_CHUNK), lambda b, h: (0, 0, 0, 0))],
        out_specs=pl.BlockSpec((None, L, LANES), lambda b, h: (b, 0, h)),
        out_shape=jax.ShapeDtypeStruct((B, L, A_WIDTH), bf16),
        scratch_shapes=[pltpu.VMEM((L, LANES), f32)],
        compiler_params=_params("parallel", "parallel"),
        name="hgrn2",
    )(u, u, u, u, u, lb_f.reshape(A_HEADS, 1, LANES), lb_b.reshape(A_HEADS, 1, LANES),
      norm_g.reshape(1, LANES), t_all, m_all)


def _rope_tables(seq):
    t = np.arange(seq)
    pos_r = (t // GRID_W).astype(np.float32)
    pos_c = (t % GRID_W).astype(np.float32)
    quarter = B_HD // 4
    inv = (ROPE_BASE ** (-np.arange(quarter, dtype=np.float32) / quarter)).astype(np.float32)
    lane = np.arange(LANES)
    in_head = lane % B_HD
    pos = np.where((in_head < B_HD // 2)[None, :], pos_r[:, None], pos_c[:, None])
    ang = (pos * inv[lane % quarter][None, :]).astype(np.float32)
    sign = np.where((lane % (2 * quarter)) < quarter, -1.0, 1.0)
    return np.cos(ang).astype(np.float32), (np.sin(ang) * sign[None, :]).astype(np.float32)


def _window_mask():
    c = np.arange(GRID_W)
    cs = np.clip(c - NA_COLS // 2, 0, GRID_W - NA_COLS)
    in_win = (c[None, :] >= cs[:, None]) & (c[None, :] < cs[:, None] + NA_COLS)
    return np.tile(in_win, (1, NA_ROWS)).astype(np.float32)


def _bias_tables(rpb):
    o = np.arange(NA_ROWS)
    j = np.arange(NA_ROWS)
    c = np.arange(GRID_W)
    dr = o[:, None] + j[None, :]
    dc = np.clip(c[None, :] - c[:, None], -(NA_COLS - 1), NA_COLS - 1) + (NA_COLS - 1)
    t = rpb[:, :, dr[:, None, :, None], dc[None, :, None, :]]
    return t.reshape(DEPTH, B_HEADS, NA_ROWS, GRID_W, NA_ROWS * GRID_W)


def _na_kernel(q_ref, k_ref, v_ref, cos_ref, sin_ref, bias_ref, mask_ref, o_ref,
               qr_scr, kr_scr, qp_scr, kc_scr, v_scr, *, seq, rows_grid):
    scale = B_HD ** -0.5
    hd = B_HD
    n_win = NA_ROWS * GRID_W
    lane = lax.broadcasted_iota(jnp.int32, (ROW_TILE, LANES), 1)
    first_half = (lane % (hd // 2)) < (hd // 4)

    def swap_halves(x):
        return jnp.where(first_half, pltpu.roll(x, LANES - hd // 4, axis=1), pltpu.roll(x, hd // 4, axis=1))

    def split_store(dst, rows, x):
        xb = x.astype(bf16)
        dst[0, rows, :] = xb[:, :hd]
        dst[1, rows, :] = xb[:, hd:]

    ctx_rows = pl.ds(0, CTX_LEN)
    split_store(qp_scr, ctx_rows, q_ref[ctx_rows, :] * scale)
    split_store(kc_scr, ctx_rows, k_ref[ctx_rows, :])
    split_store(v_scr, ctx_rows, v_ref[ctx_rows, :])

    def prep(i, carry):
        lrows = pl.ds(pl.multiple_of(i * ROW_TILE, ROW_TILE), ROW_TILE)
        rows = pl.ds(pl.multiple_of(CTX_LEN + i * ROW_TILE, ROW_TILE), ROW_TILE)
        q = q_ref[rows, :] * scale
        k = k_ref[rows, :]
        cos = cos_ref[lrows, :]
        sin = sin_ref[lrows, :]
        split_store(qp_scr, rows, q)
        split_store(v_scr, rows, v_ref[rows, :])
        split_store(qr_scr, lrows, q * cos + swap_halves(q) * sin)
        split_store(kr_scr, lrows, k * cos + swap_halves(k) * sin)
        return carry

    lax.fori_loop(0, seq // ROW_TILE, prep, 0)

    outs = []
    for h in range(2):
        s = lax.dot_general(qp_scr[h, ctx_rows, :], kc_scr[h], _NT, preferred_element_type=f32)
        e = jnp.exp(s - jnp.max(s, axis=-1, keepdims=True))
        o = jnp.dot(e.astype(bf16), v_scr[h, ctx_rows, :], preferred_element_type=f32)
        outs.append(o / jnp.sum(e, axis=-1, keepdims=True))
    o_ref[ctx_rows, :] = jnp.concatenate(outs, axis=1).astype(o_ref.dtype)

    def row_body(r, carry):
        rs = jnp.clip(r - NA_ROWS // 2, 0, rows_grid - NA_ROWS)
        off = rs - r + (NA_ROWS - 1)
        qrows = pl.ds(pl.multiple_of(r * GRID_W, GRID_W), GRID_W)
        krows = pl.ds(pl.multiple_of(rs * GRID_W, GRID_W), n_win)
        arows = pl.ds(pl.multiple_of(CTX_LEN + r * GRID_W, GRID_W), GRID_W)
        vrows = pl.ds(pl.multiple_of(CTX_LEN + rs * GRID_W, GRID_W), n_win)
        in_win = mask_ref[...] > 0.5
        outs = []
        for h in range(2):
            s_lat = lax.dot_general(qr_scr[h, qrows, :], kr_scr[h, krows, :], _NT, preferred_element_type=f32)
            s_lat = jnp.where(in_win, s_lat + bias_ref[h, off], MASK_VALUE)
            s_ctx = lax.dot_general(qp_scr[h, arows, :], kc_scr[h], _NT, preferred_element_type=f32)
            m = jnp.maximum(jnp.max(s_lat, axis=-1, keepdims=True), jnp.max(s_ctx, axis=-1, keepdims=True))
            e_lat = jnp.exp(s_lat - m)
            e_ctx = jnp.exp(s_ctx - m)
            den = jnp.sum(e_lat, axis=-1, keepdims=True) + jnp.sum(e_ctx, axis=-1, keepdims=True)
            o = (jnp.dot(e_lat.astype(bf16), v_scr[h, vrows, :], preferred_element_type=f32)
                 + jnp.dot(e_ctx.astype(bf16), v_scr[h, ctx_rows, :], preferred_element_type=f32))
            outs.append(o / den)
        o_ref[arows, :] = jnp.concatenate(outs, axis=1).astype(o_ref.dtype)
        return carry

    lax.fori_loop(0, rows_grid, row_body, 0)


def _na(l, u, cos_t, sin_t, bias_t, mask):
    B, L, _ = u.shape
    seq = L - CTX_LEN
    first = 5 * A_WIDTH // LANES
    pairs = B_WIDTH // LANES

    def col(sec):
        return pl.BlockSpec((None, L, LANES), lambda b, h: (b, 0, first + sec * pairs + h))

    n_win = NA_ROWS * GRID_W
    return pl.pallas_call(
        functools.partial(_na_kernel, seq=seq, rows_grid=seq // GRID_W),
        grid=(B, pairs),
        in_specs=[col(0), col(1), col(2),
                  pl.BlockSpec((seq, LANES), lambda b, h: (0, 0)),
                  pl.BlockSpec((seq, LANES), lambda b, h: (0, 0)),
                  pl.BlockSpec((None, 2, NA_ROWS, GRID_W, n_win), lambda b, h: (l, h, 0, 0, 0)),
                  pl.BlockSpec((GRID_W, n_win), lambda b, h: (0, 0))],
        out_specs=pl.BlockSpec((None, L, LANES), lambda b, h: (b, 0, h)),
        out_shape=jax.ShapeDtypeStruct((B, L, B_WIDTH), bf16),
        scratch_shapes=[pltpu.VMEM((2, seq, B_HD), bf16), pltpu.VMEM((2, seq, B_HD), bf16),
                        pltpu.VMEM((2, L, B_HD), bf16), pltpu.VMEM((2, CTX_LEN, B_HD), bf16),
                        pltpu.VMEM((2, L, B_HD), bf16)],
        compiler_params=_params("parallel", "parallel"),
        name="na_attn",
    )(u, u, u, cos_t, sin_t, bias_t, mask)


def _route(logits, b_router, comb_ref):
    aff, _ = _sigmoid_pair(logits)
    sel = aff + b_router
    s = [sel[e:e + 1, :] for e in range(N_EXPERTS)]
    one = lambda cond: jnp.where(cond, 1.0, 0.0)
    scores = []
    for g in range(N_GROUPS):
        s0, s1, s2, s3 = s[g * EXPERTS_PER_GROUP:(g + 1) * EXPERTS_PER_GROUP]
        hi01, lo01 = jnp.maximum(s0, s1), jnp.minimum(s0, s1)
        hi23, lo23 = jnp.maximum(s2, s3), jnp.minimum(s2, s3)
        scores.append(jnp.maximum(hi01, hi23) + jnp.maximum(jnp.minimum(hi01, hi23), jnp.maximum(lo01, lo23)))
    weights = []
    for g in range(N_GROUPS):
        gsel = None
        for g2 in range(N_GROUPS):
            if g2 == g:
                continue
            t = one(scores[g2] < scores[g]) if g2 < g else one(scores[g2] <= scores[g])
            gsel = t if gsel is None else gsel * t
        for i in range(EXPERTS_PER_GROUP):
            e = g * EXPERTS_PER_GROUP + i
            rank = None
            for j in range(EXPERTS_PER_GROUP):
                if j == i:
                    continue
                e2 = g * EXPERTS_PER_GROUP + j
                t = one(s[e2] >= s[e]) if j < i else one(s[e2] > s[e])
                rank = t if rank is None else rank + t
            weights.append(gsel * one(rank < 1.5) * aff[e:e + 1, :])
    total = weights[0]
    for w in weights[1:]:
        total = total + w
    inv = 1.0 / total
    for e in range(N_EXPERTS):
        comb_ref[e:e + 1, :] = weights[e] * inv


def _out_kernel(a_ref, b_ref, x_ref, mod_ref, wa_ref, wb_ref, g_ref, bt_ref, wr_ref, br_ref,
                x1_ref, xm_ref, comb_ref):
    y = (jnp.dot(a_ref[...], wa_ref[...], preferred_element_type=f32)
         + jnp.dot(b_ref[...], wb_ref[...], preferred_element_type=f32))
    x1 = _layer_norm(ALPHA * x_ref[...] + mod_ref[2:3, :] * y, g_ref[...], bt_ref[...])
    x1_ref[...] = x1
    xm = x1 * (1.0 + mod_ref[4:5, :]) + mod_ref[3:4, :]
    xm_ref[...] = xm.astype(bf16)
    logits = lax.dot_general(wr_ref[...], xm, _NT, preferred_element_type=f32,
                             precision=lax.Precision.HIGHEST)
    _route(logits, br_ref[...], comb_ref)


def _out_proj(l, a, bmix, x, mod, w_out, ln_g, ln_b, w_router_t, b_router):
    B, L, _ = x.shape
    half = A_WIDTH
    return pl.pallas_call(
        _out_kernel,
        grid=(B, L // ROW_TILE),
        in_specs=[_row_spec(A_WIDTH), _row_spec(B_WIDTH), _row_spec(D_MODEL), _mod_spec(),
                  pl.BlockSpec((None, half, D_MODEL), lambda b, j: (l, 0, 0)),
                  pl.BlockSpec((None, half, D_MODEL), lambda b, j: (l, 1, 0)),
                  _vec_spec(l), _vec_spec(l),
                  pl.BlockSpec((N_EXPERTS, D_MODEL), lambda b, j: (0, 0)),
                  pl.BlockSpec((N_EXPERTS, 1), lambda b, j: (0, 0))],
        out_specs=[_row_spec(D_MODEL), _row_spec(D_MODEL),
                   pl.BlockSpec((None, N_EXPERTS, ROW_TILE), lambda b, j: (b, 0, j))],
        out_shape=[jax.ShapeDtypeStruct(x.shape, f32), jax.ShapeDtypeStruct(x.shape, bf16),
                   jax.ShapeDtypeStruct((B, N_EXPERTS, L), f32)],
        compiler_params=_params("parallel", "parallel"),
        name="out_proj",
    )(a, bmix, x, mod, w_out, w_out, ln_g, ln_b, w_router_t, b_router)


def _moe_kernel(x_ref, comb_ref, wg_ref, wu_ref, wd_ref, f_ref):
    e = pl.program_id(1)
    x = x_ref[...]
    hg = jnp.dot(x, wg_ref[...], preferred_element_type=f32)
    hu = jnp.dot(x, wu_ref[...], preferred_element_type=f32)
    comb = comb_ref[...]
    lane = lax.broadcasted_iota(jnp.int32, comb.shape, 1)
    cw = jnp.sum(jnp.where(lane == e, comb, 0.0), axis=1, keepdims=True)
    hid = hg * _sigmoid_pair(hg)[0] * hu * cw
    y = jnp.dot(hid.astype(bf16), wd_ref[...], preferred_element_type=f32)

    @pl.when(e == 0)
    def _():
        f_ref[...] = y

    @pl.when(e > 0)
    def _():
        f_ref[...] += y


def _moe(l, xm, comb_t, w_gate, w_up, w_down):
    n = xm.shape[0]
    return pl.pallas_call(
        _moe_kernel,
        grid=(n // MOE_TILE, N_EXPERTS),
        in_specs=[pl.BlockSpec((MOE_TILE, D_MODEL), lambda i, e: (i, 0)),
                  pl.BlockSpec((MOE_TILE, N_EXPERTS), lambda i, e: (i, 0)),
                  pl.BlockSpec((None, None, D_MODEL, D_EXPERT), lambda i, e: (l, e, 0, 0)),
                  pl.BlockSpec((None, None, D_MODEL, D_EXPERT), lambda i, e: (l, e, 0, 0)),
                  pl.BlockSpec((None, None, D_EXPERT, D_MODEL), lambda i, e: (l, e, 0, 0))],
        out_specs=pl.BlockSpec((MOE_TILE, D_MODEL), lambda i, e: (i, 0)),
        out_shape=jax.ShapeDtypeStruct((n, D_MODEL), f32),
        compiler_params=_params("parallel", "arbitrary"),
        name="moe",
    )(xm, comb_t, w_gate, w_up, w_down)


def _final_kernel(x_ref, f_ref, mod_ref, g_ref, b_ref, o_ref):
    o_ref[...] = _layer_norm(ALPHA * x_ref[...] + mod_ref[5:6, :] * f_ref[...], g_ref[...], b_ref[...])


def _final_ln(x, f, mod, ln_g, ln_b):
    B, L, _ = x.shape
    seq = L - CTX_LEN
    skip = CTX_LEN // ROW_TILE
    lat = pl.BlockSpec((None, ROW_TILE, D_MODEL), lambda b, j: (b, j + skip, 0))
    return pl.pallas_call(
        _final_kernel,
        grid=(B, seq // ROW_TILE),
        in_specs=[lat, lat, pl.BlockSpec((None, None, 6, D_MODEL), lambda b, j: (b, 1, 0, 0)),
                  _vec_spec(DEPTH - 1), _vec_spec(DEPTH - 1)],
        out_specs=_row_spec(D_MODEL),
        out_shape=jax.ShapeDtypeStruct((B, seq, D_MODEL), f32),
        compiler_params=_params("parallel", "parallel"),
        name="final_ln",
    )(x, f, mod, ln_g, ln_b)


def kernel(x, c, ctx, c_ctx, w_ada, b_ada, w_in, lb_logits, a_norm_g, rpb, w_out, ln1_g, ln1_b,
           w_router, b_router, w_gate, w_up, w_down, ln2_g, ln2_b):
    B, seq, D = x.shape
    L = CTX_LEN + seq
    xs = jnp.concatenate([ctx, x], axis=1)

    cc = jnp.concatenate([c, c_ctx[None, :], jnp.zeros((16 - B - 1, D), f32)], axis=0)
    mods = _ada_mods(cc, w_ada, b_ada)
    lat_mod = mods[:, :B].reshape(DEPTH, B, 1, 6, D)
    ctx_mod = jnp.broadcast_to(mods[:, B].reshape(DEPTH, 1, 1, 6, D), (DEPTH, B, 1, 6, D))
    mod_all = jnp.concatenate([ctx_mod, lat_mod], axis=2)

    sm = jax.nn.softmax(lb_logits.astype(f32), axis=1)
    lower = jnp.cumsum(sm, axis=1) - sm[:, :1]

    t_np, m_np = _scan_constants()
    t_all = jnp.asarray(t_np, bf16)
    m_all = jnp.asarray(m_np, f32)
    cos_np, sin_np = _rope_tables(seq)
    cos_t, sin_t = jnp.asarray(cos_np), jnp.asarray(sin_np)
    mask = jnp.asarray(_window_mask())
    bias_t = _bias_tables(rpb.astype(f32))

    w_in_b = w_in.astype(bf16)
    w_out_b = w_out.astype(bf16)
    w_gate_b, w_up_b, w_down_b = w_gate.astype(bf16), w_up.astype(bf16), w_down.astype(bf16)
    w_router_t = w_router.T
    b_router_c = b_router.reshape(N_EXPERTS, 1)
    ln1_g3, ln1_b3 = ln1_g.reshape(DEPTH, 1, D), ln1_b.reshape(DEPTH, 1, D)
    ln2_g3, ln2_b3 = ln2_g.reshape(DEPTH, 1, D), ln2_b.reshape(DEPTH, 1, D)

    f = None
    for l in range(DEPTH):
        pmod = mod_all[l - 1] if l > 0 else None
        xs, u = _in_proj(l, xs, f, pmod, ln2_g3, ln2_b3, mod_all[l], w_in_b)
        a = _hgrn(u, lower[0, l], lower[1, l], a_norm_g[l], t_all, m_all)
        bmix = _na(l, u, cos_t, sin_t, bias_t, mask)
        xs, xm, comb = _out_proj(l, a, bmix, xs, mod_all[l], w_out_b, ln1_g3, ln1_b3, w_router_t, b_router_c)
        comb_t = comb.transpose(0, 2, 1).reshape(B * L, N_EXPERTS)
        f = _moe(l, xm.reshape(B * L, D), comb_t, w_gate_b, w_up_b, w_down_b).reshape(B, L, D)
    return _final_ln(xs, f, mod_all[DEPTH - 1], ln2_g3, ln2_b3)
```

```python
import functools

import numpy as np
import jax
import jax.numpy as jnp
from jax import lax
from jax.experimental import pallas as pl
from jax.experimental.pallas import tpu as pltpu

D_MODEL = 1024
DEPTH = 4
GRID_W = 64
CTX_LEN = 256
A_HEADS = 4
A_DK = 128
A_WIDTH = A_HEADS * A_DK
B_HEADS = 8
B_HD = 64
B_WIDTH = B_HEADS * B_HD
D_IN = 5 * A_WIDTH + 3 * B_WIDTH
NA_ROWS = 8
NA_COLS = 16
ROPE_BASE = 10000.0
N_EXPERTS = 16
N_GROUPS = 4
EXPERTS_PER_GROUP = N_EXPERTS // N_GROUPS
D_EXPERT = 512
ALPHA = (2 * DEPTH) ** 0.25
LN_EPS = 1e-5
RMS_EPS = 1e-6
F_FLOOR = 1e-6
MASK_VALUE = -1e30

LANES = 128
ROW_TILE = 256
MOE_TILE = 1152
SCAN_CHUNK = 128
SCAN_LEVELS = (64, 32, 16, 8, 4)
SCAN_BASE = 4
NA_GROUP = 4
VMEM_LIMIT = 56 * 1024 * 1024

f32 = jnp.float32
bf16 = jnp.bfloat16

_NT = (((1,), (1,)), ((), ()))
_TN = (((0,), (0,)), ((), ()))


def _params(*sem):
    return pltpu.CompilerParams(dimension_semantics=sem, vmem_limit_bytes=VMEM_LIMIT)


def _sigmoid_pair(z):
    e = jnp.exp(-jnp.abs(z))
    r = 1.0 / (1.0 + e)
    er = e * r
    pos = z >= 0
    return jnp.where(pos, r, er), jnp.where(pos, er, r)


def _layer_norm(h, g, b):
    mu = jnp.mean(h, axis=-1, keepdims=True)
    d = h - mu
    var = jnp.mean(d * d, axis=-1, keepdims=True)
    return d * lax.rsqrt(var + LN_EPS) * g + b


def _ada_kernel(c_ref, w_ref, b_ref, o_ref):
    c = c_ref[...]
    s, _ = _sigmoid_pair(c)
    o_ref[...] = jnp.dot(c * s, w_ref[...], preferred_element_type=f32,
                         precision=lax.Precision.HIGHEST) + b_ref[...]


def _ada_mods(cc, w_ada, b_ada):
    n_col = 4
    tn = 6 * D_MODEL // n_col
    return pl.pallas_call(
        _ada_kernel,
        grid=(DEPTH, n_col),
        in_specs=[
            pl.BlockSpec((16, D_MODEL), lambda l, j: (0, 0)),
            pl.BlockSpec((None, D_MODEL, tn), lambda l, j: (l, 0, j)),
            pl.BlockSpec((None, 1, tn), lambda l, j: (l, 0, j)),
        ],
        out_specs=pl.BlockSpec((None, 16, tn), lambda l, j: (l, 0, j)),
        out_shape=jax.ShapeDtypeStruct((DEPTH, 16, 6 * D_MODEL), f32),
        compiler_params=_params("parallel", "parallel"),
        name="ada_mods",
    )(cc, w_ada, b_ada.reshape(DEPTH, 1, 6 * D_MODEL))


def _in_kernel(*refs, with_ln):
    if with_ln:
        x_ref, f_ref, pmod_ref, g_ref, b_ref, mod_ref, w_ref, xo_ref, u_ref = refs
        h = ALPHA * x_ref[...] + pmod_ref[5:6, :] * f_ref[...]
        x = _layer_norm(h, g_ref[...], b_ref[...])
        xo_ref[...] = x
    else:
        x_ref, mod_ref, w_ref, u_ref = refs
        x = x_ref[...]
    xm = x * (1.0 + mod_ref[1:2, :]) + mod_ref[0:1, :]
    u_ref[...] = jnp.dot(xm.astype(bf16), w_ref[...], preferred_element_type=f32)


def _mod_spec():
    return pl.BlockSpec((None, None, 6, D_MODEL), lambda b, j: (b, jnp.minimum(j, 1), 0, 0))


def _row_spec(width):
    return pl.BlockSpec((None, ROW_TILE, width), lambda b, j: (b, j, 0))


def _vec_spec(l):
    return pl.BlockSpec((None, 1, D_MODEL), lambda b, j: (l, 0, 0))


def _in_proj(l, x, f, pmod, ln_g, ln_b, mod, w_in):
    B, L, _ = x.shape
    w_spec = pl.BlockSpec((None, D_MODEL, D_IN), lambda b, j: (l, 0, 0))
    u_shape = jax.ShapeDtypeStruct((B, L, D_IN), f32)
    if f is None:
        u = pl.pallas_call(
            functools.partial(_in_kernel, with_ln=False),
            grid=(B, L // ROW_TILE),
            in_specs=[_row_spec(D_MODEL), _mod_spec(), w_spec],
            out_specs=_row_spec(D_IN),
            out_shape=u_shape,
            compiler_params=_params("parallel", "parallel"),
            name="in_proj0",
        )(x, mod, w_in)
        return x, u
    return pl.pallas_call(
        functools.partial(_in_kernel, with_ln=True),
        grid=(B, L // ROW_TILE),
        in_specs=[_row_spec(D_MODEL), _row_spec(D_MODEL), _mod_spec(), _vec_spec(l - 1), _vec_spec(l - 1),
                  _mod_spec(), w_spec],
        out_specs=[_row_spec(D_MODEL), _row_spec(D_IN)],
        out_shape=[jax.ShapeDtypeStruct(x.shape, f32), u_shape],
        compiler_params=_params("parallel", "parallel"),
        name="in_proj",
    )(x, f, pmod, ln_g, ln_b, mod, w_in)


def _scan_constants():
    C = SCAN_CHUNK
    p = np.arange(C)
    tri = (p[None, :] <= p[:, None]).astype(np.float32)
    owner = np.full((C, C), -1, np.int32)
    signs = []
    for li, m in enumerate(SCAN_LEVELS):
        same = (p[:, None] // (2 * m)) == (p[None, :] // (2 * m))
        owner[same & ((p[:, None] // m) % 2 == 1) & ((p[None, :] // m) % 2 == 0)] = li
        signs.append(np.where((p // m) % 2 == 1, 1.0, -1.0))
    owner[((p[:, None] // SCAN_BASE) == (p[None, :] // SCAN_BASE)) & (p[None, :] <= p[:, None])] = len(SCAN_LEVELS)
    sgn = np.broadcast_to(np.stack(signs)[:, :, None], (len(SCAN_LEVELS), C, LANES)).astype(np.float32)
    return (np.stack([tri, tri[::-1, ::-1]]), np.stack([owner, owner[::-1, ::-1]]),
            np.stack([sgn, sgn[:, ::-1]]))


def _hgrn_kernel(q_ref, ff_ref, fb_ref, v_ref, g_ref, lbf_ref, lbb_ref, ng_ref, t_ref, own_ref, sgn_ref,
                 a_ref, of_scr, ob_scr, *, n_ctx, n_all):
    C = SCAN_CHUNK
    n_lv = len(SCAN_LEVELS)


    def gates(rows, z_ref, lb, d):
        sp, sn = _sigmoid_pair(z_ref[rows, :])
        lf = jnp.log2(jnp.maximum(lb + (1.0 - lb) * sp, F_FLOOR))
        k = (1.0 - lb) * sn
        qz = q_ref[rows, :]
        q = qz * _sigmoid_pair(qz)[0]
        v = v_ref[rows, :].astype(bf16)
        hi = lf.astype(bf16)
        r1 = lf - hi.astype(f32)
        mid = r1.astype(bf16)
        lo = (r1 - mid.astype(f32)).astype(bf16)
        d3 = jnp.dot(t_ref[d], jnp.concatenate([hi, mid, lo], axis=1), preferred_element_type=f32)
        cum = d3[:, :LANES] + d3[:, LANES:2 * LANES] + d3[:, 2 * LANES:]
        return q, k, v, cum

    def intra(q, k, cum, S, d):
        last = cum[C - 1:C, :] if d == 0 else cum[0:1, :]

        def rel(block, ref_rows):
            blocks = cum.reshape(C // block, block, LANES)
            if len(ref_rows) == 1:
                ref = blocks[:, ref_rows[0]:ref_rows[0] + 1, :]
            else:
                first = lax.broadcasted_iota(jnp.int32, (1, block, 1), 1) < block // 2
                ref = jnp.where(first, blocks[:, ref_rows[0]:ref_rows[0] + 1, :],
                                blocks[:, ref_rows[1]:ref_rows[1] + 1, :])
            return (blocks - ref).reshape(C, LANES)

        o = lax.dot_general((q * jnp.exp2(cum)).astype(bf16), S.astype(bf16), _NT,
                            preferred_element_type=f32)
        own = own_ref[d]
        scores = jnp.zeros((C, C), f32)
        for li in range(n_lv + 1):
            if li < n_lv:
                m = SCAN_LEVELS[li]
                wq = jnp.exp2(rel(2 * m, (m - 1,) if d == 0 else (m,)) * sgn_ref[d, li])
                wk = wq
            else:
                nb = SCAN_BASE
                dl = rel(2 * nb, (0, nb) if d == 0 else (nb - 1, 2 * nb - 1))
                wq = jnp.exp2(dl)
                wk = jnp.exp2(-dl)
            a = lax.dot_general((q * wq).astype(bf16), (k * wk).astype(bf16), _NT,
                                preferred_element_type=f32)
            scores = jnp.where(own == li, a, scores)
        kdec = (k * jnp.exp2(last - cum)).astype(bf16)
        return o, scores.astype(bf16), kdec, jnp.exp2(last)

    def finish(o, scores, v, kdec, decay, S):
        o = o + jnp.dot(scores, v, preferred_element_type=f32)
        S = S * decay + lax.dot_general(v, kdec, _TN, preferred_element_type=f32)
        return o, S

    lbf = lbf_ref[...]
    lbb = lbb_ref[...]

    def scan_body(i, carry):
        s_f, s_b = carry
        rows_f = pl.ds(pl.multiple_of(i * C, C), C)
        c = jnp.where(i < n_ctx, n_ctx - 1 - i, n_all - 1 - (i - n_ctx))
        rows_b = pl.ds(pl.multiple_of(c * C, C), C)
        qf, kf, vf, cum_f = gates(rows_f, ff_ref, lbf, 0)
        qb, kb, vb, cum_b = gates(rows_b, fb_ref, lbb, 1)
        part_f = intra(qf, kf, cum_f, s_f, 0)
        part_b = intra(qb, kb, cum_b, s_b, 1)
        o_f, s_f = finish(part_f[0], part_f[1], vf, part_f[2], part_f[3], s_f)
        o_b, s_b = finish(part_b[0], part_b[1], vb, part_b[2], part_b[3], s_b)
        of_scr[rows_f, :] = o_f
        ob_scr[rows_b, :] = o_b
        return s_f, s_b

    zero = jnp.zeros((A_DK, A_DK), f32)
    lax.fori_loop(0, n_all, scan_body, (zero, zero), unroll=2)

    ng = ng_ref[...]

    def readout_body(i, carry):
        rows = pl.ds(pl.multiple_of(i * ROW_TILE, ROW_TILE), ROW_TILE)
        o = of_scr[rows, :] + ob_scr[rows, :]
        o = o * lax.rsqrt(jnp.mean(o * o, axis=-1, keepdims=True) + RMS_EPS) * ng
        gz = g_ref[rows, :]
        a_ref[rows, :] = (o * (gz * _sigmoid_pair(gz)[0])).astype(a_ref.dtype)
        return carry

    lax.fori_loop(0, n_all * C // ROW_TILE, readout_body, 0)


def _hgrn(u, lb_f, lb_b, norm_g, scan_consts):
    B, L, _ = u.shape
    hb = A_WIDTH // LANES

    def col(sec):
        return pl.BlockSpec((None, L, LANES), lambda b, h: (b, 0, sec * hb + h))

    def whole(arr):
        return pl.BlockSpec(arr.shape, lambda b, h: (0,) * arr.ndim)

    lb_spec = pl.BlockSpec((None, 1, LANES), lambda b, h: (h, 0, 0))
    return pl.pallas_call(
        functools.partial(_hgrn_kernel, n_ctx=CTX_LEN // SCAN_CHUNK, n_all=L // SCAN_CHUNK),
        grid=(B, A_HEADS),
        in_specs=[col(0), col(1), col(2), col(3), col(4), lb_spec, lb_spec,
                  pl.BlockSpec((1, LANES), lambda b, h: (0, 0))] + [whole(a) for a in scan_consts],
        out_specs=pl.BlockSpec((None, L, LANES), lambda b, h: (b, 0, h)),
        out_shape=jax.ShapeDtypeStruct((B, L, A_WIDTH), bf16),
        scratch_shapes=[pltpu.VMEM((L, LANES), f32), pltpu.VMEM((L, LANES), f32)],
        compiler_params=_params("parallel", "parallel"),
        name="hgrn2",
    )(u, u, u, u, u, lb_f.reshape(A_HEADS, 1, LANES), lb_b.reshape(A_HEADS, 1, LANES),
      norm_g.reshape(1, LANES), *scan_consts)


def _rope_tables(seq):
    t = np.arange(seq)
    pos_r = (t // GRID_W).astype(np.float32)
    pos_c = (t % GRID_W).astype(np.float32)
    quarter = B_HD // 4
    inv = (ROPE_BASE ** (-np.arange(quarter, dtype=np.float32) / quarter)).astype(np.float32)
    lane = np.arange(LANES)
    in_head = lane % B_HD
    pos = np.where((in_head < B_HD // 2)[None, :], pos_r[:, None], pos_c[:, None])
    ang = (pos * inv[lane % quarter][None, :]).astype(np.float32)
    sign = np.where((lane % (2 * quarter)) < quarter, -1.0, 1.0)
    return np.cos(ang).astype(np.float32), (np.sin(ang) * sign[None, :]).astype(np.float32)


def _window_mask():
    c = np.arange(GRID_W)
    cs = np.clip(c - NA_COLS // 2, 0, GRID_W - NA_COLS)
    in_win = (c[None, :] >= cs[:, None]) & (c[None, :] < cs[:, None] + NA_COLS)
    return np.tile(in_win, (1, NA_ROWS)).astype(np.float32)


def _bias_tables(rpb):
    c = np.arange(GRID_W)
    dc = np.clip(c[None, :] - c[:, None], -(NA_COLS - 1), NA_COLS - 1) + (NA_COLS - 1)
    pick = (np.arange(2 * NA_COLS - 1)[:, None, None] == dc[None]).astype(np.float32)
    cols = jnp.einsum('lhdk,kcx->lhdcx', rpb, jnp.asarray(pick), precision=lax.Precision.HIGHEST)
    t = jnp.stack([cols[:, :, o:o + NA_ROWS] for o in range(NA_ROWS)], axis=2)
    t = t.transpose(0, 1, 2, 4, 3, 5)
    return t.reshape(DEPTH, B_HEADS, NA_ROWS, GRID_W, NA_ROWS * GRID_W)


def _na_kernel(q_ref, k_ref, v_ref, cos_ref, sin_ref, bias_ref, mask_ref, o_ref,
               qr_scr, kr_scr, qp_scr, kc_scr, v_scr, *, seq, rows_grid):
    scale = B_HD ** -0.5
    hd = B_HD
    n_win = NA_ROWS * GRID_W
    lane = lax.broadcasted_iota(jnp.int32, (ROW_TILE, LANES), 1)
    first_half = (lane % (hd // 2)) < (hd // 4)

    def swap_halves(x):
        return jnp.where(first_half, pltpu.roll(x, LANES - hd // 4, axis=1), pltpu.roll(x, hd // 4, axis=1))

    def split_store(dst, rows, x):
        xb = x.astype(bf16)
        dst[0, rows, :] = xb[:, :hd]
        dst[1, rows, :] = xb[:, hd:]

    ctx_rows = pl.ds(0, CTX_LEN)
    split_store(qp_scr, ctx_rows, q_ref[ctx_rows, :] * scale)
    split_store(kc_scr, ctx_rows, k_ref[ctx_rows, :])
    split_store(v_scr, ctx_rows, v_ref[ctx_rows, :])

    def prep(i, carry):
        lrows = pl.ds(pl.multiple_of(i * ROW_TILE, ROW_TILE), ROW_TILE)
        rows = pl.ds(pl.multiple_of(CTX_LEN + i * ROW_TILE, ROW_TILE), ROW_TILE)
        q = q_ref[rows, :] * scale
        k = k_ref[rows, :]
        cos = cos_ref[lrows, :]
        sin = sin_ref[lrows, :]
        split_store(qp_scr, rows, q)
        split_store(v_scr, rows, v_ref[rows, :])
        split_store(qr_scr, lrows, q * cos + swap_halves(q) * sin)
        split_store(kr_scr, lrows, k * cos + swap_halves(k) * sin)
        return carry

    lax.fori_loop(0, seq // ROW_TILE, prep, 0)

    outs = []
    for h in range(2):
        s = lax.dot_general(qp_scr[h, ctx_rows, :], kc_scr[h], _NT, preferred_element_type=f32)
        e = jnp.exp(s - jnp.max(s, axis=-1, keepdims=True))
        o = jnp.dot(e.astype(bf16), v_scr[h, ctx_rows, :], preferred_element_type=f32)
        outs.append(o / jnp.sum(e, axis=-1, keepdims=True))
    o_ref[ctx_rows, :] = jnp.concatenate(outs, axis=1).astype(o_ref.dtype)

    def rows_body(it, carry):
        chains = []
        for g in range(NA_GROUP):
            r = it * NA_GROUP + g
            rs = jnp.clip(r - NA_ROWS // 2, 0, rows_grid - NA_ROWS)
            for h in range(2):
                chains.append(dict(
                    h=h, off=rs - r + (NA_ROWS - 1),
                    qrows=pl.ds(pl.multiple_of(r * GRID_W, GRID_W), GRID_W),
                    krows=pl.ds(pl.multiple_of(rs * GRID_W, GRID_W), n_win),
                    arows=pl.ds(pl.multiple_of(CTX_LEN + r * GRID_W, GRID_W), GRID_W),
                    vrows=pl.ds(pl.multiple_of(CTX_LEN + rs * GRID_W, GRID_W), n_win)))
        for c in chains:
            h = c["h"]
            c["s_lat"] = lax.dot_general(qr_scr[h, c["qrows"], :], kr_scr[h, c["krows"], :], _NT,
                                         preferred_element_type=f32)
            c["s_ctx"] = lax.dot_general(qp_scr[h, c["arows"], :], kc_scr[h], _NT, preferred_element_type=f32)
        for c in chains:
            s_lat = jnp.where(mask_ref[...] > 0.5, c["s_lat"] + bias_ref[c["h"], c["off"]], MASK_VALUE)
            m = jnp.maximum(jnp.max(s_lat, axis=-1, keepdims=True), jnp.max(c["s_ctx"], axis=-1, keepdims=True))
            e_lat = jnp.exp(s_lat - m)
            e_ctx = jnp.exp(c["s_ctx"] - m)
            c["den"] = jnp.sum(e_lat, axis=-1, keepdims=True) + jnp.sum(e_ctx, axis=-1, keepdims=True)
            c["e_lat"] = e_lat.astype(bf16)
            c["e_ctx"] = e_ctx.astype(bf16)
        for c in chains:
            h = c["h"]
            c["o"] = (jnp.dot(c["e_lat"], v_scr[h, c["vrows"], :], preferred_element_type=f32)
                      + jnp.dot(c["e_ctx"], v_scr[h, ctx_rows, :], preferred_element_type=f32))
        for g in range(NA_GROUP):
            pair = chains[2 * g:2 * g + 2]
            o_ref[pair[0]["arows"], :] = jnp.concatenate([c["o"] / c["den"] for c in pair],
                                                         axis=1).astype(o_ref.dtype)
        return carry

    lax.fori_loop(0, rows_grid // NA_GROUP, rows_body, 0)


def _na(l, u, cos_t, sin_t, bias_t, mask):
    B, L, _ = u.shape
    seq = L - CTX_LEN
    first = 5 * A_WIDTH // LANES
    pairs = B_WIDTH // LANES

    def col(sec):
        return pl.BlockSpec((None, L, LANES), lambda b, h: (b, 0, first + sec * pairs + h))

    n_win = NA_ROWS * GRID_W
    return pl.pallas_call(
        functools.partial(_na_kernel, seq=seq, rows_grid=seq // GRID_W),
        grid=(B, pairs),
        in_specs=[col(0), col(1), col(2),
                  pl.BlockSpec((seq, LANES), lambda b, h: (0, 0)),
                  pl.BlockSpec((seq, LANES), lambda b, h: (0, 0)),
                  pl.BlockSpec((None, 2, NA_ROWS, GRID_W, n_win), lambda b, h: (l, h, 0, 0, 0)),
                  pl.BlockSpec((GRID_W, n_win), lambda b, h: (0, 0))],
        out_specs=pl.BlockSpec((None, L, LANES), lambda b, h: (b, 0, h)),
        out_shape=jax.ShapeDtypeStruct((B, L, B_WIDTH), bf16),
        scratch_shapes=[pltpu.VMEM((2, seq, B_HD), bf16), pltpu.VMEM((2, seq, B_HD), bf16),
                        pltpu.VMEM((2, L, B_HD), bf16), pltpu.VMEM((2, CTX_LEN, B_HD), bf16),
                        pltpu.VMEM((2, L, B_HD), bf16)],
        compiler_params=_params("parallel", "parallel"),
        name="na_attn",
    )(u, u, u, cos_t, sin_t, bias_t, mask)


def _route(logits, b_router, comb_ref):
    aff, _ = _sigmoid_pair(logits)
    sel = aff + b_router
    s = [sel[e:e + 1, :] for e in range(N_EXPERTS)]
    one = lambda cond: jnp.where(cond, 1.0, 0.0)
    scores = []
    for g in range(N_GROUPS):
        s0, s1, s2, s3 = s[g * EXPERTS_PER_GROUP:(g + 1) * EXPERTS_PER_GROUP]
        hi01, lo01 = jnp.maximum(s0, s1), jnp.minimum(s0, s1)
        hi23, lo23 = jnp.maximum(s2, s3), jnp.minimum(s2, s3)
        scores.append(jnp.maximum(hi01, hi23) + jnp.maximum(jnp.minimum(hi01, hi23), jnp.maximum(lo01, lo23)))
    weights = []
    for g in range(N_GROUPS):
        gsel = None
        for g2 in range(N_GROUPS):
            if g2 == g:
                continue
            t = one(scores[g2] < scores[g]) if g2 < g else one(scores[g2] <= scores[g])
            gsel = t if gsel is None else gsel * t
        for i in range(EXPERTS_PER_GROUP):
            e = g * EXPERTS_PER_GROUP + i
            rank = None
            for j in range(EXPERTS_PER_GROUP):
                if j == i:
                    continue
                e2 = g * EXPERTS_PER_GROUP + j
                t = one(s[e2] >= s[e]) if j < i else one(s[e2] > s[e])
                rank = t if rank is None else rank + t
            weights.append(gsel * one(rank < 1.5) * aff[e:e + 1, :])
    total = weights[0]
    for w in weights[1:]:
        total = total + w
    inv = 1.0 / total
    for e in range(N_EXPERTS):
        comb_ref[e:e + 1, :] = weights[e] * inv


def _out_kernel(a_ref, b_ref, x_ref, mod_ref, wa_ref, wb_ref, g_ref, bt_ref, wr_ref, br_ref,
                x1_ref, xm_ref, comb_ref):
    y = (jnp.dot(a_ref[...], wa_ref[...], preferred_element_type=f32)
         + jnp.dot(b_ref[...], wb_ref[...], preferred_element_type=f32))
    x1 = _layer_norm(ALPHA * x_ref[...] + mod_ref[2:3, :] * y, g_ref[...], bt_ref[...])
    x1_ref[...] = x1
    xm = x1 * (1.0 + mod_ref[4:5, :]) + mod_ref[3:4, :]
    xm_ref[...] = xm.astype(bf16)
    logits = lax.dot_general(wr_ref[...], xm, _NT, preferred_element_type=f32,
                             precision=lax.Precision.HIGHEST)
    _route(logits, br_ref[...], comb_ref)


def _out_proj(l, a, bmix, x, mod, w_out, ln_g, ln_b, w_router_t, b_router):
    B, L, _ = x.shape
    half = A_WIDTH
    return pl.pallas_call(
        _out_kernel,
        grid=(B, L // ROW_TILE),
        in_specs=[_row_spec(A_WIDTH), _row_spec(B_WIDTH), _row_spec(D_MODEL), _mod_spec(),
                  pl.BlockSpec((None, half, D_MODEL), lambda b, j: (l, 0, 0)),
                  pl.BlockSpec((None, half, D_MODEL), lambda b, j: (l, 1, 0)),
                  _vec_spec(l), _vec_spec(l),
                  pl.BlockSpec((N_EXPERTS, D_MODEL), lambda b, j: (0, 0)),
                  pl.BlockSpec((N_EXPERTS, 1), lambda b, j: (0, 0))],
        out_specs=[_row_spec(D_MODEL), _row_spec(D_MODEL),
                   pl.BlockSpec((None, N_EXPERTS, ROW_TILE), lambda b, j: (b, 0, j))],
        out_shape=[jax.ShapeDtypeStruct(x.shape, f32), jax.ShapeDtypeStruct(x.shape, bf16),
                   jax.ShapeDtypeStruct((B, N_EXPERTS, L), f32)],
        compiler_params=_params("parallel", "parallel"),
        name="out_proj",
    )(a, bmix, x, mod, w_out, w_out, ln_g, ln_b, w_router_t, b_router)


def _moe_kernel(x_ref, comb_ref, wg_ref, wu_ref, wd_ref, f_ref):
    e = pl.program_id(1)
    x = x_ref[...]
    hg = jnp.dot(x, wg_ref[...], preferred_element_type=f32)
    hu = jnp.dot(x, wu_ref[...], preferred_element_type=f32)
    comb = comb_ref[...]
    lane = lax.broadcasted_iota(jnp.int32, comb.shape, 1)
    cw = jnp.sum(jnp.where(lane == e, comb, 0.0), axis=1, keepdims=True)
    hid = hg * _sigmoid_pair(hg)[0] * hu * cw
    y = jnp.dot(hid.astype(bf16), wd_ref[...], preferred_element_type=f32)

    @pl.when(e == 0)
    def _():
        f_ref[...] = y

    @pl.when(e > 0)
    def _():
        f_ref[...] += y


def _moe(l, xm, comb_t, w_gate, w_up, w_down):
    n = xm.shape[0]
    return pl.pallas_call(
        _moe_kernel,
        grid=(n // MOE_TILE, N_EXPERTS),
        in_specs=[pl.BlockSpec((MOE_TILE, D_MODEL), lambda i, e: (i, 0)),
                  pl.BlockSpec((MOE_TILE, N_EXPERTS), lambda i, e: (i, 0)),
                  pl.BlockSpec((None, None, D_MODEL, D_EXPERT), lambda i, e: (l, e, 0, 0)),
                  pl.BlockSpec((None, None, D_MODEL, D_EXPERT), lambda i, e: (l, e, 0, 0)),
                  pl.BlockSpec((None, None, D_EXPERT, D_MODEL), lambda i, e: (l, e, 0, 0))],
        out_specs=pl.BlockSpec((MOE_TILE, D_MODEL), lambda i, e: (i, 0)),
        out_shape=jax.ShapeDtypeStruct((n, D_MODEL), f32),
        compiler_params=_params("parallel", "arbitrary"),
        name="moe",
    )(xm, comb_t, w_gate, w_up, w_down)


def _final_kernel(x_ref, f_ref, mod_ref, g_ref, b_ref, o_ref):
    o_ref[...] = _layer_norm(ALPHA * x_ref[...] + mod_ref[5:6, :] * f_ref[...], g_ref[...], b_ref[...])


def _final_ln(x, f, mod, ln_g, ln_b):
    B, L, _ = x.shape
    seq = L - CTX_LEN
    skip = CTX_LEN // ROW_TILE
    lat = pl.BlockSpec((None, ROW_TILE, D_MODEL), lambda b, j: (b, j + skip, 0))
    return pl.pallas_call(
        _final_kernel,
        grid=(B, seq // ROW_TILE),
        in_specs=[lat, lat, pl.BlockSpec((None, None, 6, D_MODEL), lambda b, j: (b, 1, 0, 0)),
                  _vec_spec(DEPTH - 1), _vec_spec(DEPTH - 1)],
        out_specs=_row_spec(D_MODEL),
        out_shape=jax.ShapeDtypeStruct((B, seq, D_MODEL), f32),
        compiler_params=_params("parallel", "parallel"),
        name="final_ln",
    )(x, f, mod, ln_g, ln_b)


def kernel(x, c, ctx, c_ctx, w_ada, b_ada, w_in, lb_logits, a_norm_g, rpb, w_out, ln1_g, ln1_b,
           w_router, b_router, w_gate, w_up, w_down, ln2_g, ln2_b):
    B, seq, D = x.shape
    L = CTX_LEN + seq
    xs = jnp.concatenate([ctx, x], axis=1)

    cc = jnp.concatenate([c, c_ctx[None, :], jnp.zeros((16 - B - 1, D), f32)], axis=0)
    mods = _ada_mods(cc, w_ada, b_ada)
    lat_mod = mods[:, :B].reshape(DEPTH, B, 1, 6, D)
    ctx_mod = jnp.broadcast_to(mods[:, B].reshape(DEPTH, 1, 1, 6, D), (DEPTH, B, 1, 6, D))
    mod_all = jnp.concatenate([ctx_mod, lat_mod], axis=2)

    sm = jax.nn.softmax(lb_logits.astype(f32), axis=1)
    lower = jnp.cumsum(sm, axis=1) - sm[:, :1]

    tri_np, own_np, sgn_np = _scan_constants()
    scan_consts = (jnp.asarray(tri_np, bf16), jnp.asarray(own_np), jnp.asarray(sgn_np))
    cos_np, sin_np = _rope_tables(seq)
    cos_t, sin_t = jnp.asarray(cos_np), jnp.asarray(sin_np)
    mask = jnp.asarray(_window_mask())
    bias_t = _bias_tables(rpb.astype(f32))

    w_in_b = w_in.astype(bf16)
    w_out_b = w_out.astype(bf16)
    w_gate_b, w_up_b, w_down_b = w_gate.astype(bf16), w_up.astype(bf16), w_down.astype(bf16)
    w_router_t = w_router.T
    b_router_c = b_router.reshape(N_EXPERTS, 1)
    ln1_g3, ln1_b3 = ln1_g.reshape(DEPTH, 1, D), ln1_b.reshape(DEPTH, 1, D)
    ln2_g3, ln2_b3 = ln2_g.reshape(DEPTH, 1, D), ln2_b.reshape(DEPTH, 1, D)

    f = None
    for l in range(DEPTH):
        pmod = mod_all[l - 1] if l > 0 else None
        xs, u = _in_proj(l, xs, f, pmod, ln2_g3, ln2_b3, mod_all[l], w_in_b)
        a = _hgrn(u, lower[0, l], lower[1, l], a_norm_g[l], scan_consts)
        bmix = _na(l, u, cos_t, sin_t, bias_t, mask)
        xs, xm, comb = _out_proj(l, a, bmix, xs, mod_all[l], w_out_b, ln1_g3, ln1_b3, w_router_t, b_router_c)
        comb_t = comb.transpose(0, 2, 1).reshape(B * L, N_EXPERTS)
        f = _moe(l, xm.reshape(B * L, D), comb_t, w_gate_b, w_up_b, w_down_b).reshape(B, L, D)
    return _final_ln(xs, f, mod_all[DEPTH - 1], ln2_g3, ln2_b3)
```

```python
import functools

import numpy as np
import jax
import jax.numpy as jnp
from jax import lax
from jax.experimental import pallas as pl
from jax.experimental.pallas import tpu as pltpu

D_MODEL = 1024
DEPTH = 4
GRID_W = 64
CTX_LEN = 256
A_HEADS = 4
A_DK = 128
A_WIDTH = A_HEADS * A_DK
B_HEADS = 8
B_HD = 64
B_WIDTH = B_HEADS * B_HD
D_IN = 5 * A_WIDTH + 3 * B_WIDTH
NA_ROWS = 8
NA_COLS = 16
ROPE_BASE = 10000.0
N_EXPERTS = 16
N_GROUPS = 4
EXPERTS_PER_GROUP = N_EXPERTS // N_GROUPS
D_EXPERT = 512
ALPHA = (2 * DEPTH) ** 0.25
LN_EPS = 1e-5
RMS_EPS = 1e-6
F_FLOOR = 1e-6
MASK_VALUE = -1e30

LANES = 128
ROW_TILE = 256
XS_BLK = 8
XS_ROWS = 2 * ROW_TILE + N_EXPERTS * XS_BLK
PACK_W = D_MODEL // 2
XS_W = PACK_W + LANES
EXP_TILE = 512
SCAN_CHUNK = 128
SCAN_LEVELS = (64, 32, 16, 8, 4)
SCAN_BASE = 4
NA_GROUP = 4
VMEM_LIMIT = 56 * 1024 * 1024

f32 = jnp.float32
bf16 = jnp.bfloat16

_NT = (((1,), (1,)), ((), ()))
_TN = (((0,), (0,)), ((), ()))


def _params(*sem):
    return pltpu.CompilerParams(dimension_semantics=sem, vmem_limit_bytes=VMEM_LIMIT)


def _sigmoid_pair(z):
    e = jnp.exp(-jnp.abs(z))
    r = 1.0 / (1.0 + e)
    er = e * r
    pos = z >= 0
    return jnp.where(pos, r, er), jnp.where(pos, er, r)


def _layer_norm(h, g, b):
    mu = jnp.mean(h, axis=-1, keepdims=True)
    d = h - mu
    var = jnp.mean(d * d, axis=-1, keepdims=True)
    return d * lax.rsqrt(var + LN_EPS) * g + b


def _ada_kernel(c_ref, w_ref, b_ref, o_ref):
    c = c_ref[...]
    s, _ = _sigmoid_pair(c)
    o_ref[...] = jnp.dot(c * s, w_ref[...], preferred_element_type=f32,
                         precision=lax.Precision.HIGHEST) + b_ref[...]


def _ada_mods(cc, w_ada, b_ada):
    n_col = 4
    tn = 6 * D_MODEL // n_col
    return pl.pallas_call(
        _ada_kernel,
        grid=(DEPTH, n_col),
        in_specs=[
            pl.BlockSpec((16, D_MODEL), lambda l, j: (0, 0)),
            pl.BlockSpec((None, D_MODEL, tn), lambda l, j: (l, 0, j)),
            pl.BlockSpec((None, 1, tn), lambda l, j: (l, 0, j)),
        ],
        out_specs=pl.BlockSpec((None, 16, tn), lambda l, j: (l, 0, j)),
        out_shape=jax.ShapeDtypeStruct((DEPTH, 16, 6 * D_MODEL), f32),
        compiler_params=_params("parallel", "parallel"),
        name="ada_mods",
    )(cc, w_ada, b_ada.reshape(DEPTH, 1, 6 * D_MODEL))


def _pack_pairs(x):
    bits = lax.bitcast_convert_type(x, jnp.uint32)
    half = x.shape[1] // 2
    return (bits[:, :half] >> 16) | (bits[:, half:] & jnp.uint32(0xFFFF0000))


def _unpack_pairs(w):
    lo = lax.bitcast_convert_type(w << 16, f32)
    hi = lax.bitcast_convert_type(w & jnp.uint32(0xFFFF0000), f32)
    return jnp.concatenate([lo, hi], axis=1).astype(bf16)


def _tile_id(j_off):
    b, j = pl.program_id(0), pl.program_id(1)
    nj = pl.num_programs(1)
    tiles_per_batch = nj + j_off
    t = b * tiles_per_batch + j + j_off
    last_j = j == nj - 1
    t_next = jnp.where(last_j, (b + 1) * tiles_per_batch + j_off, t + 1)
    has_next = jnp.logical_not(jnp.logical_and(last_j, b == pl.num_programs(0) - 1))
    return t, t_next, has_next, jnp.logical_and(b == 0, j == 0)


def _combine_experts(tab_ref, pos_ref, ys_hbm, y_vmem, sem, j_off):
    t, t_next, has_next, is_first = _tile_id(j_off)
    step = pl.program_id(0) * pl.num_programs(1) + pl.program_id(1)
    slot = lax.rem(step, 2)

    def block_copy(src_row, dst_row, sl):
        return pltpu.make_async_copy(ys_hbm.at[pl.ds(pl.multiple_of(src_row, XS_BLK), XS_BLK), :],
                                     y_vmem.at[sl, pl.ds(pl.multiple_of(dst_row, XS_BLK), XS_BLK), :],
                                     sem.at[sl])

    def fetch(tile, sl):
        y_vmem[sl] = jnp.zeros((XS_ROWS, PACK_W), jnp.uint32)
        for e in range(N_EXPERTS):
            base, start, nblk = tab_ref[tile, 3 * e], tab_ref[tile, 3 * e + 1], tab_ref[tile, 3 * e + 2]

            def body(i, carry):
                block_copy(base + i * XS_BLK, start + i * XS_BLK, sl).start()
                return carry

            lax.fori_loop(0, nblk, body, 0)

    def wait_all(tile, sl):
        total = tab_ref[tile, 2]
        for e in range(1, N_EXPERTS):
            total = total + tab_ref[tile, 3 * e + 2]

        def body(i, carry):
            block_copy(0, 0, sl).wait()
            return carry

        lax.fori_loop(0, total, body, 0)

    @pl.when(is_first)
    def _():
        fetch(t, slot)

    @pl.when(has_next)
    def _():
        fetch(t_next, 1 - slot)

    wait_all(t, slot)
    y = _unpack_pairs(y_vmem[slot])
    r = lax.broadcasted_iota(jnp.int32, (XS_ROWS, ROW_TILE), 0)
    perm = jnp.where(r == pos_ref[0:1, :], 1.0, jnp.where(r == pos_ref[1:2, :], 1.0, 0.0)).astype(bf16)
    return lax.dot_general(perm, y, _TN, preferred_element_type=f32)


def _in_kernel(*refs, with_ln):
    if with_ln:
        (tab_ref, x_ref, pos_ref, ys_hbm, pmod_ref, g_ref, b_ref, mod_ref, w_ref,
         xo_ref, u_ref, y_vmem, sem) = refs
        f = _combine_experts(tab_ref, pos_ref, ys_hbm, y_vmem, sem, 0)
        h = ALPHA * x_ref[...] + pmod_ref[5:6, :] * f
        x = _layer_norm(h, g_ref[...], b_ref[...])
        xo_ref[...] = x
    else:
        x_ref, mod_ref, w_ref, u_ref = refs
        x = x_ref[...]
    xm = x * (1.0 + mod_ref[1:2, :]) + mod_ref[0:1, :]
    u_ref[...] = jnp.dot(xm.astype(bf16), w_ref[...], preferred_element_type=f32)


def _mod_spec():
    return pl.BlockSpec((None, None, 6, D_MODEL), lambda b, j: (b, jnp.minimum(j, 1), 0, 0))


def _row_spec(width):
    return pl.BlockSpec((None, ROW_TILE, width), lambda b, j: (b, j, 0))


def _vec_spec(l):
    return pl.BlockSpec((None, 1, D_MODEL), lambda b, j: (l, 0, 0))


def _pos_spec(j_off=0):
    return pl.BlockSpec((None, 2, ROW_TILE), lambda b, j: (b, 0, j + j_off))


def _combine_scratch():
    return [pltpu.VMEM((2, XS_ROWS, PACK_W), jnp.uint32), pltpu.SemaphoreType.DMA((2,))]


def _in_proj(l, x, moe, pmod, ln_g, ln_b, mod, w_in):
    B, L, _ = x.shape
    w_spec = pl.BlockSpec((None, D_MODEL, D_IN), lambda b, j: (l, 0, 0))
    u_shape = jax.ShapeDtypeStruct((B, L, D_IN), f32)
    if moe is None:
        u = pl.pallas_call(
            functools.partial(_in_kernel, with_ln=False),
            grid=(B, L // ROW_TILE),
            in_specs=[_row_spec(D_MODEL), _mod_spec(), w_spec],
            out_specs=_row_spec(D_IN),
            out_shape=u_shape,
            compiler_params=_params("parallel", "parallel"),
            name="in_proj0",
        )(x, mod, w_in)
        return x, u
    tab, pos, ys = moe
    return pl.pallas_call(
        functools.partial(_in_kernel, with_ln=True),
        grid=(B, L // ROW_TILE),
        in_specs=[pl.BlockSpec(memory_space=pltpu.SMEM), _row_spec(D_MODEL), _pos_spec(),
                  pl.BlockSpec(memory_space=pl.ANY), _mod_spec(), _vec_spec(l - 1), _vec_spec(l - 1),
                  _mod_spec(), w_spec],
        out_specs=[_row_spec(D_MODEL), _row_spec(D_IN)],
        out_shape=[jax.ShapeDtypeStruct(x.shape, f32), u_shape],
        scratch_shapes=_combine_scratch(),
        compiler_params=_params("arbitrary", "arbitrary"),
        name="in_proj",
    )(tab, x, pos, ys, pmod, ln_g, ln_b, mod, w_in)


def _scan_constants():
    C = SCAN_CHUNK
    p = np.arange(C)
    tri = (p[None, :] <= p[:, None]).astype(np.float32)
    owner = np.full((C, C), -1, np.int32)
    signs = []
    for li, m in enumerate(SCAN_LEVELS):
        same = (p[:, None] // (2 * m)) == (p[None, :] // (2 * m))
        owner[same & ((p[:, None] // m) % 2 == 1) & ((p[None, :] // m) % 2 == 0)] = li
        signs.append(np.where((p // m) % 2 == 1, 1.0, -1.0))
    owner[((p[:, None] // SCAN_BASE) == (p[None, :] // SCAN_BASE)) & (p[None, :] <= p[:, None])] = len(SCAN_LEVELS)
    sgn = np.broadcast_to(np.stack(signs)[:, :, None], (len(SCAN_LEVELS), C, LANES)).astype(np.float32)
    return (np.stack([tri, tri[::-1, ::-1]]), np.stack([owner, owner[::-1, ::-1]]),
            np.stack([sgn, sgn[:, ::-1]]))


def _hgrn_kernel(q_ref, ff_ref, fb_ref, v_ref, g_ref, lbf_ref, lbb_ref, ng_ref, t_ref, own_ref, sgn_ref,
                 a_ref, of_scr, ob_scr, *, n_ctx, n_all):
    C = SCAN_CHUNK
    n_lv = len(SCAN_LEVELS)


    def gates(rows, z_ref, lb, d):
        sp, sn = _sigmoid_pair(z_ref[rows, :])
        lf = jnp.log2(jnp.maximum(lb + (1.0 - lb) * sp, F_FLOOR))
        k = (1.0 - lb) * sn
        qz = q_ref[rows, :]
        q = qz * _sigmoid_pair(qz)[0]
        v = v_ref[rows, :].astype(bf16)
        hi = lf.astype(bf16)
        r1 = lf - hi.astype(f32)
        mid = r1.astype(bf16)
        lo = (r1 - mid.astype(f32)).astype(bf16)
        d3 = jnp.dot(t_ref[d], jnp.concatenate([hi, mid, lo], axis=1), preferred_element_type=f32)
        cum = d3[:, :LANES] + d3[:, LANES:2 * LANES] + d3[:, 2 * LANES:]
        return q, k, v, cum

    def intra(q, k, cum, S, d):
        last = cum[C - 1:C, :] if d == 0 else cum[0:1, :]

        def rel(block, ref_rows):
            blocks = cum.reshape(C // block, block, LANES)
            if len(ref_rows) == 1:
                ref = blocks[:, ref_rows[0]:ref_rows[0] + 1, :]
            else:
                first = lax.broadcasted_iota(jnp.int32, (1, block, 1), 1) < block // 2
                ref = jnp.where(first, blocks[:, ref_rows[0]:ref_rows[0] + 1, :],
                                blocks[:, ref_rows[1]:ref_rows[1] + 1, :])
            return (blocks - ref).reshape(C, LANES)

        o = lax.dot_general((q * jnp.exp2(cum)).astype(bf16), S.astype(bf16), _NT,
                            preferred_element_type=f32)
        own = own_ref[d]
        scores = jnp.zeros((C, C), f32)
        for li in range(n_lv + 1):
            if li < n_lv:
                m = SCAN_LEVELS[li]
                wq = jnp.exp2(rel(2 * m, (m - 1,) if d == 0 else (m,)) * sgn_ref[d, li])
                wk = wq
            else:
                nb = SCAN_BASE
                dl = rel(2 * nb, (0, nb) if d == 0 else (nb - 1, 2 * nb - 1))
                wq = jnp.exp2(dl)
                wk = jnp.exp2(-dl)
            a = lax.dot_general((q * wq).astype(bf16), (k * wk).astype(bf16), _NT,
                                preferred_element_type=f32)
            scores = jnp.where(own == li, a, scores)
        kdec = (k * jnp.exp2(last - cum)).astype(bf16)
        return o, scores.astype(bf16), kdec, jnp.exp2(last)

    def finish(o, scores, v, kdec, decay, S):
        o = o + jnp.dot(scores, v, preferred_element_type=f32)
        S = S * decay + lax.dot_general(v, kdec, _TN, preferred_element_type=f32)
        return o, S

    lbf = lbf_ref[...]
    lbb = lbb_ref[...]

    def scan_body(i, carry):
        s_f, s_b = carry
        rows_f = pl.ds(pl.multiple_of(i * C, C), C)
        c = jnp.where(i < n_ctx, n_ctx - 1 - i, n_all - 1 - (i - n_ctx))
        rows_b = pl.ds(pl.multiple_of(c * C, C), C)
        qf, kf, vf, cum_f = gates(rows_f, ff_ref, lbf, 0)
        qb, kb, vb, cum_b = gates(rows_b, fb_ref, lbb, 1)
        part_f = intra(qf, kf, cum_f, s_f, 0)
        part_b = intra(qb, kb, cum_b, s_b, 1)
        o_f, s_f = finish(part_f[0], part_f[1], vf, part_f[2], part_f[3], s_f)
        o_b, s_b = finish(part_b[0], part_b[1], vb, part_b[2], part_b[3], s_b)
        of_scr[rows_f, :] = o_f
        ob_scr[rows_b, :] = o_b
        return s_f, s_b

    zero = jnp.zeros((A_DK, A_DK), f32)
    lax.fori_loop(0, n_all, scan_body, (zero, zero), unroll=2)

    ng = ng_ref[...]

    def readout_body(i, carry):
        rows = pl.ds(pl.multiple_of(i * ROW_TILE, ROW_TILE), ROW_TILE)
        o = of_scr[rows, :] + ob_scr[rows, :]
        o = o * lax.rsqrt(jnp.mean(o * o, axis=-1, keepdims=True) + RMS_EPS) * ng
        gz = g_ref[rows, :]
        a_ref[rows, :] = (o * (gz * _sigmoid_pair(gz)[0])).astype(a_ref.dtype)
        return carry

    lax.fori_loop(0, n_all * C // ROW_TILE, readout_body, 0)


def _hgrn(u, lb_f, lb_b, norm_g, scan_consts):
    B, L, _ = u.shape
    hb = A_WIDTH // LANES

    def col(sec):
        return pl.BlockSpec((None, L, LANES), lambda b, h: (b, 0, sec * hb + h))

    def whole(arr):
        return pl.BlockSpec(arr.shape, lambda b, h: (0,) * arr.ndim)

    lb_spec = pl.BlockSpec((None, 1, LANES), lambda b, h: (h, 0, 0))
    return pl.pallas_call(
        functools.partial(_hgrn_kernel, n_ctx=CTX_LEN // SCAN_CHUNK, n_all=L // SCAN_CHUNK),
        grid=(B, A_HEADS),
        in_specs=[col(0), col(1), col(2), col(3), col(4), lb_spec, lb_spec,
                  pl.BlockSpec((1, LANES), lambda b, h: (0, 0))] + [whole(a) for a in scan_consts],
        out_specs=pl.BlockSpec((None, L, LANES), lambda b, h: (b, 0, h)),
        out_shape=jax.ShapeDtypeStruct((B, L, A_WIDTH), bf16),
        scratch_shapes=[pltpu.VMEM((L, LANES), f32), pltpu.VMEM((L, LANES), f32)],
        compiler_params=_params("parallel", "parallel"),
        name="hgrn2",
    )(u, u, u, u, u, lb_f.reshape(A_HEADS, 1, LANES), lb_b.reshape(A_HEADS, 1, LANES),
      norm_g.reshape(1, LANES), *scan_consts)


def _rope_tables(seq):
    t = np.arange(seq)
    pos_r = (t // GRID_W).astype(np.float32)
    pos_c = (t % GRID_W).astype(np.float32)
    quarter = B_HD // 4
    inv = (ROPE_BASE ** (-np.arange(quarter, dtype=np.float32) / quarter)).astype(np.float32)
    lane = np.arange(LANES)
    in_head = lane % B_HD
    pos = np.where((in_head < B_HD // 2)[None, :], pos_r[:, None], pos_c[:, None])
    ang = (pos * inv[lane % quarter][None, :]).astype(np.float32)
    sign = np.where((lane % (2 * quarter)) < quarter, -1.0, 1.0)
    return np.cos(ang).astype(np.float32), (np.sin(ang) * sign[None, :]).astype(np.float32)


def _window_mask():
    c = np.arange(GRID_W)
    cs = np.clip(c - NA_COLS // 2, 0, GRID_W - NA_COLS)
    in_win = (c[None, :] >= cs[:, None]) & (c[None, :] < cs[:, None] + NA_COLS)
    return np.tile(in_win, (1, NA_ROWS)).astype(np.float32)


def _bias_tables(rpb):
    c = np.arange(GRID_W)
    dc = np.clip(c[None, :] - c[:, None], -(NA_COLS - 1), NA_COLS - 1) + (NA_COLS - 1)
    pick = (np.arange(2 * NA_COLS - 1)[:, None, None] == dc[None]).astype(np.float32)
    cols = jnp.einsum('lhdk,kcx->lhdcx', rpb, jnp.asarray(pick), precision=lax.Precision.HIGHEST)
    t = jnp.stack([cols[:, :, o:o + NA_ROWS] for o in range(NA_ROWS)], axis=2)
    t = t.transpose(0, 1, 2, 4, 3, 5)
    return t.reshape(DEPTH, B_HEADS, NA_ROWS, GRID_W, NA_ROWS * GRID_W)


def _na_kernel(q_ref, k_ref, v_ref, cos_ref, sin_ref, bias_ref, mask_ref, o_ref,
               qr_scr, kr_scr, qp_scr, kc_scr, v_scr, *, seq, rows_grid):
    scale = B_HD ** -0.5
    hd = B_HD
    n_win = NA_ROWS * GRID_W
    lane = lax.broadcasted_iota(jnp.int32, (ROW_TILE, LANES), 1)
    first_half = (lane % (hd // 2)) < (hd // 4)

    def swap_halves(x):
        return jnp.where(first_half, pltpu.roll(x, LANES - hd // 4, axis=1), pltpu.roll(x, hd // 4, axis=1))

    def split_store(dst, rows, x):
        xb = x.astype(bf16)
        dst[0, rows, :] = xb[:, :hd]
        dst[1, rows, :] = xb[:, hd:]

    ctx_rows = pl.ds(0, CTX_LEN)
    split_store(qp_scr, ctx_rows, q_ref[ctx_rows, :] * scale)
    split_store(kc_scr, ctx_rows, k_ref[ctx_rows, :])
    split_store(v_scr, ctx_rows, v_ref[ctx_rows, :])

    def prep(i, carry):
        lrows = pl.ds(pl.multiple_of(i * ROW_TILE, ROW_TILE), ROW_TILE)
        rows = pl.ds(pl.multiple_of(CTX_LEN + i * ROW_TILE, ROW_TILE), ROW_TILE)
        q = q_ref[rows, :] * scale
        k = k_ref[rows, :]
        cos = cos_ref[lrows, :]
        sin = sin_ref[lrows, :]
        split_store(qp_scr, rows, q)
        split_store(v_scr, rows, v_ref[rows, :])
        split_store(qr_scr, lrows, q * cos + swap_halves(q) * sin)
        split_store(kr_scr, lrows, k * cos + swap_halves(k) * sin)
        return carry

    lax.fori_loop(0, seq // ROW_TILE, prep, 0)

    outs = []
    for h in range(2):
        s = lax.dot_general(qp_scr[h, ctx_rows, :], kc_scr[h], _NT, preferred_element_type=f32)
        e = jnp.exp(s - jnp.max(s, axis=-1, keepdims=True))
        o = jnp.dot(e.astype(bf16), v_scr[h, ctx_rows, :], preferred_element_type=f32)
        outs.append(o / jnp.sum(e, axis=-1, keepdims=True))
    o_ref[ctx_rows, :] = jnp.concatenate(outs, axis=1).astype(o_ref.dtype)

    def rows_body(it, carry):
        chains = []
        for g in range(NA_GROUP):
            r = it * NA_GROUP + g
            rs = jnp.clip(r - NA_ROWS // 2, 0, rows_grid - NA_ROWS)
            for h in range(2):
                chains.append(dict(
                    h=h, off=rs - r + (NA_ROWS - 1),
                    qrows=pl.ds(pl.multiple_of(r * GRID_W, GRID_W), GRID_W),
                    krows=pl.ds(pl.multiple_of(rs * GRID_W, GRID_W), n_win),
                    arows=pl.ds(pl.multiple_of(CTX_LEN + r * GRID_W, GRID_W), GRID_W),
                    vrows=pl.ds(pl.multiple_of(CTX_LEN + rs * GRID_W, GRID_W), n_win)))
        for c in chains:
            h = c["h"]
            c["s_lat"] = lax.dot_general(qr_scr[h, c["qrows"], :], kr_scr[h, c["krows"], :], _NT,
                                         preferred_element_type=f32)
            c["s_ctx"] = lax.dot_general(qp_scr[h, c["arows"], :], kc_scr[h], _NT, preferred_element_type=f32)
        for c in chains:
            s_lat = jnp.where(mask_ref[...] > 0.5, c["s_lat"] + bias_ref[c["h"], c["off"]], MASK_VALUE)
            m = jnp.maximum(jnp.max(s_lat, axis=-1, keepdims=True), jnp.max(c["s_ctx"], axis=-1, keepdims=True))
            e_lat = jnp.exp(s_lat - m)
            e_ctx = jnp.exp(c["s_ctx"] - m)
            c["den"] = jnp.sum(e_lat, axis=-1, keepdims=True) + jnp.sum(e_ctx, axis=-1, keepdims=True)
            c["e_lat"] = e_lat.astype(bf16)
            c["e_ctx"] = e_ctx.astype(bf16)
        for c in chains:
            h = c["h"]
            c["o"] = (jnp.dot(c["e_lat"], v_scr[h, c["vrows"], :], preferred_element_type=f32)
                      + jnp.dot(c["e_ctx"], v_scr[h, ctx_rows, :], preferred_element_type=f32))
        for g in range(NA_GROUP):
            pair = chains[2 * g:2 * g + 2]
            o_ref[pair[0]["arows"], :] = jnp.concatenate([c["o"] / c["den"] for c in pair],
                                                         axis=1).astype(o_ref.dtype)
        return carry

    lax.fori_loop(0, rows_grid // NA_GROUP, rows_body, 0)


def _na(l, u, cos_t, sin_t, bias_t, mask):
    B, L, _ = u.shape
    seq = L - CTX_LEN
    first = 5 * A_WIDTH // LANES
    pairs = B_WIDTH // LANES

    def col(sec):
        return pl.BlockSpec((None, L, LANES), lambda b, h: (b, 0, first + sec * pairs + h))

    n_win = NA_ROWS * GRID_W
    return pl.pallas_call(
        functools.partial(_na_kernel, seq=seq, rows_grid=seq // GRID_W),
        grid=(B, pairs),
        in_specs=[col(0), col(1), col(2),
                  pl.BlockSpec((seq, LANES), lambda b, h: (0, 0)),
                  pl.BlockSpec((seq, LANES), lambda b, h: (0, 0)),
                  pl.BlockSpec((None, 2, NA_ROWS, GRID_W, n_win), lambda b, h: (l, h, 0, 0, 0)),
                  pl.BlockSpec((GRID_W, n_win), lambda b, h: (0, 0))],
        out_specs=pl.BlockSpec((None, L, LANES), lambda b, h: (b, 0, h)),
        out_shape=jax.ShapeDtypeStruct((B, L, B_WIDTH), bf16),
        scratch_shapes=[pltpu.VMEM((2, seq, B_HD), bf16), pltpu.VMEM((2, seq, B_HD), bf16),
                        pltpu.VMEM((2, L, B_HD), bf16), pltpu.VMEM((2, CTX_LEN, B_HD), bf16),
                        pltpu.VMEM((2, L, B_HD), bf16)],
        compiler_params=_params("parallel", "parallel"),
        name="na_attn",
    )(u, u, u, cos_t, sin_t, bias_t, mask)


def _route(logits, b_router, sel_ref, gate_ref):
    aff, _ = _sigmoid_pair(logits)
    sel = aff + b_router
    s = [sel[e:e + 1, :] for e in range(N_EXPERTS)]
    one = lambda cond: jnp.where(cond, 1.0, 0.0)
    scores = []
    for g in range(N_GROUPS):
        s0, s1, s2, s3 = s[g * EXPERTS_PER_GROUP:(g + 1) * EXPERTS_PER_GROUP]
        hi01, lo01 = jnp.maximum(s0, s1), jnp.minimum(s0, s1)
        hi23, lo23 = jnp.maximum(s2, s3), jnp.minimum(s2, s3)
        scores.append(jnp.maximum(hi01, hi23) + jnp.maximum(jnp.minimum(hi01, hi23), jnp.maximum(lo01, lo23)))
    weights = []
    chosen = []
    for g in range(N_GROUPS):
        gsel = None
        for g2 in range(N_GROUPS):
            if g2 == g:
                continue
            t = one(scores[g2] < scores[g]) if g2 < g else one(scores[g2] <= scores[g])
            gsel = t if gsel is None else gsel * t
        for i in range(EXPERTS_PER_GROUP):
            e = g * EXPERTS_PER_GROUP + i
            rank = None
            for j in range(EXPERTS_PER_GROUP):
                if j == i:
                    continue
                e2 = g * EXPERTS_PER_GROUP + j
                t = one(s[e2] >= s[e]) if j < i else one(s[e2] > s[e])
                rank = t if rank is None else rank + t
            chosen.append(gsel * one(rank < 1.5))
            weights.append(chosen[-1] * aff[e:e + 1, :])
    total = weights[0]
    for w in weights[1:]:
        total = total + w
    inv = 1.0 / total
    for e in range(N_EXPERTS):
        sel_ref[e:e + 1, :] = chosen[e]
        gate_ref[e:e + 1, :] = weights[e] * inv


def _out_kernel(a_ref, b_ref, x_ref, mod_ref, wa_ref, wb_ref, g_ref, bt_ref, wr_ref, br_ref, tri_ref, low_ref,
                x1_ref, pos_ref, tab_ref, cnt_ref, xs_hbm,
                sel_scr, gate_scr, meta_v, meta_s, xs_vmem, sem, cnt_smem, nblk_smem, *, cap):
    t, _, has_next, is_first = _tile_id(0)
    slot = lax.rem(t, 2)

    @pl.when(is_first)
    def _():
        for e in range(N_EXPERTS):
            cnt_smem[e] = 0
        nblk_smem[0] = 0
        nblk_smem[1] = 0

    y = (jnp.dot(a_ref[...], wa_ref[...], preferred_element_type=f32)
         + jnp.dot(b_ref[...], wb_ref[...], preferred_element_type=f32))
    x1 = _layer_norm(ALPHA * x_ref[...] + mod_ref[2:3, :] * y, g_ref[...], bt_ref[...])
    x1_ref[...] = x1
    xm = x1 * (1.0 + mod_ref[4:5, :]) + mod_ref[3:4, :]
    logits = lax.dot_general(wr_ref[...], xm, _NT, preferred_element_type=f32,
                             precision=lax.Precision.HIGHEST)
    _route(logits, br_ref[...], sel_scr, gate_scr)

    sel = sel_scr[...]
    gate = gate_scr[...]
    sel_b = sel.astype(bf16)
    rank = jnp.dot(sel_b, tri_ref[...], preferred_element_type=f32)
    below = jnp.dot(low_ref[...], sel_b, preferred_element_type=f32)
    n = jnp.sum(sel, axis=1, keepdims=True)
    npad = jnp.broadcast_to(jnp.floor((n + (XS_BLK - 1)) * (1.0 / XS_BLK)) * XS_BLK, (N_EXPERTS, LANES))
    start = jnp.dot(low_ref[...], npad.astype(bf16), preferred_element_type=f32)
    pos = start[:, :1] + rank
    first = jnp.where(below < 0.5, sel, 0.0)
    second = sel - first
    pos0 = jnp.sum(first * pos, axis=0, keepdims=True).astype(jnp.int32)
    pos1 = jnp.sum(second * pos, axis=0, keepdims=True).astype(jnp.int32)
    g0 = jnp.sum(first * gate, axis=0, keepdims=True)
    g1 = jnp.sum(second * gate, axis=0, keepdims=True)
    pos_ref[0:1, :] = pos0
    pos_ref[1:2, :] = pos1
    meta_v[0] = start.astype(jnp.int32)
    meta_v[1] = (npad * (1.0 / XS_BLK)).astype(jnp.int32)
    pltpu.sync_copy(meta_v, meta_s)

    r = lax.broadcasted_iota(jnp.int32, (XS_ROWS, ROW_TILE), 0)
    hit0 = r == pos0
    hit1 = r == pos1
    perm = jnp.where(hit0, 1.0, jnp.where(hit1, 1.0, 0.0)).astype(bf16)
    gate_sorted = jnp.sum(jnp.where(hit0, g0, jnp.where(hit1, g1, 0.0)), axis=1, keepdims=True)
    xs = jnp.dot(perm, xm.astype(bf16), preferred_element_type=f32)

    def block_copy(src_row, dst_row, sl):
        return pltpu.make_async_copy(xs_vmem.at[sl, pl.ds(pl.multiple_of(src_row, XS_BLK), XS_BLK), :],
                                     xs_hbm.at[pl.ds(pl.multiple_of(dst_row, XS_BLK), XS_BLK), :],
                                     sem.at[sl])

    def wait_blocks(count, sl):
        def body(i, carry):
            block_copy(0, 0, sl).wait()
            return carry

        lax.fori_loop(0, count, body, 0)

    wait_blocks(nblk_smem[slot], slot)
    xs_vmem[slot, :, :PACK_W] = _pack_pairs(xs)
    xs_vmem[slot, :, PACK_W:] = lax.bitcast_convert_type(
        jnp.broadcast_to(gate_sorted, (XS_ROWS, LANES)), jnp.uint32)
    total = 0
    for e in range(N_EXPERTS):
        start_e = meta_s[0, e, 0]
        nblk_e = meta_s[1, e, 0]
        base = e * cap + cnt_smem[e]
        tab_ref[t, 3 * e] = base
        tab_ref[t, 3 * e + 1] = start_e
        tab_ref[t, 3 * e + 2] = nblk_e

        def body(i, carry):
            block_copy(start_e + i * XS_BLK, base + i * XS_BLK, slot).start()
            return carry

        lax.fori_loop(0, nblk_e, body, 0)
        cnt_smem[e] = cnt_smem[e] + nblk_e * XS_BLK
        total = total + nblk_e
    nblk_smem[slot] = total

    @pl.when(jnp.logical_not(has_next))
    def _():
        wait_blocks(nblk_smem[slot], slot)
        wait_blocks(nblk_smem[1 - slot], 1 - slot)
        for e in range(N_EXPERTS):
            cnt_ref[0, e] = cnt_smem[e]


def _dispatch_constants():
    t = np.arange(ROW_TILE)
    e = np.arange(N_EXPERTS)
    return ((t[:, None] < t[None, :]).astype(np.float32), (e[None, :] < e[:, None]).astype(np.float32))


def _out_proj(l, a, bmix, x, mod, w_out, ln_g, ln_b, w_router_t, b_router, tri, low):
    B, L, _ = x.shape
    half = A_WIDTH
    n_tiles = B * (L // ROW_TILE)
    cap = _expert_capacity(n_tiles)
    smem = pl.BlockSpec(memory_space=pltpu.SMEM)
    return pl.pallas_call(
        functools.partial(_out_kernel, cap=cap),
        grid=(B, L // ROW_TILE),
        in_specs=[_row_spec(A_WIDTH), _row_spec(B_WIDTH), _row_spec(D_MODEL), _mod_spec(),
                  pl.BlockSpec((None, half, D_MODEL), lambda b, j: (l, 0, 0)),
                  pl.BlockSpec((None, half, D_MODEL), lambda b, j: (l, 1, 0)),
                  _vec_spec(l), _vec_spec(l),
                  pl.BlockSpec((N_EXPERTS, D_MODEL), lambda b, j: (0, 0)),
                  pl.BlockSpec((N_EXPERTS, 1), lambda b, j: (0, 0)),
                  pl.BlockSpec((ROW_TILE, ROW_TILE), lambda b, j: (0, 0)),
                  pl.BlockSpec((N_EXPERTS, N_EXPERTS), lambda b, j: (0, 0))],
        out_specs=[_row_spec(D_MODEL), _pos_spec(), smem, smem, pl.BlockSpec(memory_space=pl.ANY)],
        out_shape=[jax.ShapeDtypeStruct(x.shape, f32), jax.ShapeDtypeStruct((B, 2, L), jnp.int32),
                   jax.ShapeDtypeStruct((n_tiles, 3 * N_EXPERTS), jnp.int32),
                   jax.ShapeDtypeStruct((1, N_EXPERTS), jnp.int32),
                   jax.ShapeDtypeStruct((N_EXPERTS * cap, XS_W), jnp.uint32)],
        scratch_shapes=[pltpu.VMEM((N_EXPERTS, ROW_TILE), f32), pltpu.VMEM((N_EXPERTS, ROW_TILE), f32),
                        pltpu.VMEM((2, N_EXPERTS, LANES), jnp.int32), pltpu.SMEM((2, N_EXPERTS, LANES), jnp.int32),
                        pltpu.VMEM((2, XS_ROWS, XS_W), jnp.uint32), pltpu.SemaphoreType.DMA((2,)),
                        pltpu.SMEM((N_EXPERTS,), jnp.int32), pltpu.SMEM((2,), jnp.int32)],
        compiler_params=_params("arbitrary", "arbitrary"),
        name="out_proj",
    )(a, bmix, x, mod, w_out, w_out, ln_g, ln_b, w_router_t, b_router, tri, low)


def _expert_capacity(n_tiles):
    rows = n_tiles * (ROW_TILE + XS_BLK)
    return -(-rows // EXP_TILE) * EXP_TILE


def _moe_kernel(te_ref, tb_ref, tv_ref, xs_ref, wg_ref, wu_ref, wd_ref, ys_ref):
    i = pl.program_id(0)
    valid = tv_ref[i]

    @pl.when(valid > 0)
    def _():
        w32 = xs_ref[...]
        ok = lax.broadcasted_iota(jnp.int32, (EXP_TILE, 1), 0) < valid
        x = _unpack_pairs(jnp.where(ok, w32[:, :PACK_W], jnp.uint32(0)))
        gate = jnp.where(ok, lax.bitcast_convert_type(w32[:, PACK_W:PACK_W + 1], f32), 0.0)
        hg = jnp.dot(x, wg_ref[...], preferred_element_type=f32)
        hu = jnp.dot(x, wu_ref[...], preferred_element_type=f32)
        hid = hg * _sigmoid_pair(hg)[0] * hu * gate
        y = jnp.dot(hid.astype(bf16), wd_ref[...], preferred_element_type=f32)
        ys_ref[...] = _pack_pairs(y.astype(bf16).astype(f32))


def _expert_tiles(cnt, cap, n_steps):
    cnt = cnt.reshape(N_EXPERTS)
    tiles = (cnt + (EXP_TILE - 1)) // EXP_TILE
    ends = jnp.cumsum(tiles)
    n_valid = ends[-1]
    i = jnp.minimum(jnp.arange(n_steps, dtype=jnp.int32), n_valid - 1)
    e = jnp.sum((i[:, None] >= ends[None, :]).astype(jnp.int32), axis=1)
    local = i - (ends - tiles)[e]
    rows = jnp.clip(cnt[e] - local * EXP_TILE, 0, EXP_TILE)
    rows = jnp.where(jnp.arange(n_steps) < n_valid, rows, 0)
    return e.astype(jnp.int32), (e * (cap // EXP_TILE) + local).astype(jnp.int32), rows.astype(jnp.int32)


def _moe(l, xs, cnt, n_tiles, w_gate, w_up, w_down):
    cap = _expert_capacity(n_tiles)
    n_steps = n_tiles * XS_ROWS // EXP_TILE + N_EXPERTS
    te, tb, tv = _expert_tiles(cnt, cap, n_steps)
    grid_spec = pltpu.PrefetchScalarGridSpec(
        num_scalar_prefetch=3,
        grid=(n_steps,),
        in_specs=[pl.BlockSpec((EXP_TILE, XS_W), lambda i, te, tb, tv: (tb[i], 0)),
                  pl.BlockSpec((None, None, D_MODEL, D_EXPERT), lambda i, te, tb, tv: (l, te[i], 0, 0)),
                  pl.BlockSpec((None, None, D_MODEL, D_EXPERT), lambda i, te, tb, tv: (l, te[i], 0, 0)),
                  pl.BlockSpec((None, None, D_EXPERT, D_MODEL), lambda i, te, tb, tv: (l, te[i], 0, 0))],
        out_specs=pl.BlockSpec((EXP_TILE, PACK_W), lambda i, te, tb, tv: (tb[i], 0)),
    )
    return pl.pallas_call(
        _moe_kernel,
        grid_spec=grid_spec,
        out_shape=jax.ShapeDtypeStruct((N_EXPERTS * cap, PACK_W), jnp.uint32),
        compiler_params=_params("arbitrary"),
        name="moe",
    )(te, tb, tv, xs, w_gate, w_up, w_down)


def _final_kernel(tab_ref, x_ref, pos_ref, ys_hbm, mod_ref, g_ref, b_ref, o_ref, y_vmem, sem, *, j_off):
    f = _combine_experts(tab_ref, pos_ref, ys_hbm, y_vmem, sem, j_off)
    o_ref[...] = _layer_norm(ALPHA * x_ref[...] + mod_ref[5:6, :] * f, g_ref[...], b_ref[...])


def _final_ln(x, moe, mod, ln_g, ln_b):
    B, L, _ = x.shape
    seq = L - CTX_LEN
    skip = CTX_LEN // ROW_TILE
    tab, pos, ys = moe
    lat = pl.BlockSpec((None, ROW_TILE, D_MODEL), lambda b, j: (b, j + skip, 0))
    return pl.pallas_call(
        functools.partial(_final_kernel, j_off=skip),
        grid=(B, seq // ROW_TILE),
        in_specs=[pl.BlockSpec(memory_space=pltpu.SMEM), lat, _pos_spec(skip), pl.BlockSpec(memory_space=pl.ANY),
                  pl.BlockSpec((None, None, 6, D_MODEL), lambda b, j: (b, 1, 0, 0)),
                  _vec_spec(DEPTH - 1), _vec_spec(DEPTH - 1)],
        out_specs=_row_spec(D_MODEL),
        out_shape=jax.ShapeDtypeStruct((B, seq, D_MODEL), f32),
        scratch_shapes=_combine_scratch(),
        compiler_params=_params("arbitrary", "arbitrary"),
        name="final_ln",
    )(tab, x, pos, ys, mod, ln_g, ln_b)


def kernel(x, c, ctx, c_ctx, w_ada, b_ada, w_in, lb_logits, a_norm_g, rpb, w_out, ln1_g, ln1_b,
           w_router, b_router, w_gate, w_up, w_down, ln2_g, ln2_b):
    B, seq, D = x.shape
    L = CTX_LEN + seq
    xs = jnp.concatenate([ctx, x], axis=1)

    cc = jnp.concatenate([c, c_ctx[None, :], jnp.zeros((16 - B - 1, D), f32)], axis=0)
    mods = _ada_mods(cc, w_ada, b_ada)
    lat_mod = mods[:, :B].reshape(DEPTH, B, 1, 6, D)
    ctx_mod = jnp.broadcast_to(mods[:, B].reshape(DEPTH, 1, 1, 6, D), (DEPTH, B, 1, 6, D))
    mod_all = jnp.concatenate([ctx_mod, lat_mod], axis=2)

    sm = jax.nn.softmax(lb_logits.astype(f32), axis=1)
    lower = jnp.cumsum(sm, axis=1) - sm[:, :1]

    tri_np, own_np, sgn_np = _scan_constants()
    scan_consts = (jnp.asarray(tri_np, bf16), jnp.asarray(own_np), jnp.asarray(sgn_np))
    cos_np, sin_np = _rope_tables(seq)
    cos_t, sin_t = jnp.asarray(cos_np), jnp.asarray(sin_np)
    mask = jnp.asarray(_window_mask())
    bias_t = _bias_tables(rpb.astype(f32))

    w_in_b = w_in.astype(bf16)
    w_out_b = w_out.astype(bf16)
    w_gate_b, w_up_b, w_down_b = w_gate.astype(bf16), w_up.astype(bf16), w_down.astype(bf16)
    w_router_t = w_router.T
    b_router_c = b_router.reshape(N_EXPERTS, 1)
    ln1_g3, ln1_b3 = ln1_g.reshape(DEPTH, 1, D), ln1_b.reshape(DEPTH, 1, D)
    ln2_g3, ln2_b3 = ln2_g.reshape(DEPTH, 1, D), ln2_b.reshape(DEPTH, 1, D)

    tri_np, low_np = _dispatch_constants()
    tri, low = jnp.asarray(tri_np, bf16), jnp.asarray(low_np, bf16)
    n_tiles = B * (L // ROW_TILE)

    moe = None
    for l in range(DEPTH):
        pmod = mod_all[l - 1] if l > 0 else None
        xs, u = _in_proj(l, xs, moe, pmod, ln2_g3, ln2_b3, mod_all[l], w_in_b)
        a = _hgrn(u, lower[0, l], lower[1, l], a_norm_g[l], scan_consts)
        bmix = _na(l, u, cos_t, sin_t, bias_t, mask)
        xs, pos, tab, cnt, x_sorted = _out_proj(l, a, bmix, xs, mod_all[l], w_out_b, ln1_g3, ln1_b3,
                                                w_router_t, b_router_c, tri, low)
        y_sorted = _moe(l, x_sorted, cnt, n_tiles, w_gate_b, w_up_b, w_down_b)
        moe = (tab, pos, y_sorted)
    return _final_ln(xs, moe, mod_all[DEPTH - 1], ln2_g3, ln2_b3)
```

```python
import functools

import numpy as np
import jax
import jax.numpy as jnp
from jax import lax
from jax.experimental import pallas as pl
from jax.experimental.pallas import tpu as pltpu

D_MODEL = 1024
DEPTH = 4
GRID_W = 64
CTX_LEN = 256
A_HEADS = 4
A_DK = 128
A_WIDTH = A_HEADS * A_DK
B_HEADS = 8
B_HD = 64
B_WIDTH = B_HEADS * B_HD
D_IN = 5 * A_WIDTH + 3 * B_WIDTH
NA_ROWS = 8
NA_COLS = 16
ROPE_BASE = 10000.0
N_EXPERTS = 16
N_GROUPS = 4
EXPERTS_PER_GROUP = N_EXPERTS // N_GROUPS
D_EXPERT = 512
ALPHA = (2 * DEPTH) ** 0.25
LN_EPS = 1e-5
RMS_EPS = 1e-6
F_FLOOR = 1e-6
MASK_VALUE = -1e30
LOG2E = 1.4426950408889634

LANES = 128
ROW_TILE = 256
XS_BLK = 8
XS_ROWS = 2 * ROW_TILE + N_EXPERTS * XS_BLK
PACK_W = D_MODEL // 2
XS_W = PACK_W + LANES
EXP_TILE = 512
SCAN_CHUNK = 128
SCAN_LEVELS = (64, 32, 16, 8, 4)
SCAN_BASE = 4
NA_GROUP = 4
VMEM_LIMIT = 56 * 1024 * 1024

f32 = jnp.float32
bf16 = jnp.bfloat16

_NT = (((1,), (1,)), ((), ()))
_TN = (((0,), (0,)), ((), ()))


def _params(*sem):
    return pltpu.CompilerParams(dimension_semantics=sem, vmem_limit_bytes=VMEM_LIMIT)


def _sigmoid(z):
    return 1.0 / (1.0 + jnp.exp(-z))


def _layer_norm(h, g, b):
    mu = jnp.mean(h, axis=-1, keepdims=True)
    d = h - mu
    var = jnp.mean(d * d, axis=-1, keepdims=True)
    return d * lax.rsqrt(var + LN_EPS) * g + b


def _ada_kernel(c_ref, w_ref, b_ref, o_ref):
    c = c_ref[...]
    o_ref[...] = jnp.dot(c * _sigmoid(c), w_ref[...], preferred_element_type=f32,
                         precision=lax.Precision.HIGHEST) + b_ref[...]


def _ada_mods(cc, w_ada, b_ada):
    n_col = 4
    tn = 6 * D_MODEL // n_col
    return pl.pallas_call(
        _ada_kernel,
        grid=(DEPTH, n_col),
        in_specs=[
            pl.BlockSpec((16, D_MODEL), lambda l, j: (0, 0)),
            pl.BlockSpec((None, D_MODEL, tn), lambda l, j: (l, 0, j)),
            pl.BlockSpec((None, 1, tn), lambda l, j: (l, 0, j)),
        ],
        out_specs=pl.BlockSpec((None, 16, tn), lambda l, j: (l, 0, j)),
        out_shape=jax.ShapeDtypeStruct((DEPTH, 16, 6 * D_MODEL), f32),
        compiler_params=_params("parallel", "parallel"),
        name="ada_mods",
    )(cc, w_ada, b_ada.reshape(DEPTH, 1, 6 * D_MODEL))


def _pack_pairs(x):
    bits = lax.bitcast_convert_type(x, jnp.uint32)
    half = x.shape[1] // 2
    return (bits[:, :half] >> 16) | (bits[:, half:] & jnp.uint32(0xFFFF0000))


def _unpack_pairs(w):
    lo = lax.bitcast_convert_type(w << 16, f32)
    hi = lax.bitcast_convert_type(w & jnp.uint32(0xFFFF0000), f32)
    return jnp.concatenate([lo, hi], axis=1).astype(bf16)


def _tile_id(j_off):
    b, j = pl.program_id(0), pl.program_id(1)
    nj = pl.num_programs(1)
    tiles_per_batch = nj + j_off
    t = b * tiles_per_batch + j + j_off
    last_j = j == nj - 1
    t_next = jnp.where(last_j, (b + 1) * tiles_per_batch + j_off, t + 1)
    has_next = jnp.logical_not(jnp.logical_and(last_j, b == pl.num_programs(0) - 1))
    return t, t_next, has_next, jnp.logical_and(b == 0, j == 0)


def _combine_experts(tab_ref, pos_ref, ys_hbm, y_vmem, sem, j_off):
    t, t_next, has_next, is_first = _tile_id(j_off)
    step = pl.program_id(0) * pl.num_programs(1) + pl.program_id(1)
    slot = lax.rem(step, 2)

    def block_copy(src_row, dst_row, sl):
        return pltpu.make_async_copy(ys_hbm.at[pl.ds(pl.multiple_of(src_row, XS_BLK), XS_BLK), :],
                                     y_vmem.at[sl, pl.ds(pl.multiple_of(dst_row, XS_BLK), XS_BLK), :],
                                     sem.at[sl])

    def fetch(tile, sl):
        y_vmem[sl] = jnp.zeros((XS_ROWS, PACK_W), jnp.uint32)
        for e in range(N_EXPERTS):
            base, start, nblk = tab_ref[tile, 3 * e], tab_ref[tile, 3 * e + 1], tab_ref[tile, 3 * e + 2]

            def body(i, carry):
                block_copy(base + i * XS_BLK, start + i * XS_BLK, sl).start()
                return carry

            lax.fori_loop(0, nblk, body, 0)

    def wait_all(tile, sl):
        total = tab_ref[tile, 2]
        for e in range(1, N_EXPERTS):
            total = total + tab_ref[tile, 3 * e + 2]

        def body(i, carry):
            block_copy(0, 0, sl).wait()
            return carry

        lax.fori_loop(0, total, body, 0)

    @pl.when(is_first)
    def _():
        fetch(t, slot)

    @pl.when(has_next)
    def _():
        fetch(t_next, 1 - slot)

    wait_all(t, slot)
    y = _unpack_pairs(y_vmem[slot])
    r = lax.broadcasted_iota(jnp.int32, (XS_ROWS, ROW_TILE), 0)
    perm = jnp.where(r == pos_ref[0:1, :], 1.0, jnp.where(r == pos_ref[1:2, :], 1.0, 0.0)).astype(bf16)
    return lax.dot_general(perm, y, _TN, preferred_element_type=f32)


def _in_kernel(*refs, with_ln):
    if with_ln:
        (tab_ref, x_ref, pos_ref, ys_hbm, pmod_ref, g_ref, b_ref, mod_ref, w_ref,
         xo_ref, u_ref, y_vmem, sem) = refs
        f = _combine_experts(tab_ref, pos_ref, ys_hbm, y_vmem, sem, 0)
        h = ALPHA * x_ref[...] + pmod_ref[5:6, :] * f
        x = _layer_norm(h, g_ref[...], b_ref[...])
        xo_ref[...] = x
    else:
        x_ref, mod_ref, w_ref, u_ref = refs
        x = x_ref[...]
    xm = x * (1.0 + mod_ref[1:2, :]) + mod_ref[0:1, :]
    u_ref[...] = jnp.dot(xm.astype(bf16), w_ref[...], preferred_element_type=f32)


def _mod_spec():
    return pl.BlockSpec((None, None, 6, D_MODEL), lambda b, j: (b, jnp.minimum(j, 1), 0, 0))


def _row_spec(width):
    return pl.BlockSpec((None, ROW_TILE, width), lambda b, j: (b, j, 0))


def _vec_spec(l):
    return pl.BlockSpec((None, 1, D_MODEL), lambda b, j: (l, 0, 0))


def _pos_spec(j_off=0):
    return pl.BlockSpec((None, 2, ROW_TILE), lambda b, j: (b, 0, j + j_off))


def _combine_scratch():
    return [pltpu.VMEM((2, XS_ROWS, PACK_W), jnp.uint32), pltpu.SemaphoreType.DMA((2,))]


def _in_proj(l, x, moe, pmod, ln_g, ln_b, mod, w_in):
    B, L, _ = x.shape
    w_spec = pl.BlockSpec((None, D_MODEL, D_IN), lambda b, j: (l, 0, 0))
    u_shape = jax.ShapeDtypeStruct((B, L, D_IN), f32)
    if moe is None:
        u = pl.pallas_call(
            functools.partial(_in_kernel, with_ln=False),
            grid=(B, L // ROW_TILE),
            in_specs=[_row_spec(D_MODEL), _mod_spec(), w_spec],
            out_specs=_row_spec(D_IN),
            out_shape=u_shape,
            compiler_params=_params("parallel", "parallel"),
            name="in_proj0",
        )(x, mod, w_in)
        return x, u
    tab, pos, ys = moe
    return pl.pallas_call(
        functools.partial(_in_kernel, with_ln=True),
        grid=(B, L // ROW_TILE),
        in_specs=[pl.BlockSpec(memory_space=pltpu.SMEM), _row_spec(D_MODEL), _pos_spec(),
                  pl.BlockSpec(memory_space=pl.ANY), _mod_spec(), _vec_spec(l - 1), _vec_spec(l - 1),
                  _mod_spec(), w_spec],
        out_specs=[_row_spec(D_MODEL), _row_spec(D_IN)],
        out_shape=[jax.ShapeDtypeStruct(x.shape, f32), u_shape],
        scratch_shapes=_combine_scratch(),
        compiler_params=_params("arbitrary", "arbitrary"),
        name="in_proj",
    )(tab, x, pos, ys, pmod, ln_g, ln_b, mod, w_in)


def _scan_constants():
    C = SCAN_CHUNK
    p = np.arange(C)
    tri = (p[None, :] <= p[:, None]).astype(np.float32)
    owner = np.full((C, C), -1, np.int32)
    signs = []
    for li, m in enumerate(SCAN_LEVELS):
        same = (p[:, None] // (2 * m)) == (p[None, :] // (2 * m))
        owner[same & ((p[:, None] // m) % 2 == 1) & ((p[None, :] // m) % 2 == 0)] = li
        signs.append(np.where((p // m) % 2 == 1, 1.0, -1.0))
    owner[((p[:, None] // SCAN_BASE) == (p[None, :] // SCAN_BASE)) & (p[None, :] <= p[:, None])] = len(SCAN_LEVELS)
    sgn = np.broadcast_to(np.stack(signs)[:, :, None], (len(SCAN_LEVELS), C, LANES)).astype(np.float32)
    return (np.stack([tri, tri[::-1, ::-1]]), np.stack([owner, owner[::-1, ::-1]]),
            np.stack([sgn, sgn[:, ::-1]]))


def _hgrn_kernel(q_ref, ff_ref, fb_ref, v_ref, g_ref, lbf_ref, lbb_ref, ng_ref, t_ref, own_ref, sgn_ref,
                 a_ref, of_scr, ob_scr, *, n_ctx, n_all):
    C = SCAN_CHUNK
    n_lv = len(SCAN_LEVELS)


    def gates(rows, z_ref, lb, d):
        z = z_ref[rows, :]
        lf = jnp.log2(jnp.maximum(lb + (1.0 - lb) * _sigmoid(z), F_FLOOR))
        k = (1.0 - lb) * _sigmoid(-z)
        qz = q_ref[rows, :]
        q = qz * _sigmoid(qz)
        v = v_ref[rows, :].astype(bf16)
        hi = lf.astype(bf16)
        r1 = lf - hi.astype(f32)
        mid = r1.astype(bf16)
        lo = (r1 - mid.astype(f32)).astype(bf16)
        d3 = jnp.dot(t_ref[d], jnp.concatenate([hi, mid, lo], axis=1), preferred_element_type=f32)
        cum = d3[:, :LANES] + d3[:, LANES:2 * LANES] + d3[:, 2 * LANES:]
        return q, k, v, cum

    def intra(q, k, cum, S, d):
        last = cum[C - 1:C, :] if d == 0 else cum[0:1, :]

        def rel(block, ref_rows):
            blocks = cum.reshape(C // block, block, LANES)
            if len(ref_rows) == 1:
                ref = blocks[:, ref_rows[0]:ref_rows[0] + 1, :]
            else:
                first = lax.broadcasted_iota(jnp.int32, (1, block, 1), 1) < block // 2
                ref = jnp.where(first, blocks[:, ref_rows[0]:ref_rows[0] + 1, :],
                                blocks[:, ref_rows[1]:ref_rows[1] + 1, :])
            return (blocks - ref).reshape(C, LANES)

        o = lax.dot_general((q * jnp.exp2(cum)).astype(bf16), S.astype(bf16), _NT,
                            preferred_element_type=f32)
        own = own_ref[d]
        scores = jnp.zeros((C, C), f32)
        for li in range(n_lv + 1):
            if li < n_lv:
                m = SCAN_LEVELS[li]
                wq = jnp.exp2(rel(2 * m, (m - 1,) if d == 0 else (m,)) * sgn_ref[d, li])
                wk = wq
            else:
                nb = SCAN_BASE
                dl = rel(2 * nb, (0, nb) if d == 0 else (nb - 1, 2 * nb - 1))
                wq = jnp.exp2(dl)
                wk = jnp.exp2(-dl)
            a = lax.dot_general((q * wq).astype(bf16), (k * wk).astype(bf16), _NT,
                                preferred_element_type=f32)
            scores = jnp.where(own == li, a, scores)
        kdec = (k * jnp.exp2(last - cum)).astype(bf16)
        return o, scores.astype(bf16), kdec, jnp.exp2(last)

    def finish(o, scores, v, kdec, decay, S):
        o = o + jnp.dot(scores, v, preferred_element_type=f32)
        S = S * decay + lax.dot_general(v, kdec, _TN, preferred_element_type=f32)
        return o, S

    lbf = lbf_ref[...]
    lbb = lbb_ref[...]

    def scan_body(i, carry):
        s_f, s_b = carry
        rows_f = pl.ds(pl.multiple_of(i * C, C), C)
        c = jnp.where(i < n_ctx, n_ctx - 1 - i, n_all - 1 - (i - n_ctx))
        rows_b = pl.ds(pl.multiple_of(c * C, C), C)
        qf, kf, vf, cum_f = gates(rows_f, ff_ref, lbf, 0)
        qb, kb, vb, cum_b = gates(rows_b, fb_ref, lbb, 1)
        part_f = intra(qf, kf, cum_f, s_f, 0)
        part_b = intra(qb, kb, cum_b, s_b, 1)
        o_f, s_f = finish(part_f[0], part_f[1], vf, part_f[2], part_f[3], s_f)
        o_b, s_b = finish(part_b[0], part_b[1], vb, part_b[2], part_b[3], s_b)
        of_scr[rows_f, :] = o_f
        ob_scr[rows_b, :] = o_b
        return s_f, s_b

    zero = jnp.zeros((A_DK, A_DK), f32)
    lax.fori_loop(0, n_all, scan_body, (zero, zero), unroll=2)

    ng = ng_ref[...]

    def readout_body(i, carry):
        rows = pl.ds(pl.multiple_of(i * ROW_TILE, ROW_TILE), ROW_TILE)
        o = of_scr[rows, :] + ob_scr[rows, :]
        o = o * lax.rsqrt(jnp.mean(o * o, axis=-1, keepdims=True) + RMS_EPS) * ng
        gz = g_ref[rows, :]
        a_ref[rows, :] = (o * (gz * _sigmoid(gz))).astype(a_ref.dtype)
        return carry

    lax.fori_loop(0, n_all * C // ROW_TILE, readout_body, 0)


def _hgrn(u, lb_f, lb_b, norm_g, scan_consts):
    B, L, _ = u.shape
    hb = A_WIDTH // LANES

    def col(sec):
        return pl.BlockSpec((None, L, LANES), lambda b, h: (b, 0, sec * hb + h))

    def whole(arr):
        return pl.BlockSpec(arr.shape, lambda b, h: (0,) * arr.ndim)

    lb_spec = pl.BlockSpec((None, 1, LANES), lambda b, h: (h, 0, 0))
    return pl.pallas_call(
        functools.partial(_hgrn_kernel, n_ctx=CTX_LEN // SCAN_CHUNK, n_all=L // SCAN_CHUNK),
        grid=(B, A_HEADS),
        in_specs=[col(0), col(1), col(2), col(3), col(4), lb_spec, lb_spec,
                  pl.BlockSpec((1, LANES), lambda b, h: (0, 0))] + [whole(a) for a in scan_consts],
        out_specs=pl.BlockSpec((None, L, LANES), lambda b, h: (b, 0, h)),
        out_shape=jax.ShapeDtypeStruct((B, L, A_WIDTH), bf16),
        scratch_shapes=[pltpu.VMEM((L, LANES), f32), pltpu.VMEM((L, LANES), f32)],
        compiler_params=_params("parallel", "parallel"),
        name="hgrn2",
    )(u, u, u, u, u, lb_f.reshape(A_HEADS, 1, LANES), lb_b.reshape(A_HEADS, 1, LANES),
      norm_g.reshape(1, LANES), *scan_consts)


def _rope_tables(seq):
    t = np.arange(seq)
    pos_r = (t // GRID_W).astype(np.float32)
    pos_c = (t % GRID_W).astype(np.float32)
    quarter = B_HD // 4
    inv = (ROPE_BASE ** (-np.arange(quarter, dtype=np.float32) / quarter)).astype(np.float32)
    lane = np.arange(LANES)
    in_head = lane % B_HD
    pos = np.where((in_head < B_HD // 2)[None, :], pos_r[:, None], pos_c[:, None])
    ang = (pos * inv[lane % quarter][None, :]).astype(np.float32)
    sign = np.where((lane % (2 * quarter)) < quarter, -1.0, 1.0)
    return np.cos(ang).astype(np.float32), (np.sin(ang) * sign[None, :]).astype(np.float32)


def _window_mask():
    c = np.arange(GRID_W)
    cs = np.clip(c - NA_COLS // 2, 0, GRID_W - NA_COLS)
    in_win = (c[None, :] >= cs[:, None]) & (c[None, :] < cs[:, None] + NA_COLS)
    return np.tile(in_win, (1, NA_ROWS)).astype(np.float32)


def _bias_tables(rpb):
    c = np.arange(GRID_W)
    dc = np.clip(c[None, :] - c[:, None], -(NA_COLS - 1), NA_COLS - 1) + (NA_COLS - 1)
    pick = (np.arange(2 * NA_COLS - 1)[:, None, None] == dc[None]).astype(np.float32)
    cols = jnp.einsum('lhdk,kcx->lhdcx', rpb, jnp.asarray(pick), precision=lax.Precision.HIGHEST)
    t = jnp.stack([cols[:, :, o:o + NA_ROWS] for o in range(NA_ROWS)], axis=2)
    t = t.transpose(0, 1, 2, 4, 3, 5)
    return t.reshape(DEPTH, B_HEADS, NA_ROWS, GRID_W, NA_ROWS * GRID_W)


def _na_kernel(q_ref, k_ref, v_ref, cos_ref, sin_ref, bias_ref, mask_ref, o_ref,
               qr_scr, kr_scr, qp_scr, kc_scr, v_scr, *, seq, rows_grid):
    hd = B_HD
    scale = hd ** -0.5 * LOG2E
    n_win = NA_ROWS * GRID_W
    lane = lax.broadcasted_iota(jnp.int32, (ROW_TILE, LANES), 1)
    first_half = (lane % (hd // 2)) < (hd // 4)
    head0 = lane < hd

    def swap_halves(x):
        return jnp.where(first_half, pltpu.roll(x, LANES - hd // 4, axis=1), pltpu.roll(x, hd // 4, axis=1))

    def store_per_head(dst, rows, x):
        dst[0, rows, :] = jnp.where(head0, x, 0.0).astype(bf16)
        dst[1, rows, :] = jnp.where(head0, 0.0, x).astype(bf16)

    ctx_rows = pl.ds(0, CTX_LEN)
    store_per_head(qp_scr, ctx_rows, q_ref[ctx_rows, :] * scale)
    kc_scr[...] = k_ref[ctx_rows, :].astype(bf16)
    v_scr[ctx_rows, :] = v_ref[ctx_rows, :].astype(bf16)

    def prep(i, carry):
        lrows = pl.ds(pl.multiple_of(i * ROW_TILE, ROW_TILE), ROW_TILE)
        rows = pl.ds(pl.multiple_of(CTX_LEN + i * ROW_TILE, ROW_TILE), ROW_TILE)
        q = q_ref[rows, :] * scale
        k = k_ref[rows, :]
        cos = cos_ref[lrows, :]
        sin = sin_ref[lrows, :]
        store_per_head(qp_scr, rows, q)
        v_scr[rows, :] = v_ref[rows, :].astype(bf16)
        store_per_head(qr_scr, lrows, q * cos + swap_halves(q) * sin)
        kr_scr[lrows, :] = (k * cos + swap_halves(k) * sin).astype(bf16)
        return carry

    lax.fori_loop(0, seq // ROW_TILE, prep, 0)

    def pick_heads(o0, o1):
        keep0 = lax.broadcasted_iota(jnp.int32, o0.shape, 1) < hd
        return jnp.where(keep0, o0, o1)

    outs = []
    for h in range(2):
        s = lax.dot_general(qp_scr[h, ctx_rows, :], kc_scr[...], _NT, preferred_element_type=f32)
        e = jnp.exp2(s - jnp.max(s, axis=-1, keepdims=True))
        o = jnp.dot(e.astype(bf16), v_scr[ctx_rows, :], preferred_element_type=f32)
        outs.append(o / jnp.sum(e, axis=-1, keepdims=True))
    o_ref[ctx_rows, :] = pick_heads(*outs).astype(o_ref.dtype)

    def rows_body(it, carry):
        chains = []
        for g in range(NA_GROUP):
            r = it * NA_GROUP + g
            rs = jnp.clip(r - NA_ROWS // 2, 0, rows_grid - NA_ROWS)
            for h in range(2):
                chains.append(dict(
                    h=h, off=rs - r + (NA_ROWS - 1),
                    qrows=pl.ds(pl.multiple_of(r * GRID_W, GRID_W), GRID_W),
                    krows=pl.ds(pl.multiple_of(rs * GRID_W, GRID_W), n_win),
                    arows=pl.ds(pl.multiple_of(CTX_LEN + r * GRID_W, GRID_W), GRID_W),
                    vrows=pl.ds(pl.multiple_of(CTX_LEN + rs * GRID_W, GRID_W), n_win)))
        for c in chains:
            h = c["h"]
            c["s_lat"] = lax.dot_general(qr_scr[h, c["qrows"], :], kr_scr[c["krows"], :], _NT,
                                         preferred_element_type=f32)
            c["s_ctx"] = lax.dot_general(qp_scr[h, c["arows"], :], kc_scr[...], _NT, preferred_element_type=f32)
        for c in chains:
            s_lat = jnp.where(mask_ref[...] > 0.5, c["s_lat"] + bias_ref[c["h"], c["off"]], MASK_VALUE)
            m = jnp.maximum(jnp.max(s_lat, axis=-1, keepdims=True), jnp.max(c["s_ctx"], axis=-1, keepdims=True))
            e_lat = jnp.exp2(s_lat - m)
            e_ctx = jnp.exp2(c["s_ctx"] - m)
            c["den"] = jnp.sum(e_lat, axis=-1, keepdims=True) + jnp.sum(e_ctx, axis=-1, keepdims=True)
            c["e_lat"] = e_lat.astype(bf16)
            c["e_ctx"] = e_ctx.astype(bf16)
        for c in chains:
            c["o"] = (jnp.dot(c["e_lat"], v_scr[c["vrows"], :], preferred_element_type=f32)
                      + jnp.dot(c["e_ctx"], v_scr[ctx_rows, :], preferred_element_type=f32))
        for g in range(NA_GROUP):
            c0, c1 = chains[2 * g:2 * g + 2]
            o_ref[c0["arows"], :] = pick_heads(c0["o"] / c0["den"], c1["o"] / c1["den"]).astype(o_ref.dtype)
        return carry

    lax.fori_loop(0, rows_grid // NA_GROUP, rows_body, 0)


def _na(l, u, cos_t, sin_t, bias_t, mask):
    B, L, _ = u.shape
    seq = L - CTX_LEN
    first = 5 * A_WIDTH // LANES
    pairs = B_WIDTH // LANES

    def col(sec):
        return pl.BlockSpec((None, L, LANES), lambda b, h: (b, 0, first + sec * pairs + h))

    n_win = NA_ROWS * GRID_W
    return pl.pallas_call(
        functools.partial(_na_kernel, seq=seq, rows_grid=seq // GRID_W),
        grid=(B, pairs),
        in_specs=[col(0), col(1), col(2),
                  pl.BlockSpec((seq, LANES), lambda b, h: (0, 0)),
                  pl.BlockSpec((seq, LANES), lambda b, h: (0, 0)),
                  pl.BlockSpec((None, 2, NA_ROWS, GRID_W, n_win), lambda b, h: (l, h, 0, 0, 0)),
                  pl.BlockSpec((GRID_W, n_win), lambda b, h: (0, 0))],
        out_specs=pl.BlockSpec((None, L, LANES), lambda b, h: (b, 0, h)),
        out_shape=jax.ShapeDtypeStruct((B, L, B_WIDTH), bf16),
        scratch_shapes=[pltpu.VMEM((2, seq, LANES), bf16), pltpu.VMEM((seq, LANES), bf16),
                        pltpu.VMEM((2, L, LANES), bf16), pltpu.VMEM((CTX_LEN, LANES), bf16),
                        pltpu.VMEM((L, LANES), bf16)],
        compiler_params=_params("parallel", "parallel"),
        name="na_attn",
    )(u, u, u, cos_t, sin_t, bias_t, mask)


def _route(logits, b_router, sel_ref, gate_ref):
    aff = _sigmoid(logits)
    sel = aff + b_router
    s = [sel[e:e + 1, :] for e in range(N_EXPERTS)]
    one = lambda cond: jnp.where(cond, 1.0, 0.0)
    scores = []
    for g in range(N_GROUPS):
        s0, s1, s2, s3 = s[g * EXPERTS_PER_GROUP:(g + 1) * EXPERTS_PER_GROUP]
        hi01, lo01 = jnp.maximum(s0, s1), jnp.minimum(s0, s1)
        hi23, lo23 = jnp.maximum(s2, s3), jnp.minimum(s2, s3)
        scores.append(jnp.maximum(hi01, hi23) + jnp.maximum(jnp.minimum(hi01, hi23), jnp.maximum(lo01, lo23)))
    weights = []
    chosen = []
    for g in range(N_GROUPS):
        gsel = None
        for g2 in range(N_GROUPS):
            if g2 == g:
                continue
            t = one(scores[g2] < scores[g]) if g2 < g else one(scores[g2] <= scores[g])
            gsel = t if gsel is None else gsel * t
        for i in range(EXPERTS_PER_GROUP):
            e = g * EXPERTS_PER_GROUP + i
            rank = None
            for j in range(EXPERTS_PER_GROUP):
                if j == i:
                    continue
                e2 = g * EXPERTS_PER_GROUP + j
                t = one(s[e2] >= s[e]) if j < i else one(s[e2] > s[e])
                rank = t if rank is None else rank + t
            chosen.append(gsel * one(rank < 1.5))
            weights.append(chosen[-1] * aff[e:e + 1, :])
    total = weights[0]
    for w in weights[1:]:
        total = total + w
    inv = 1.0 / total
    for e in range(N_EXPERTS):
        sel_ref[e:e + 1, :] = chosen[e]
        gate_ref[e:e + 1, :] = weights[e] * inv


def _out_kernel(a_ref, b_ref, x_ref, mod_ref, wa_ref, wb_ref, g_ref, bt_ref, wr_ref, br_ref, tri_ref, low_ref,
                x1_ref, pos_ref, tab_ref, cnt_ref, xs_hbm,
                sel_scr, gate_scr, meta_v, meta_s, xs_vmem, sem, cnt_smem, nblk_smem, *, cap):
    t, _, has_next, is_first = _tile_id(0)
    slot = lax.rem(t, 2)

    @pl.when(is_first)
    def _():
        for e in range(N_EXPERTS):
            cnt_smem[e] = 0
        nblk_smem[0] = 0
        nblk_smem[1] = 0

    y = (jnp.dot(a_ref[...], wa_ref[...], preferred_element_type=f32)
         + jnp.dot(b_ref[...], wb_ref[...], preferred_element_type=f32))
    x1 = _layer_norm(ALPHA * x_ref[...] + mod_ref[2:3, :] * y, g_ref[...], bt_ref[...])
    x1_ref[...] = x1
    xm = x1 * (1.0 + mod_ref[4:5, :]) + mod_ref[3:4, :]
    xm_hi = xm.astype(bf16)
    xm_lo = (xm - xm_hi.astype(f32)).astype(bf16)
    E = N_EXPERTS
    r_hi = lax.dot_general(wr_ref[...], xm_hi, _NT, preferred_element_type=f32)
    r_lo = lax.dot_general(wr_ref[0:2 * E, :], xm_lo, _NT, preferred_element_type=f32)
    logits = (r_hi[0:E] + r_lo[0:E]) + (r_hi[E:2 * E] + r_lo[E:2 * E]) + r_hi[2 * E:3 * E]
    _route(logits, br_ref[...], sel_scr, gate_scr)

    sel = sel_scr[...]
    gate = gate_scr[...]
    sel_b = sel.astype(bf16)
    rank = jnp.dot(sel_b, tri_ref[...], preferred_element_type=f32)
    below = jnp.dot(low_ref[...], sel_b, preferred_element_type=f32)
    n = jnp.sum(sel, axis=1, keepdims=True)
    npad = jnp.broadcast_to(jnp.floor((n + (XS_BLK - 1)) * (1.0 / XS_BLK)) * XS_BLK, (N_EXPERTS, LANES))
    start = jnp.dot(low_ref[...], npad.astype(bf16), preferred_element_type=f32)
    pos = start[:, :1] + rank
    first = jnp.where(below < 0.5, sel, 0.0)
    second = sel - first
    pos0 = jnp.sum(first * pos, axis=0, keepdims=True).astype(jnp.int32)
    pos1 = jnp.sum(second * pos, axis=0, keepdims=True).astype(jnp.int32)
    g0 = jnp.sum(first * gate, axis=0, keepdims=True)
    g1 = jnp.sum(second * gate, axis=0, keepdims=True)
    pos_ref[0:1, :] = pos0
    pos_ref[1:2, :] = pos1
    meta_v[0] = start.astype(jnp.int32)
    meta_v[1] = (npad * (1.0 / XS_BLK)).astype(jnp.int32)
    pltpu.sync_copy(meta_v, meta_s)

    r = lax.broadcasted_iota(jnp.int32, (XS_ROWS, ROW_TILE), 0)
    hit0 = r == pos0
    hit1 = r == pos1
    perm = jnp.where(hit0, 1.0, jnp.where(hit1, 1.0, 0.0)).astype(bf16)
    gate_sorted = jnp.sum(jnp.where(hit0, g0, jnp.where(hit1, g1, 0.0)), axis=1, keepdims=True)
    xs = jnp.dot(perm, xm_hi, preferred_element_type=f32)

    def block_copy(src_row, dst_row, sl):
        return pltpu.make_async_copy(xs_vmem.at[sl, pl.ds(pl.multiple_of(src_row, XS_BLK), XS_BLK), :],
                                     xs_hbm.at[pl.ds(pl.multiple_of(dst_row, XS_BLK), XS_BLK), :],
                                     sem.at[sl])

    def wait_blocks(count, sl):
        def body(i, carry):
            block_copy(0, 0, sl).wait()
            return carry

        lax.fori_loop(0, count, body, 0)

    wait_blocks(nblk_smem[slot], slot)
    xs_vmem[slot, :, :PACK_W] = _pack_pairs(xs)
    xs_vmem[slot, :, PACK_W:] = lax.bitcast_convert_type(
        jnp.broadcast_to(gate_sorted, (XS_ROWS, LANES)), jnp.uint32)
    total = 0
    for e in range(N_EXPERTS):
        start_e = meta_s[0, e, 0]
        nblk_e = meta_s[1, e, 0]
        base = e * cap + cnt_smem[e]
        tab_ref[t, 3 * e] = base
        tab_ref[t, 3 * e + 1] = start_e
        tab_ref[t, 3 * e + 2] = nblk_e

        def body(i, carry):
            block_copy(start_e + i * XS_BLK, base + i * XS_BLK, slot).start()
            return carry

        lax.fori_loop(0, nblk_e, body, 0)
        cnt_smem[e] = cnt_smem[e] + nblk_e * XS_BLK
        total = total + nblk_e
    nblk_smem[slot] = total

    @pl.when(jnp.logical_not(has_next))
    def _():
        wait_blocks(nblk_smem[slot], slot)
        wait_blocks(nblk_smem[1 - slot], 1 - slot)
        for e in range(N_EXPERTS):
            cnt_ref[0, e] = cnt_smem[e]


def _dispatch_constants():
    t = np.arange(ROW_TILE)
    e = np.arange(N_EXPERTS)
    return ((t[:, None] < t[None, :]).astype(np.float32), (e[None, :] < e[:, None]).astype(np.float32))


def _out_proj(l, a, bmix, x, mod, w_out, ln_g, ln_b, w_router_t, b_router, tri, low):
    B, L, _ = x.shape
    half = A_WIDTH
    n_tiles = B * (L // ROW_TILE)
    cap = _expert_capacity(n_tiles)
    smem = pl.BlockSpec(memory_space=pltpu.SMEM)
    return pl.pallas_call(
        functools.partial(_out_kernel, cap=cap),
        grid=(B, L // ROW_TILE),
        in_specs=[_row_spec(A_WIDTH), _row_spec(B_WIDTH), _row_spec(D_MODEL), _mod_spec(),
                  pl.BlockSpec((None, half, D_MODEL), lambda b, j: (l, 0, 0)),
                  pl.BlockSpec((None, half, D_MODEL), lambda b, j: (l, 1, 0)),
                  _vec_spec(l), _vec_spec(l),
                  pl.BlockSpec((3 * N_EXPERTS, D_MODEL), lambda b, j: (0, 0)),
                  pl.BlockSpec((N_EXPERTS, 1), lambda b, j: (0, 0)),
                  pl.BlockSpec((ROW_TILE, ROW_TILE), lambda b, j: (0, 0)),
                  pl.BlockSpec((N_EXPERTS, N_EXPERTS), lambda b, j: (0, 0))],
        out_specs=[_row_spec(D_MODEL), _pos_spec(), smem, smem, pl.BlockSpec(memory_space=pl.ANY)],
        out_shape=[jax.ShapeDtypeStruct(x.shape, f32), jax.ShapeDtypeStruct((B, 2, L), jnp.int32),
                   jax.ShapeDtypeStruct((n_tiles, 3 * N_EXPERTS), jnp.int32),
                   jax.ShapeDtypeStruct((1, N_EXPERTS), jnp.int32),
                   jax.ShapeDtypeStruct((N_EXPERTS * cap, XS_W), jnp.uint32)],
        scratch_shapes=[pltpu.VMEM((N_EXPERTS, ROW_TILE), f32), pltpu.VMEM((N_EXPERTS, ROW_TILE), f32),
                        pltpu.VMEM((2, N_EXPERTS, LANES), jnp.int32), pltpu.SMEM((2, N_EXPERTS, LANES), jnp.int32),
                        pltpu.VMEM((2, XS_ROWS, XS_W), jnp.uint32), pltpu.SemaphoreType.DMA((2,)),
                        pltpu.SMEM((N_EXPERTS,), jnp.int32), pltpu.SMEM((2,), jnp.int32)],
        compiler_params=_params("arbitrary", "arbitrary"),
        name="out_proj",
    )(a, bmix, x, mod, w_out, w_out, ln_g, ln_b, w_router_t, b_router, tri, low)


def _expert_capacity(n_tiles):
    rows = n_tiles * (ROW_TILE + XS_BLK)
    return -(-rows // EXP_TILE) * EXP_TILE


def _moe_kernel(te_ref, tb_ref, tv_ref, xs_ref, wg_ref, wu_ref, wd_ref, ys_ref):
    i = pl.program_id(0)
    valid = tv_ref[i]

    @pl.when(valid > 0)
    def _():
        w32 = xs_ref[...]
        ok = lax.broadcasted_iota(jnp.int32, (EXP_TILE, 1), 0) < valid
        x = _unpack_pairs(jnp.where(ok, w32[:, :PACK_W], jnp.uint32(0)))
        gate = jnp.where(ok, lax.bitcast_convert_type(w32[:, PACK_W:PACK_W + 1], f32), 0.0)
        hg = jnp.dot(x, wg_ref[...].astype(bf16), preferred_element_type=f32)
        hu = jnp.dot(x, wu_ref[...].astype(bf16), preferred_element_type=f32)
        hid = hg * _sigmoid(hg) * hu * gate
        y = jnp.dot(hid.astype(bf16), wd_ref[...].astype(bf16), preferred_element_type=f32)
        ys_ref[...] = _pack_pairs(y.astype(bf16).astype(f32))


def _expert_tiles(cnt, cap, n_steps):
    cnt = cnt.reshape(N_EXPERTS)
    tiles = (cnt + (EXP_TILE - 1)) // EXP_TILE
    ends = jnp.cumsum(tiles)
    n_valid = ends[-1]
    i = jnp.clip(jnp.arange(n_steps, dtype=jnp.int32), 0, jnp.maximum(n_valid - 1, 0))
    e = jnp.sum((i[:, None] >= ends[None, :]).astype(jnp.int32), axis=1)
    local = i - (ends - tiles)[e]
    rows = jnp.clip(cnt[e] - local * EXP_TILE, 0, EXP_TILE)
    rows = jnp.where(jnp.arange(n_steps) < n_valid, rows, 0)
    return e.astype(jnp.int32), (e * (cap // EXP_TILE) + local).astype(jnp.int32), rows.astype(jnp.int32)


def _moe(l, xs, cnt, n_tiles, w_gate, w_up, w_down):
    cap = _expert_capacity(n_tiles)
    n_steps = n_tiles * XS_ROWS // EXP_TILE + N_EXPERTS
    te, tb, tv = _expert_tiles(cnt, cap, n_steps)
    grid_spec = pltpu.PrefetchScalarGridSpec(
        num_scalar_prefetch=3,
        grid=(n_steps,),
        in_specs=[pl.BlockSpec((EXP_TILE, XS_W), lambda i, te, tb, tv: (tb[i], 0)),
                  pl.BlockSpec((None, None, D_MODEL, D_EXPERT), lambda i, te, tb, tv: (l, te[i], 0, 0)),
                  pl.BlockSpec((None, None, D_MODEL, D_EXPERT), lambda i, te, tb, tv: (l, te[i], 0, 0)),
                  pl.BlockSpec((None, None, D_EXPERT, D_MODEL), lambda i, te, tb, tv: (l, te[i], 0, 0))],
        out_specs=pl.BlockSpec((EXP_TILE, PACK_W), lambda i, te, tb, tv: (tb[i], 0)),
    )
    return pl.pallas_call(
        _moe_kernel,
        grid_spec=grid_spec,
        out_shape=jax.ShapeDtypeStruct((N_EXPERTS * cap, PACK_W), jnp.uint32),
        compiler_params=_params("arbitrary"),
        name="moe",
    )(te, tb, tv, xs, w_gate, w_up, w_down)


def _final_kernel(tab_ref, x_ref, pos_ref, ys_hbm, mod_ref, g_ref, b_ref, o_ref, y_vmem, sem, *, j_off):
    f = _combine_experts(tab_ref, pos_ref, ys_hbm, y_vmem, sem, j_off)
    o_ref[...] = _layer_norm(ALPHA * x_ref[...] + mod_ref[5:6, :] * f, g_ref[...], b_ref[...])


def _final_ln(x, moe, mod, ln_g, ln_b):
    B, L, _ = x.shape
    seq = L - CTX_LEN
    skip = CTX_LEN // ROW_TILE
    tab, pos, ys = moe
    lat = pl.BlockSpec((None, ROW_TILE, D_MODEL), lambda b, j: (b, j + skip, 0))
    return pl.pallas_call(
        functools.partial(_final_kernel, j_off=skip),
        grid=(B, seq // ROW_TILE),
        in_specs=[pl.BlockSpec(memory_space=pltpu.SMEM), lat, _pos_spec(skip), pl.BlockSpec(memory_space=pl.ANY),
                  pl.BlockSpec((None, None, 6, D_MODEL), lambda b, j: (b, 1, 0, 0)),
                  _vec_spec(DEPTH - 1), _vec_spec(DEPTH - 1)],
        out_specs=_row_spec(D_MODEL),
        out_shape=jax.ShapeDtypeStruct((B, seq, D_MODEL), f32),
        scratch_shapes=_combine_scratch(),
        compiler_params=_params("arbitrary", "arbitrary"),
        name="final_ln",
    )(tab, x, pos, ys, mod, ln_g, ln_b)


def kernel(x, c, ctx, c_ctx, w_ada, b_ada, w_in, lb_logits, a_norm_g, rpb, w_out, ln1_g, ln1_b,
           w_router, b_router, w_gate, w_up, w_down, ln2_g, ln2_b):
    B, seq, D = x.shape
    L = CTX_LEN + seq
    xs = jnp.concatenate([ctx, x], axis=1)

    cc = jnp.concatenate([c, c_ctx[None, :], jnp.zeros((16 - B - 1, D), f32)], axis=0)
    mods = _ada_mods(cc, w_ada, b_ada)
    lat_mod = mods[:, :B].reshape(DEPTH, B, 1, 6, D)
    ctx_mod = jnp.broadcast_to(mods[:, B].reshape(DEPTH, 1, 1, 6, D), (DEPTH, B, 1, 6, D))
    mod_all = jnp.concatenate([ctx_mod, lat_mod], axis=2)

    sm = jax.nn.softmax(lb_logits.astype(f32), axis=1)
    lower = jnp.cumsum(sm, axis=1) - sm[:, :1]

    tri_np, own_np, sgn_np = _scan_constants()
    scan_consts = (jnp.asarray(tri_np, bf16), jnp.asarray(own_np), jnp.asarray(sgn_np))
    cos_np, sin_np = _rope_tables(seq)
    cos_t, sin_t = jnp.asarray(cos_np), jnp.asarray(sin_np)
    mask = jnp.asarray(_window_mask())
    bias_t = _bias_tables(rpb.astype(f32) * LOG2E)

    w_in_b = w_in.astype(bf16)
    w_out_b = w_out.astype(bf16)
    wr_hi = w_router.T.astype(bf16)
    wr_res = w_router.T - wr_hi.astype(f32)
    wr_mid = wr_res.astype(bf16)
    wr_lo = (wr_res - wr_mid.astype(f32)).astype(bf16)
    w_router_t = jnp.concatenate([wr_hi, wr_mid, wr_lo], axis=0)
    b_router_c = b_router.reshape(N_EXPERTS, 1)
    ln1_g3, ln1_b3 = ln1_g.reshape(DEPTH, 1, D), ln1_b.reshape(DEPTH, 1, D)
    ln2_g3, ln2_b3 = ln2_g.reshape(DEPTH, 1, D), ln2_b.reshape(DEPTH, 1, D)

    tri_np, low_np = _dispatch_constants()
    tri, low = jnp.asarray(tri_np, bf16), jnp.asarray(low_np, bf16)
    n_tiles = B * (L // ROW_TILE)

    moe = None
    for l in range(DEPTH):
        pmod = mod_all[l - 1] if l > 0 else None
        xs, u = _in_proj(l, xs, moe, pmod, ln2_g3, ln2_b3, mod_all[l], w_in_b)
        a = _hgrn(u, lower[0, l], lower[1, l], a_norm_g[l], scan_consts)
        bmix = _na(l, u, cos_t, sin_t, bias_t, mask)
        xs, pos, tab, cnt, x_sorted = _out_proj(l, a, bmix, xs, mod_all[l], w_out_b, ln1_g3, ln1_b3,
                                                w_router_t, b_router_c, tri, low)
        y_sorted = _moe(l, x_sorted, cnt, n_tiles, w_gate, w_up, w_down)
        moe = (tab, pos, y_sorted)
    return _final_ln(xs, moe, mod_all[DEPTH - 1], ln2_g3, ln2_b3)
```

```python
import functools

import numpy as np
import jax
import jax.numpy as jnp
from jax import lax
from jax.experimental import pallas as pl
from jax.experimental.pallas import tpu as pltpu

D_MODEL = 1024
DEPTH = 4
GRID_W = 64
CTX_LEN = 256
A_HEADS = 4
A_DK = 128
A_WIDTH = A_HEADS * A_DK
B_HEADS = 8
B_HD = 64
B_WIDTH = B_HEADS * B_HD
D_IN = 5 * A_WIDTH + 3 * B_WIDTH
NA_ROWS = 8
NA_COLS = 16
ROPE_BASE = 10000.0
N_EXPERTS = 16
N_GROUPS = 4
EXPERTS_PER_GROUP = N_EXPERTS // N_GROUPS
D_EXPERT = 512
ALPHA = (2 * DEPTH) ** 0.25
LN_EPS = 1e-5
RMS_EPS = 1e-6
F_FLOOR = 1e-6
MASK_VALUE = -1e30
LOG2E = 1.4426950408889634

LANES = 128
ROW_TILE = 256
XS_BLK = 8
XS_ROWS = 2 * ROW_TILE + N_EXPERTS * XS_BLK
PACK_W = D_MODEL // 2
XS_W = PACK_W + LANES
DRAIN_GROUP = 8
EXP_TILE = 512
SCAN_CHUNK = 128
SCAN_LEVELS = (64, 32, 16, 8, 4)
SCAN_BASE = 4
NA_GROUP = 4
VMEM_LIMIT = 56 * 1024 * 1024

f32 = jnp.float32
bf16 = jnp.bfloat16

_NT = (((1,), (1,)), ((), ()))
_TN = (((0,), (0,)), ((), ()))


def _params(*sem):
    return pltpu.CompilerParams(dimension_semantics=sem, vmem_limit_bytes=VMEM_LIMIT)


def _sigmoid(z, sign=1.0):
    return 1.0 / (1.0 + jnp.exp2(z * (-sign * LOG2E)))


def _layer_norm(h, g, b):
    mu = jnp.mean(h, axis=-1, keepdims=True)
    d = h - mu
    var = jnp.mean(d * d, axis=-1, keepdims=True)
    return d * lax.rsqrt(var + LN_EPS) * g + b


def _ada_kernel(c_ref, w_ref, b_ref, o_ref):
    c = c_ref[...]
    o_ref[...] = jnp.dot(c * _sigmoid(c), w_ref[...], preferred_element_type=f32,
                         precision=lax.Precision.HIGHEST) + b_ref[...]


def _ada_mods(cc, w_ada, b_ada):
    n_col = 4
    tn = 6 * D_MODEL // n_col
    return pl.pallas_call(
        _ada_kernel,
        grid=(DEPTH, n_col),
        in_specs=[
            pl.BlockSpec((16, D_MODEL), lambda l, j: (0, 0)),
            pl.BlockSpec((None, D_MODEL, tn), lambda l, j: (l, 0, j)),
            pl.BlockSpec((None, 1, tn), lambda l, j: (l, 0, j)),
        ],
        out_specs=pl.BlockSpec((None, 16, tn), lambda l, j: (l, 0, j)),
        out_shape=jax.ShapeDtypeStruct((DEPTH, 16, 6 * D_MODEL), f32),
        compiler_params=_params("parallel", "parallel"),
        name="ada_mods",
    )(cc, w_ada, b_ada.reshape(DEPTH, 1, 6 * D_MODEL))


def _pack_pairs(x):
    bits = lax.bitcast_convert_type(x, jnp.uint32)
    half = x.shape[1] // 2
    return (bits[:, :half] >> 16) | (bits[:, half:] & jnp.uint32(0xFFFF0000))


def _unpack_pairs(w):
    lo = lax.bitcast_convert_type(w << 16, f32)
    hi = lax.bitcast_convert_type(w & jnp.uint32(0xFFFF0000), f32)
    return jnp.concatenate([lo, hi], axis=1).astype(bf16)


def _drain(copy_of_rows, n_blocks):
    def grouped(i, carry):
        copy_of_rows(DRAIN_GROUP * XS_BLK).wait()
        return carry

    def single(i, carry):
        copy_of_rows(XS_BLK).wait()
        return carry

    lax.fori_loop(0, n_blocks // DRAIN_GROUP, grouped, 0)
    lax.fori_loop(0, lax.rem(n_blocks, DRAIN_GROUP), single, 0)


def _tile_id(j_off):
    b, j = pl.program_id(0), pl.program_id(1)
    nj = pl.num_programs(1)
    tiles_per_batch = nj + j_off
    t = b * tiles_per_batch + j + j_off
    last_j = j == nj - 1
    t_next = jnp.where(last_j, (b + 1) * tiles_per_batch + j_off, t + 1)
    has_next = jnp.logical_not(jnp.logical_and(last_j, b == pl.num_programs(0) - 1))
    return t, t_next, has_next, jnp.logical_and(b == 0, j == 0)


def _combine_experts(tab_ref, pos_ref, ys_hbm, y_vmem, sem, j_off):
    t, t_next, has_next, is_first = _tile_id(j_off)
    step = pl.program_id(0) * pl.num_programs(1) + pl.program_id(1)
    slot = lax.rem(step, 2)

    def block_copy(src_row, dst_row, sl, rows=XS_BLK):
        return pltpu.make_async_copy(ys_hbm.at[pl.ds(pl.multiple_of(src_row, XS_BLK), rows), :],
                                     y_vmem.at[sl, pl.ds(pl.multiple_of(dst_row, XS_BLK), rows), :],
                                     sem.at[sl])

    def fetch(tile, sl):
        y_vmem[sl] = jnp.zeros((XS_ROWS, PACK_W), jnp.uint32)
        for e in range(N_EXPERTS):
            base, start, nblk = tab_ref[tile, 3 * e], tab_ref[tile, 3 * e + 1], tab_ref[tile, 3 * e + 2]

            def body(i, carry):
                block_copy(base + i * XS_BLK, start + i * XS_BLK, sl).start()
                return carry

            lax.fori_loop(0, nblk, body, 0)

    def wait_all(tile, sl):
        total = tab_ref[tile, 2]
        for e in range(1, N_EXPERTS):
            total = total + tab_ref[tile, 3 * e + 2]
        _drain(lambda rows: block_copy(0, 0, sl, rows), total)

    @pl.when(is_first)
    def _():
        fetch(t, slot)

    @pl.when(has_next)
    def _():
        fetch(t_next, 1 - slot)

    wait_all(t, slot)
    y = _unpack_pairs(y_vmem[slot])
    r = lax.broadcasted_iota(jnp.int32, (XS_ROWS, ROW_TILE), 0)
    perm = jnp.where(r == pos_ref[0:1, :], 1.0, jnp.where(r == pos_ref[1:2, :], 1.0, 0.0)).astype(bf16)
    return lax.dot_general(perm, y, _TN, preferred_element_type=f32)


def _in_kernel(*refs, with_ln):
    if with_ln:
        (tab_ref, x_ref, pos_ref, ys_hbm, pmod_ref, g_ref, b_ref, mod_ref, w_ref,
         xo_ref, u_ref, y_vmem, sem) = refs
        f = _combine_experts(tab_ref, pos_ref, ys_hbm, y_vmem, sem, 0)
        h = ALPHA * x_ref[...] + pmod_ref[5:6, :] * f
        x = _layer_norm(h, g_ref[...], b_ref[...])
        xo_ref[...] = x
    else:
        x_ref, mod_ref, w_ref, u_ref = refs
        x = x_ref[...]
    xm = x * (1.0 + mod_ref[1:2, :]) + mod_ref[0:1, :]
    u_ref[...] = jnp.dot(xm.astype(bf16), w_ref[...], preferred_element_type=f32)


def _mod_spec():
    return pl.BlockSpec((None, None, 6, D_MODEL), lambda b, j: (b, jnp.minimum(j, 1), 0, 0))


def _row_spec(width):
    return pl.BlockSpec((None, ROW_TILE, width), lambda b, j: (b, j, 0))


def _vec_spec(l):
    return pl.BlockSpec((None, 1, D_MODEL), lambda b, j: (l, 0, 0))


def _pos_spec(j_off=0):
    return pl.BlockSpec((None, 2, ROW_TILE), lambda b, j: (b, 0, j + j_off))


def _combine_scratch():
    return [pltpu.VMEM((2, XS_ROWS, PACK_W), jnp.uint32), pltpu.SemaphoreType.DMA((2,))]


def _in_proj(l, x, moe, pmod, ln_g, ln_b, mod, w_in):
    B, L, _ = x.shape
    w_spec = pl.BlockSpec((None, D_MODEL, D_IN), lambda b, j: (l, 0, 0))
    u_shape = jax.ShapeDtypeStruct((B, L, D_IN), f32)
    if moe is None:
        u = pl.pallas_call(
            functools.partial(_in_kernel, with_ln=False),
            grid=(B, L // ROW_TILE),
            in_specs=[_row_spec(D_MODEL), _mod_spec(), w_spec],
            out_specs=_row_spec(D_IN),
            out_shape=u_shape,
            compiler_params=_params("parallel", "parallel"),
            name="in_proj0",
        )(x, mod, w_in)
        return x, u
    tab, pos, ys = moe
    return pl.pallas_call(
        functools.partial(_in_kernel, with_ln=True),
        grid=(B, L // ROW_TILE),
        in_specs=[pl.BlockSpec(memory_space=pltpu.SMEM), _row_spec(D_MODEL), _pos_spec(),
                  pl.BlockSpec(memory_space=pl.ANY), _mod_spec(), _vec_spec(l - 1), _vec_spec(l - 1),
                  _mod_spec(), w_spec],
        out_specs=[_row_spec(D_MODEL), _row_spec(D_IN)],
        out_shape=[jax.ShapeDtypeStruct(x.shape, f32), u_shape],
        scratch_shapes=_combine_scratch(),
        compiler_params=_params("arbitrary", "arbitrary"),
        name="in_proj",
    )(tab, x, pos, ys, pmod, ln_g, ln_b, mod, w_in)


def _scan_constants():
    C = SCAN_CHUNK
    p = np.arange(C)
    tri = (p[None, :] <= p[:, None]).astype(np.float32)
    owner = np.full((C, C), -1, np.int32)
    signs = []
    for li, m in enumerate(SCAN_LEVELS):
        same = (p[:, None] // (2 * m)) == (p[None, :] // (2 * m))
        owner[same & ((p[:, None] // m) % 2 == 1) & ((p[None, :] // m) % 2 == 0)] = li
        signs.append(np.where((p // m) % 2 == 1, 1.0, -1.0))
    owner[((p[:, None] // SCAN_BASE) == (p[None, :] // SCAN_BASE)) & (p[None, :] <= p[:, None])] = len(SCAN_LEVELS)
    sgn = np.broadcast_to(np.stack(signs)[:, :, None], (len(SCAN_LEVELS), C, LANES)).astype(np.float32)
    return (np.stack([tri, tri[::-1, ::-1]]), np.stack([owner, owner[::-1, ::-1]]),
            np.stack([sgn, sgn[:, ::-1]]))


def _hgrn_kernel(q_ref, ff_ref, fb_ref, v_ref, g_ref, lbf_ref, lbb_ref, ng_ref, t_ref, own_ref, sgn_ref,
                 a_ref, of_scr, ob_scr, *, n_ctx, n_all):
    C = SCAN_CHUNK
    n_lv = len(SCAN_LEVELS)


    def gates(rows, z_ref, lb, d):
        z = z_ref[rows, :]
        lf = jnp.log2(jnp.maximum(lb + (1.0 - lb) * _sigmoid(z), F_FLOOR))
        k = (1.0 - lb) * _sigmoid(z, -1.0)
        qz = q_ref[rows, :]
        q = qz * _sigmoid(qz)
        v = v_ref[rows, :].astype(bf16)
        hi = lf.astype(bf16)
        r1 = lf - hi.astype(f32)
        mid = r1.astype(bf16)
        lo = (r1 - mid.astype(f32)).astype(bf16)
        d3 = jnp.dot(t_ref[d], jnp.concatenate([hi, mid, lo], axis=1), preferred_element_type=f32)
        cum = d3[:, :LANES] + d3[:, LANES:2 * LANES] + d3[:, 2 * LANES:]
        return q, k, v, cum

    def intra(q, k, cum, d):
        last = cum[C - 1:C, :] if d == 0 else cum[0:1, :]

        def rel(block, ref_rows):
            blocks = cum.reshape(C // block, block, LANES)
            if len(ref_rows) == 1:
                ref = blocks[:, ref_rows[0]:ref_rows[0] + 1, :]
            else:
                first = lax.broadcasted_iota(jnp.int32, (1, block, 1), 1) < block // 2
                ref = jnp.where(first, blocks[:, ref_rows[0]:ref_rows[0] + 1, :],
                                blocks[:, ref_rows[1]:ref_rows[1] + 1, :])
            return (blocks - ref).reshape(C, LANES)

        def to_boundary(m):
            blocks = cum.reshape(C // (2 * m), 2 * m, LANES)
            lo, hi = blocks[:, :m, :], blocks[:, m:, :]
            if d == 0:
                ref = lo[:, m - 1:m, :]
                parts = (ref - lo, hi - ref)
            else:
                ref = hi[:, 0:1, :]
                parts = (lo - ref, ref - hi)
            return jnp.concatenate(parts, axis=1).reshape(C, LANES)

        q_in = (q * jnp.exp2(cum)).astype(bf16)
        own = own_ref[d]
        scores = jnp.zeros((C, C), f32)
        for li in range(n_lv + 1):
            if li < n_lv:
                m = SCAN_LEVELS[li]
                if m % 8 == 0:
                    wq = jnp.exp2(to_boundary(m))
                else:
                    wq = jnp.exp2(rel(2 * m, (m - 1,) if d == 0 else (m,)) * sgn_ref[d, li])
                wk = wq
            else:
                nb = SCAN_BASE
                dl = rel(2 * nb, (0, nb) if d == 0 else (nb - 1, 2 * nb - 1))
                wq = jnp.exp2(dl)
                wk = jnp.exp2(-dl)
            a = lax.dot_general((q * wq).astype(bf16), (k * wk).astype(bf16), _NT,
                                preferred_element_type=f32)
            scores = jnp.where(own == li, a, scores)
        kdec = (k * jnp.exp2(last - cum)).astype(bf16)
        return q_in, scores.astype(bf16), kdec, jnp.exp2(last)

    def finish(pending, S, o_scr):
        q_in, scores, kdec, decay, v, row0 = pending
        o = (lax.dot_general(q_in, S.astype(bf16), _NT, preferred_element_type=f32)
             + jnp.dot(scores, v, preferred_element_type=f32))
        o_scr[pl.ds(pl.multiple_of(row0, C), C), :] = o
        return S * decay + lax.dot_general(v, kdec, _TN, preferred_element_type=f32)

    lbf = lbf_ref[...]
    lbb = lbb_ref[...]

    def bwd_chunk(i):
        return jnp.where(i < n_ctx, n_ctx - 1 - i, n_all - 1 - (i - n_ctx))

    def scan_body(i, carry):
        s_f, s_b, pend_f, pend_b = carry
        row_f = i * C
        row_b = bwd_chunk(i) * C
        s_f = finish(pend_f, s_f, of_scr)
        s_b = finish(pend_b, s_b, ob_scr)
        qf, kf, vf, cum_f = gates(pl.ds(pl.multiple_of(row_f, C), C), ff_ref, lbf, 0)
        qb, kb, vb, cum_b = gates(pl.ds(pl.multiple_of(row_b, C), C), fb_ref, lbb, 1)
        pend_f = intra(qf, kf, cum_f, 0) + (vf, row_f)
        pend_b = intra(qb, kb, cum_b, 1) + (vb, row_b)
        return s_f, s_b, pend_f, pend_b

    def idle(row0):
        zc = jnp.zeros((C, LANES), bf16)
        return (zc, jnp.zeros((C, C), bf16), zc, jnp.ones((1, LANES), f32), zc, row0)

    zero = jnp.zeros((A_DK, A_DK), f32)
    init = (zero, zero, idle(jnp.int32(0)), idle(bwd_chunk(jnp.int32(0)) * C))
    s_f, s_b, pend_f, pend_b = lax.fori_loop(0, n_all, scan_body, init, unroll=3)
    finish(pend_f, s_f, of_scr)
    finish(pend_b, s_b, ob_scr)

    ng = ng_ref[...]

    def readout_body(i, carry):
        rows = pl.ds(pl.multiple_of(i * ROW_TILE, ROW_TILE), ROW_TILE)
        o = of_scr[rows, :] + ob_scr[rows, :]
        o = o * lax.rsqrt(jnp.mean(o * o, axis=-1, keepdims=True) + RMS_EPS) * ng
        gz = g_ref[rows, :]
        a_ref[rows, :] = (o * (gz * _sigmoid(gz))).astype(a_ref.dtype)
        return carry

    lax.fori_loop(0, n_all * C // ROW_TILE, readout_body, 0)


def _hgrn(u, lb_f, lb_b, norm_g, scan_consts):
    B, L, _ = u.shape
    hb = A_WIDTH // LANES

    def col(sec):
        return pl.BlockSpec((None, L, LANES), lambda b, h: (b, 0, sec * hb + h))

    def whole(arr):
        return pl.BlockSpec(arr.shape, lambda b, h: (0,) * arr.ndim)

    lb_spec = pl.BlockSpec((None, 1, LANES), lambda b, h: (h, 0, 0))
    return pl.pallas_call(
        functools.partial(_hgrn_kernel, n_ctx=CTX_LEN // SCAN_CHUNK, n_all=L // SCAN_CHUNK),
        grid=(B, A_HEADS),
        in_specs=[col(0), col(1), col(2), col(3), col(4), lb_spec, lb_spec,
                  pl.BlockSpec((1, LANES), lambda b, h: (0, 0))] + [whole(a) for a in scan_consts],
        out_specs=pl.BlockSpec((None, L, LANES), lambda b, h: (b, 0, h)),
        out_shape=jax.ShapeDtypeStruct((B, L, A_WIDTH), bf16),
        scratch_shapes=[pltpu.VMEM((L, LANES), f32), pltpu.VMEM((L, LANES), f32)],
        compiler_params=_params("parallel", "parallel"),
        name="hgrn2",
    )(u, u, u, u, u, lb_f.reshape(A_HEADS, 1, LANES), lb_b.reshape(A_HEADS, 1, LANES),
      norm_g.reshape(1, LANES), *scan_consts)


def _rope_tables(seq):
    t = np.arange(seq)
    pos_r = (t // GRID_W).astype(np.float32)
    pos_c = (t % GRID_W).astype(np.float32)
    quarter = B_HD // 4
    inv = (ROPE_BASE ** (-np.arange(quarter, dtype=np.float32) / quarter)).astype(np.float32)
    lane = np.arange(LANES)
    in_head = lane % B_HD
    pos = np.where((in_head < B_HD // 2)[None, :], pos_r[:, None], pos_c[:, None])
    ang = (pos * inv[lane % quarter][None, :]).astype(np.float32)
    sign = np.where((lane % (2 * quarter)) < quarter, -1.0, 1.0)
    return np.cos(ang).astype(np.float32), (np.sin(ang) * sign[None, :]).astype(np.float32)


def _window_mask():
    c = np.arange(GRID_W)
    cs = np.clip(c - NA_COLS // 2, 0, GRID_W - NA_COLS)
    in_win = (c[None, :] >= cs[:, None]) & (c[None, :] < cs[:, None] + NA_COLS)
    return np.tile(in_win, (1, NA_ROWS)).astype(np.float32)


def _bias_tables(rpb):
    c = np.arange(GRID_W)
    dc = np.clip(c[None, :] - c[:, None], -(NA_COLS - 1), NA_COLS - 1) + (NA_COLS - 1)
    pick = (np.arange(2 * NA_COLS - 1)[:, None, None] == dc[None]).astype(np.float32)
    cols = jnp.einsum('lhdk,kcx->lhdcx', rpb, jnp.asarray(pick), precision=lax.Precision.HIGHEST)
    t = jnp.stack([cols[:, :, o:o + NA_ROWS] for o in range(NA_ROWS)], axis=2)
    t = t.transpose(0, 1, 2, 4, 3, 5)
    return t.reshape(DEPTH, B_HEADS, NA_ROWS, GRID_W, NA_ROWS * GRID_W)


def _na_kernel(q_ref, k_ref, v_ref, cos_ref, sin_ref, bias_ref, mask_ref, o_ref,
               qr_scr, kr_scr, qp_scr, kc_scr, v_scr, *, seq, rows_grid):
    hd = B_HD
    scale = hd ** -0.5 * LOG2E
    n_win = NA_ROWS * GRID_W
    lane = lax.broadcasted_iota(jnp.int32, (ROW_TILE, LANES), 1)
    first_half = (lane % (hd // 2)) < (hd // 4)
    head0 = lane < hd

    def swap_halves(x):
        return jnp.where(first_half, pltpu.roll(x, LANES - hd // 4, axis=1), pltpu.roll(x, hd // 4, axis=1))

    def store_per_head(dst, rows, x):
        dst[0, rows, :] = jnp.where(head0, x, 0.0).astype(bf16)
        dst[1, rows, :] = jnp.where(head0, 0.0, x).astype(bf16)

    ctx_rows = pl.ds(0, CTX_LEN)
    store_per_head(qp_scr, ctx_rows, q_ref[ctx_rows, :] * scale)
    kc_scr[...] = k_ref[ctx_rows, :].astype(bf16)
    v_scr[ctx_rows, :] = v_ref[ctx_rows, :].astype(bf16)

    def prep(i, carry):
        lrows = pl.ds(pl.multiple_of(i * ROW_TILE, ROW_TILE), ROW_TILE)
        rows = pl.ds(pl.multiple_of(CTX_LEN + i * ROW_TILE, ROW_TILE), ROW_TILE)
        q = q_ref[rows, :] * scale
        k = k_ref[rows, :]
        cos = cos_ref[lrows, :]
        sin = sin_ref[lrows, :]
        store_per_head(qp_scr, rows, q)
        v_scr[rows, :] = v_ref[rows, :].astype(bf16)
        store_per_head(qr_scr, lrows, q * cos + swap_halves(q) * sin)
        kr_scr[lrows, :] = (k * cos + swap_halves(k) * sin).astype(bf16)
        return carry

    lax.fori_loop(0, seq // ROW_TILE, prep, 0)

    def pick_heads(o0, o1):
        keep0 = lax.broadcasted_iota(jnp.int32, o0.shape, 1) < hd
        return jnp.where(keep0, o0, o1)

    outs = []
    for h in range(2):
        s = lax.dot_general(qp_scr[h, ctx_rows, :], kc_scr[...], _NT, preferred_element_type=f32)
        e = jnp.exp2(s - jnp.max(s, axis=-1, keepdims=True))
        o = jnp.dot(e.astype(bf16), v_scr[ctx_rows, :], preferred_element_type=f32)
        outs.append(o / jnp.sum(e, axis=-1, keepdims=True))
    o_ref[ctx_rows, :] = pick_heads(*outs).astype(o_ref.dtype)

    def rows_body(it, carry):
        chains = []
        for g in range(NA_GROUP):
            r = it * NA_GROUP + g
            rs = jnp.clip(r - NA_ROWS // 2, 0, rows_grid - NA_ROWS)
            for h in range(2):
                chains.append(dict(
                    h=h, off=rs - r + (NA_ROWS - 1),
                    qrows=pl.ds(pl.multiple_of(r * GRID_W, GRID_W), GRID_W),
                    krows=pl.ds(pl.multiple_of(rs * GRID_W, GRID_W), n_win),
                    arows=pl.ds(pl.multiple_of(CTX_LEN + r * GRID_W, GRID_W), GRID_W),
                    vrows=pl.ds(pl.multiple_of(CTX_LEN + rs * GRID_W, GRID_W), n_win)))
        for c in chains:
            h = c["h"]
            c["s_lat"] = lax.dot_general(qr_scr[h, c["qrows"], :], kr_scr[c["krows"], :], _NT,
                                         preferred_element_type=f32)
            c["s_ctx"] = lax.dot_general(qp_scr[h, c["arows"], :], kc_scr[...], _NT, preferred_element_type=f32)
        for c in chains:
            s_lat = jnp.where(mask_ref[...] > 0.5, c["s_lat"] + bias_ref[c["h"], c["off"]], MASK_VALUE)
            m = jnp.maximum(jnp.max(s_lat, axis=-1, keepdims=True), jnp.max(c["s_ctx"], axis=-1, keepdims=True))
            e_lat = jnp.exp2(s_lat - m)
            e_ctx = jnp.exp2(c["s_ctx"] - m)
            c["den"] = jnp.sum(e_lat, axis=-1, keepdims=True) + jnp.sum(e_ctx, axis=-1, keepdims=True)
            c["e_lat"] = e_lat.astype(bf16)
            c["e_ctx"] = e_ctx.astype(bf16)
        for c in chains:
            c["o"] = (jnp.dot(c["e_lat"], v_scr[c["vrows"], :], preferred_element_type=f32)
                      + jnp.dot(c["e_ctx"], v_scr[ctx_rows, :], preferred_element_type=f32))
        for g in range(NA_GROUP):
            c0, c1 = chains[2 * g:2 * g + 2]
            o_ref[c0["arows"], :] = pick_heads(c0["o"] / c0["den"], c1["o"] / c1["den"]).astype(o_ref.dtype)
        return carry

    lax.fori_loop(0, rows_grid // NA_GROUP, rows_body, 0)


def _na(l, u, cos_t, sin_t, bias_t, mask):
    B, L, _ = u.shape
    seq = L - CTX_LEN
    first = 5 * A_WIDTH // LANES
    pairs = B_WIDTH // LANES

    def col(sec):
        return pl.BlockSpec((None, L, LANES), lambda b, h: (b, 0, first + sec * pairs + h))

    n_win = NA_ROWS * GRID_W
    return pl.pallas_call(
        functools.partial(_na_kernel, seq=seq, rows_grid=seq // GRID_W),
        grid=(B, pairs),
        in_specs=[col(0), col(1), col(2),
                  pl.BlockSpec((seq, LANES), lambda b, h: (0, 0)),
                  pl.BlockSpec((seq, LANES), lambda b, h: (0, 0)),
                  pl.BlockSpec((None, 2, NA_ROWS, GRID_W, n_win), lambda b, h: (l, h, 0, 0, 0)),
                  pl.BlockSpec((GRID_W, n_win), lambda b, h: (0, 0))],
        out_specs=pl.BlockSpec((None, L, LANES), lambda b, h: (b, 0, h)),
        out_shape=jax.ShapeDtypeStruct((B, L, B_WIDTH), bf16),
        scratch_shapes=[pltpu.VMEM((2, seq, LANES), bf16), pltpu.VMEM((seq, LANES), bf16),
                        pltpu.VMEM((2, L, LANES), bf16), pltpu.VMEM((CTX_LEN, LANES), bf16),
                        pltpu.VMEM((L, LANES), bf16)],
        compiler_params=_params("parallel", "parallel"),
        name="na_attn",
    )(u, u, u, cos_t, sin_t, bias_t, mask)


def _route(logits, b_router, sel_ref, gate_ref):
    aff = _sigmoid(logits)
    sel = aff + b_router
    s = [sel[e:e + 1, :] for e in range(N_EXPERTS)]
    one = lambda cond: jnp.where(cond, 1.0, 0.0)
    scores = []
    for g in range(N_GROUPS):
        s0, s1, s2, s3 = s[g * EXPERTS_PER_GROUP:(g + 1) * EXPERTS_PER_GROUP]
        hi01, lo01 = jnp.maximum(s0, s1), jnp.minimum(s0, s1)
        hi23, lo23 = jnp.maximum(s2, s3), jnp.minimum(s2, s3)
        scores.append(jnp.maximum(hi01, hi23) + jnp.maximum(jnp.minimum(hi01, hi23), jnp.maximum(lo01, lo23)))
    weights = []
    chosen = []
    for g in range(N_GROUPS):
        gsel = None
        for g2 in range(N_GROUPS):
            if g2 == g:
                continue
            t = one(scores[g2] < scores[g]) if g2 < g else one(scores[g2] <= scores[g])
            gsel = t if gsel is None else gsel * t
        for i in range(EXPERTS_PER_GROUP):
            e = g * EXPERTS_PER_GROUP + i
            rank = None
            for j in range(EXPERTS_PER_GROUP):
                if j == i:
                    continue
                e2 = g * EXPERTS_PER_GROUP + j
                t = one(s[e2] >= s[e]) if j < i else one(s[e2] > s[e])
                rank = t if rank is None else rank + t
            chosen.append(gsel * one(rank < 1.5))
            weights.append(chosen[-1] * aff[e:e + 1, :])
    total = weights[0]
    for w in weights[1:]:
        total = total + w
    inv = 1.0 / total
    for e in range(N_EXPERTS):
        sel_ref[e:e + 1, :] = chosen[e]
        gate_ref[e:e + 1, :] = weights[e] * inv


def _out_kernel(a_ref, b_ref, x_ref, mod_ref, wa_ref, wb_ref, g_ref, bt_ref, wr_ref, br_ref, tri_ref, low_ref,
                x1_ref, pos_ref, tab_ref, cnt_ref, xs_hbm,
                sel_scr, gate_scr, meta_v, meta_s, xs_vmem, sem, cnt_smem, nblk_smem, *, cap):
    t, _, has_next, is_first = _tile_id(0)
    slot = lax.rem(t, 2)

    @pl.when(is_first)
    def _():
        for e in range(N_EXPERTS):
            cnt_smem[e] = 0
        nblk_smem[0] = 0
        nblk_smem[1] = 0

    y = (jnp.dot(a_ref[...], wa_ref[...], preferred_element_type=f32)
         + jnp.dot(b_ref[...], wb_ref[...], preferred_element_type=f32))
    x1 = _layer_norm(ALPHA * x_ref[...] + mod_ref[2:3, :] * y, g_ref[...], bt_ref[...])
    x1_ref[...] = x1
    xm = x1 * (1.0 + mod_ref[4:5, :]) + mod_ref[3:4, :]
    xm_hi = xm.astype(bf16)
    xm_res = xm - xm_hi.astype(f32)
    xm_mid = xm_res.astype(bf16)
    xm_lo = (xm_res - xm_mid.astype(f32)).astype(bf16)
    E = N_EXPERTS
    r_hi = lax.dot_general(wr_ref[...], xm_hi, _NT, preferred_element_type=f32)
    r_mid = lax.dot_general(wr_ref[0:2 * E, :], xm_mid, _NT, preferred_element_type=f32)
    r_lo = lax.dot_general(wr_ref[0:E, :], xm_lo, _NT, preferred_element_type=f32)
    logits = ((r_hi[0:E] + r_mid[0:E]) + (r_hi[E:2 * E] + r_mid[E:2 * E])) + (r_hi[2 * E:3 * E] + r_lo)
    _route(logits, br_ref[...], sel_scr, gate_scr)

    sel = sel_scr[...]
    gate = gate_scr[...]
    sel_b = sel.astype(bf16)
    rank = jnp.dot(sel_b, tri_ref[...], preferred_element_type=f32)
    below = jnp.dot(low_ref[...], sel_b, preferred_element_type=f32)
    n = jnp.sum(sel, axis=1, keepdims=True)
    npad = jnp.broadcast_to(jnp.floor((n + (XS_BLK - 1)) * (1.0 / XS_BLK)) * XS_BLK, (N_EXPERTS, LANES))
    start = jnp.dot(low_ref[...], npad.astype(bf16), preferred_element_type=f32)
    pos = start[:, :1] + rank
    first = jnp.where(below < 0.5, sel, 0.0)
    second = sel - first
    pos0 = jnp.sum(first * pos, axis=0, keepdims=True).astype(jnp.int32)
    pos1 = jnp.sum(second * pos, axis=0, keepdims=True).astype(jnp.int32)
    g0 = jnp.sum(first * gate, axis=0, keepdims=True)
    g1 = jnp.sum(second * gate, axis=0, keepdims=True)
    pos_ref[0:1, :] = pos0
    pos_ref[1:2, :] = pos1
    meta_v[0] = start.astype(jnp.int32)
    meta_v[1] = (npad * (1.0 / XS_BLK)).astype(jnp.int32)
    pltpu.sync_copy(meta_v, meta_s)

    r = lax.broadcasted_iota(jnp.int32, (XS_ROWS, ROW_TILE), 0)
    hit0 = r == pos0
    hit1 = r == pos1
    perm = jnp.where(hit0, 1.0, jnp.where(hit1, 1.0, 0.0)).astype(bf16)
    gate_sorted = jnp.sum(jnp.where(hit0, g0, jnp.where(hit1, g1, 0.0)), axis=1, keepdims=True)
    xs = jnp.dot(perm, xm_hi, preferred_element_type=f32)

    def block_copy(src_row, dst_row, sl, rows=XS_BLK):
        return pltpu.make_async_copy(xs_vmem.at[sl, pl.ds(pl.multiple_of(src_row, XS_BLK), rows), :],
                                     xs_hbm.at[pl.ds(pl.multiple_of(dst_row, XS_BLK), rows), :],
                                     sem.at[sl])

    def wait_blocks(count, sl):
        _drain(lambda rows: block_copy(0, 0, sl, rows), count)

    wait_blocks(nblk_smem[slot], slot)
    xs_vmem[slot, :, :PACK_W] = _pack_pairs(xs)
    xs_vmem[slot, :, PACK_W:] = lax.bitcast_convert_type(
        jnp.broadcast_to(gate_sorted, (XS_ROWS, LANES)), jnp.uint32)
    total = 0
    for e in range(N_EXPERTS):
        start_e = meta_s[0, e, 0]
        nblk_e = meta_s[1, e, 0]
        base = e * cap + cnt_smem[e]
        tab_ref[t, 3 * e] = base
        tab_ref[t, 3 * e + 1] = start_e
        tab_ref[t, 3 * e + 2] = nblk_e

        def body(i, carry):
            block_copy(start_e + i * XS_BLK, base + i * XS_BLK, slot).start()
            return carry

        lax.fori_loop(0, nblk_e, body, 0)
        cnt_smem[e] = cnt_smem[e] + nblk_e * XS_BLK
        total = total + nblk_e
    nblk_smem[slot] = total

    @pl.when(jnp.logical_not(has_next))
    def _():
        wait_blocks(nblk_smem[slot], slot)
        wait_blocks(nblk_smem[1 - slot], 1 - slot)
        for e in range(N_EXPERTS):
            cnt_ref[0, e] = cnt_smem[e]


def _dispatch_constants():
    t = np.arange(ROW_TILE)
    e = np.arange(N_EXPERTS)
    return ((t[:, None] < t[None, :]).astype(np.float32), (e[None, :] < e[:, None]).astype(np.float32))


def _out_proj(l, a, bmix, x, mod, w_out, ln_g, ln_b, w_router_t, b_router, tri, low):
    B, L, _ = x.shape
    half = A_WIDTH
    n_tiles = B * (L // ROW_TILE)
    cap = _expert_capacity(n_tiles)
    smem = pl.BlockSpec(memory_space=pltpu.SMEM)
    return pl.pallas_call(
        functools.partial(_out_kernel, cap=cap),
        grid=(B, L // ROW_TILE),
        in_specs=[_row_spec(A_WIDTH), _row_spec(B_WIDTH), _row_spec(D_MODEL), _mod_spec(),
                  pl.BlockSpec((None, half, D_MODEL), lambda b, j: (l, 0, 0)),
                  pl.BlockSpec((None, half, D_MODEL), lambda b, j: (l, 1, 0)),
                  _vec_spec(l), _vec_spec(l),
                  pl.BlockSpec((3 * N_EXPERTS, D_MODEL), lambda b, j: (0, 0)),
                  pl.BlockSpec((N_EXPERTS, 1), lambda b, j: (0, 0)),
                  pl.BlockSpec((ROW_TILE, ROW_TILE), lambda b, j: (0, 0)),
                  pl.BlockSpec((N_EXPERTS, N_EXPERTS), lambda b, j: (0, 0))],
        out_specs=[_row_spec(D_MODEL), _pos_spec(), smem, smem, pl.BlockSpec(memory_space=pl.ANY)],
        out_shape=[jax.ShapeDtypeStruct(x.shape, f32), jax.ShapeDtypeStruct((B, 2, L), jnp.int32),
                   jax.ShapeDtypeStruct((n_tiles, 3 * N_EXPERTS), jnp.int32),
                   jax.ShapeDtypeStruct((1, N_EXPERTS), jnp.int32),
                   jax.ShapeDtypeStruct((N_EXPERTS * cap, XS_W), jnp.uint32)],
        scratch_shapes=[pltpu.VMEM((N_EXPERTS, ROW_TILE), f32), pltpu.VMEM((N_EXPERTS, ROW_TILE), f32),
                        pltpu.VMEM((2, N_EXPERTS, LANES), jnp.int32), pltpu.SMEM((2, N_EXPERTS, LANES), jnp.int32),
                        pltpu.VMEM((2, XS_ROWS, XS_W), jnp.uint32), pltpu.SemaphoreType.DMA((2,)),
                        pltpu.SMEM((N_EXPERTS,), jnp.int32), pltpu.SMEM((2,), jnp.int32)],
        compiler_params=_params("arbitrary", "arbitrary"),
        name="out_proj",
    )(a, bmix, x, mod, w_out, w_out, ln_g, ln_b, w_router_t, b_router, tri, low)


def _expert_capacity(n_tiles):
    rows = n_tiles * (ROW_TILE + XS_BLK)
    return -(-rows // EXP_TILE) * EXP_TILE


def _moe_kernel(te_ref, tb_ref, tv_ref, xs_ref, wg_ref, wu_ref, wd_ref, ys_ref):
    i = pl.program_id(0)
    valid = tv_ref[i]

    @pl.when(valid > 0)
    def _():
        w32 = xs_ref[...]
        ok = lax.broadcasted_iota(jnp.int32, (EXP_TILE, 1), 0) < valid
        x = _unpack_pairs(jnp.where(ok, w32[:, :PACK_W], jnp.uint32(0)))
        gate = jnp.where(ok, lax.bitcast_convert_type(w32[:, PACK_W:PACK_W + 1], f32), 0.0)
        hg = jnp.dot(x, wg_ref[...].astype(bf16), preferred_element_type=f32)
        hu = jnp.dot(x, wu_ref[...].astype(bf16), preferred_element_type=f32)
        hid = hg * _sigmoid(hg) * hu * gate
        y = jnp.dot(hid.astype(bf16), wd_ref[...].astype(bf16), preferred_element_type=f32)
        ys_ref[...] = _pack_pairs(y.astype(bf16).astype(f32))


def _expert_tiles(cnt, cap, n_steps):
    cnt = cnt.reshape(N_EXPERTS)
    tiles = (cnt + (EXP_TILE - 1)) // EXP_TILE
    ends = jnp.cumsum(tiles)
    n_valid = ends[-1]
    i = jnp.clip(jnp.arange(n_steps, dtype=jnp.int32), 0, jnp.maximum(n_valid - 1, 0))
    e = jnp.sum((i[:, None] >= ends[None, :]).astype(jnp.int32), axis=1)
    local = i - (ends - tiles)[e]
    rows = jnp.clip(cnt[e] - local * EXP_TILE, 0, EXP_TILE)
    rows = jnp.where(jnp.arange(n_steps) < n_valid, rows, 0)
    return e.astype(jnp.int32), (e * (cap // EXP_TILE) + local).astype(jnp.int32), rows.astype(jnp.int32)


def _moe(l, xs, cnt, n_tiles, w_gate, w_up, w_down):
    cap = _expert_capacity(n_tiles)
    n_steps = n_tiles * XS_ROWS // EXP_TILE + N_EXPERTS
    te, tb, tv = _expert_tiles(cnt, cap, n_steps)
    grid_spec = pltpu.PrefetchScalarGridSpec(
        num_scalar_prefetch=3,
        grid=(n_steps,),
        in_specs=[pl.BlockSpec((EXP_TILE, XS_W), lambda i, te, tb, tv: (tb[i], 0)),
                  pl.BlockSpec((None, None, D_MODEL, D_EXPERT), lambda i, te, tb, tv: (l, te[i], 0, 0)),
                  pl.BlockSpec((None, None, D_MODEL, D_EXPERT), lambda i, te, tb, tv: (l, te[i], 0, 0)),
                  pl.BlockSpec((None, None, D_EXPERT, D_MODEL), lambda i, te, tb, tv: (l, te[i], 0, 0))],
        out_specs=pl.BlockSpec((EXP_TILE, PACK_W), lambda i, te, tb, tv: (tb[i], 0)),
    )
    return pl.pallas_call(
        _moe_kernel,
        grid_spec=grid_spec,
        out_shape=jax.ShapeDtypeStruct((N_EXPERTS * cap, PACK_W), jnp.uint32),
        compiler_params=_params("arbitrary"),
        name="moe",
    )(te, tb, tv, xs, w_gate, w_up, w_down)


def _final_kernel(tab_ref, x_ref, pos_ref, ys_hbm, mod_ref, g_ref, b_ref, o_ref, y_vmem, sem, *, j_off):
    f = _combine_experts(tab_ref, pos_ref, ys_hbm, y_vmem, sem, j_off)
    o_ref[...] = _layer_norm(ALPHA * x_ref[...] + mod_ref[5:6, :] * f, g_ref[...], b_ref[...])


def _final_ln(x, moe, mod, ln_g, ln_b):
    B, L, _ = x.shape
    seq = L - CTX_LEN
    skip = CTX_LEN // ROW_TILE
    tab, pos, ys = moe
    lat = pl.BlockSpec((None, ROW_TILE, D_MODEL), lambda b, j: (b, j + skip, 0))
    return pl.pallas_call(
        functools.partial(_final_kernel, j_off=skip),
        grid=(B, seq // ROW_TILE),
        in_specs=[pl.BlockSpec(memory_space=pltpu.SMEM), lat, _pos_spec(skip), pl.BlockSpec(memory_space=pl.ANY),
                  pl.BlockSpec((None, None, 6, D_MODEL), lambda b, j: (b, 1, 0, 0)),
                  _vec_spec(DEPTH - 1), _vec_spec(DEPTH - 1)],
        out_specs=_row_spec(D_MODEL),
        out_shape=jax.ShapeDtypeStruct((B, seq, D_MODEL), f32),
        scratch_shapes=_combine_scratch(),
        compiler_params=_params("arbitrary", "arbitrary"),
        name="final_ln",
    )(tab, x, pos, ys, mod, ln_g, ln_b)


def kernel(x, c, ctx, c_ctx, w_ada, b_ada, w_in, lb_logits, a_norm_g, rpb, w_out, ln1_g, ln1_b,
           w_router, b_router, w_gate, w_up, w_down, ln2_g, ln2_b):
    B, seq, D = x.shape
    L = CTX_LEN + seq
    xs = jnp.concatenate([ctx, x], axis=1)

    cc = jnp.concatenate([c, c_ctx[None, :], jnp.zeros((16 - B - 1, D), f32)], axis=0)
    mods = _ada_mods(cc, w_ada, b_ada)
    lat_mod = mods[:, :B].reshape(DEPTH, B, 1, 6, D)
    ctx_mod = jnp.broadcast_to(mods[:, B].reshape(DEPTH, 1, 1, 6, D), (DEPTH, B, 1, 6, D))
    mod_all = jnp.concatenate([ctx_mod, lat_mod], axis=2)

    sm = jax.nn.softmax(lb_logits.astype(f32), axis=1)
    lower = jnp.cumsum(sm, axis=1) - sm[:, :1]

    tri_np, own_np, sgn_np = _scan_constants()
    scan_consts = (jnp.asarray(tri_np, bf16), jnp.asarray(own_np), jnp.asarray(sgn_np))
    cos_np, sin_np = _rope_tables(seq)
    cos_t, sin_t = jnp.asarray(cos_np), jnp.asarray(sin_np)
    mask = jnp.asarray(_window_mask())
    bias_t = _bias_tables(rpb.astype(f32) * LOG2E)

    w_in_b = w_in.astype(bf16)
    w_out_b = w_out.astype(bf16)
    wr_hi = w_router.T.astype(bf16)
    wr_res = w_router.T - wr_hi.astype(f32)
    wr_mid = wr_res.astype(bf16)
    wr_lo = (wr_res - wr_mid.astype(f32)).astype(bf16)
    w_router_t = jnp.concatenate([wr_hi, wr_mid, wr_lo], axis=0)
    b_router_c = b_router.reshape(N_EXPERTS, 1)
    ln1_g3, ln1_b3 = ln1_g.reshape(DEPTH, 1, D), ln1_b.reshape(DEPTH, 1, D)
    ln2_g3, ln2_b3 = ln2_g.reshape(DEPTH, 1, D), ln2_b.reshape(DEPTH, 1, D)

    tri_np, low_np = _dispatch_constants()
    tri, low = jnp.asarray(tri_np, bf16), jnp.asarray(low_np, bf16)
    n_tiles = B * (L // ROW_TILE)

    moe = None
    for l in range(DEPTH):
        pmod = mod_all[l - 1] if l > 0 else None
        xs, u = _in_proj(l, xs, moe, pmod, ln2_g3, ln2_b3, mod_all[l], w_in_b)
        a = _hgrn(u, lower[0, l], lower[1, l], a_norm_g[l], scan_consts)
        bmix = _na(l, u, cos_t, sin_t, bias_t, mask)
        xs, pos, tab, cnt, x_sorted = _out_proj(l, a, bmix, xs, mod_all[l], w_out_b, ln1_g3, ln1_b3,
                                                w_router_t, b_router_c, tri, low)
        y_sorted = _moe(l, x_sorted, cnt, n_tiles, w_gate, w_up, w_down)
        moe = (tab, pos, y_sorted)
    return _final_ln(xs, moe, mod_all[DEPTH - 1], ln2_g3, ln2_b3)
```

```python
import functools

import numpy as np
import jax
import jax.numpy as jnp
from jax import lax
from jax.experimental import pallas as pl
from jax.experimental.pallas import tpu as pltpu

D_MODEL = 1024
DEPTH = 4
GRID_W = 64
CTX_LEN = 256
A_HEADS = 4
A_DK = 128
A_WIDTH = A_HEADS * A_DK
B_HEADS = 8
B_HD = 64
B_WIDTH = B_HEADS * B_HD
D_IN = 5 * A_WIDTH + 3 * B_WIDTH
NA_ROWS = 8
NA_COLS = 16
ROPE_BASE = 10000.0
N_EXPERTS = 16
N_GROUPS = 4
EXPERTS_PER_GROUP = N_EXPERTS // N_GROUPS
D_EXPERT = 512
ALPHA = (2 * DEPTH) ** 0.25
LN_EPS = 1e-5
RMS_EPS = 1e-6
F_FLOOR = 1e-6
MASK_VALUE = -1e30
LOG2E = 1.4426950408889634

LANES = 128
ROW_TILE = 256
XS_BLK = 8
XS_ROWS = 2 * ROW_TILE + N_EXPERTS * XS_BLK
PACK_W = D_MODEL // 2
XS_W = PACK_W + LANES
DRAIN_GROUP = 8
EXP_TILE = 1024
MOE_CHUNK = 256
SCAN_CHUNK = 128
SCAN_LEVELS = (64, 32, 16, 8, 4)
SCAN_BASE = 4
NA_GROUP = 4
VMEM_LIMIT = 56 * 1024 * 1024

f32 = jnp.float32
bf16 = jnp.bfloat16

_NT = (((1,), (1,)), ((), ()))
_TN = (((0,), (0,)), ((), ()))


def _params(*sem):
    return pltpu.CompilerParams(dimension_semantics=sem, vmem_limit_bytes=VMEM_LIMIT)


def _sigmoid(z, sign=1.0):
    return 1.0 / (1.0 + jnp.exp2(z * (-sign * LOG2E)))


def _layer_norm(h, g, b):
    mu = jnp.mean(h, axis=-1, keepdims=True)
    d = h - mu
    var = jnp.mean(d * d, axis=-1, keepdims=True)
    return d * lax.rsqrt(var + LN_EPS) * g + b


def _ada_kernel(c_ref, w_ref, b_ref, o_ref):
    c = c_ref[...]
    o_ref[...] = jnp.dot(c * _sigmoid(c), w_ref[...], preferred_element_type=f32,
                         precision=lax.Precision.HIGHEST) + b_ref[...]


def _ada_mods(cc, w_ada, b_ada):
    n_col = 4
    tn = 6 * D_MODEL // n_col
    return pl.pallas_call(
        _ada_kernel,
        grid=(DEPTH, n_col),
        in_specs=[
            pl.BlockSpec((16, D_MODEL), lambda l, j: (0, 0)),
            pl.BlockSpec((None, D_MODEL, tn), lambda l, j: (l, 0, j)),
            pl.BlockSpec((None, 1, tn), lambda l, j: (l, 0, j)),
        ],
        out_specs=pl.BlockSpec((None, 16, tn), lambda l, j: (l, 0, j)),
        out_shape=jax.ShapeDtypeStruct((DEPTH, 16, 6 * D_MODEL), f32),
        compiler_params=_params("parallel", "parallel"),
        name="ada_mods",
    )(cc, w_ada, b_ada.reshape(DEPTH, 1, 6 * D_MODEL))


def _pack_pairs(x):
    bits = lax.bitcast_convert_type(x, jnp.uint32)
    half = x.shape[1] // 2
    return (bits[:, :half] >> 16) | (bits[:, half:] & jnp.uint32(0xFFFF0000))


def _unpack_pairs(w):
    lo = lax.bitcast_convert_type(w << 16, f32)
    hi = lax.bitcast_convert_type(w & jnp.uint32(0xFFFF0000), f32)
    return jnp.concatenate([lo, hi], axis=1).astype(bf16)


def _drain(copy_of_rows, n_blocks):
    def grouped(i, carry):
        copy_of_rows(DRAIN_GROUP * XS_BLK).wait()
        return carry

    def single(i, carry):
        copy_of_rows(XS_BLK).wait()
        return carry

    lax.fori_loop(0, n_blocks // DRAIN_GROUP, grouped, 0)
    lax.fori_loop(0, lax.rem(n_blocks, DRAIN_GROUP), single, 0)


def _tile_id(j_off):
    b, j = pl.program_id(0), pl.program_id(1)
    nj = pl.num_programs(1)
    tiles_per_batch = nj + j_off
    t = b * tiles_per_batch + j + j_off
    last_j = j == nj - 1
    t_next = jnp.where(last_j, (b + 1) * tiles_per_batch + j_off, t + 1)
    has_next = jnp.logical_not(jnp.logical_and(last_j, b == pl.num_programs(0) - 1))
    return t, t_next, has_next, jnp.logical_and(b == 0, j == 0)


def _combine_experts(tab_ref, pos_ref, ys_hbm, y_vmem, sem, j_off):
    t, t_next, has_next, is_first = _tile_id(j_off)
    step = pl.program_id(0) * pl.num_programs(1) + pl.program_id(1)
    slot = lax.rem(step, 2)

    def block_copy(src_row, dst_row, sl, rows=XS_BLK):
        return pltpu.make_async_copy(ys_hbm.at[pl.ds(pl.multiple_of(src_row, XS_BLK), rows), :],
                                     y_vmem.at[sl, pl.ds(pl.multiple_of(dst_row, XS_BLK), rows), :],
                                     sem.at[sl])

    def fetch(tile, sl):
        y_vmem[sl] = jnp.zeros((XS_ROWS, PACK_W), jnp.uint32)
        for e in range(N_EXPERTS):
            base, start, nblk = tab_ref[tile, 3 * e], tab_ref[tile, 3 * e + 1], tab_ref[tile, 3 * e + 2]

            def body(i, carry):
                block_copy(base + i * XS_BLK, start + i * XS_BLK, sl).start()
                return carry

            lax.fori_loop(0, nblk, body, 0)

    def wait_all(tile, sl):
        total = tab_ref[tile, 2]
        for e in range(1, N_EXPERTS):
            total = total + tab_ref[tile, 3 * e + 2]
        _drain(lambda rows: block_copy(0, 0, sl, rows), total)

    @pl.when(is_first)
    def _():
        fetch(t, slot)

    @pl.when(has_next)
    def _():
        fetch(t_next, 1 - slot)

    wait_all(t, slot)
    y = _unpack_pairs(y_vmem[slot])
    r = lax.broadcasted_iota(jnp.int32, (XS_ROWS, ROW_TILE), 0)
    perm = jnp.where(r == pos_ref[0:1, :], 1.0, jnp.where(r == pos_ref[1:2, :], 1.0, 0.0)).astype(bf16)
    return lax.dot_general(perm, y, _TN, preferred_element_type=f32)


def _in_kernel(*refs, with_ln):
    if with_ln:
        (tab_ref, x_ref, pos_ref, ys_hbm, pmod_ref, g_ref, b_ref, mod_ref, w_ref,
         xo_ref, u_ref, y_vmem, sem) = refs
        f = _combine_experts(tab_ref, pos_ref, ys_hbm, y_vmem, sem, 0)
        h = ALPHA * x_ref[...] + pmod_ref[5:6, :] * f
        x = _layer_norm(h, g_ref[...], b_ref[...])
        xo_ref[...] = x
    else:
        x_ref, mod_ref, w_ref, u_ref = refs
        x = x_ref[...]
    xm = x * (1.0 + mod_ref[1:2, :]) + mod_ref[0:1, :]
    u_ref[...] = jnp.dot(xm.astype(bf16), w_ref[...], preferred_element_type=f32)


def _mod_spec():
    return pl.BlockSpec((None, None, 6, D_MODEL), lambda b, j: (b, jnp.minimum(j, 1), 0, 0))


def _row_spec(width):
    return pl.BlockSpec((None, ROW_TILE, width), lambda b, j: (b, j, 0))


def _vec_spec(l):
    return pl.BlockSpec((None, 1, D_MODEL), lambda b, j: (l, 0, 0))


def _pos_spec(j_off=0):
    return pl.BlockSpec((None, 2, ROW_TILE), lambda b, j: (b, 0, j + j_off))


def _combine_scratch():
    return [pltpu.VMEM((2, XS_ROWS, PACK_W), jnp.uint32), pltpu.SemaphoreType.DMA((2,))]


def _in_proj(l, x, moe, pmod, ln_g, ln_b, mod, w_in):
    B, L, _ = x.shape
    w_spec = pl.BlockSpec((None, D_MODEL, D_IN), lambda b, j: (l, 0, 0))
    u_shape = jax.ShapeDtypeStruct((B, L, D_IN), f32)
    if moe is None:
        u = pl.pallas_call(
            functools.partial(_in_kernel, with_ln=False),
            grid=(B, L // ROW_TILE),
            in_specs=[_row_spec(D_MODEL), _mod_spec(), w_spec],
            out_specs=_row_spec(D_IN),
            out_shape=u_shape,
            compiler_params=_params("parallel", "parallel"),
            name="in_proj0",
        )(x, mod, w_in)
        return x, u
    tab, pos, ys = moe
    return pl.pallas_call(
        functools.partial(_in_kernel, with_ln=True),
        grid=(B, L // ROW_TILE),
        in_specs=[pl.BlockSpec(memory_space=pltpu.SMEM), _row_spec(D_MODEL), _pos_spec(),
                  pl.BlockSpec(memory_space=pl.ANY), _mod_spec(), _vec_spec(l - 1), _vec_spec(l - 1),
                  _mod_spec(), w_spec],
        out_specs=[_row_spec(D_MODEL), _row_spec(D_IN)],
        out_shape=[jax.ShapeDtypeStruct(x.shape, f32), u_shape],
        scratch_shapes=_combine_scratch(),
        compiler_params=_params("arbitrary", "arbitrary"),
        name="in_proj",
    )(tab, x, pos, ys, pmod, ln_g, ln_b, mod, w_in)


def _scan_constants():
    C = SCAN_CHUNK
    p = np.arange(C)
    tri = (p[None, :] <= p[:, None]).astype(np.float32)
    owner = np.full((C, C), -1, np.int32)
    signs = []
    for li, m in enumerate(SCAN_LEVELS):
        same = (p[:, None] // (2 * m)) == (p[None, :] // (2 * m))
        owner[same & ((p[:, None] // m) % 2 == 1) & ((p[None, :] // m) % 2 == 0)] = li
        signs.append(np.where((p // m) % 2 == 1, 1.0, -1.0))
    owner[((p[:, None] // SCAN_BASE) == (p[None, :] // SCAN_BASE)) & (p[None, :] <= p[:, None])] = len(SCAN_LEVELS)
    sgn = np.broadcast_to(np.stack(signs)[:, :, None], (len(SCAN_LEVELS), C, LANES)).astype(np.float32)
    return (np.stack([tri, tri[::-1, ::-1]]), np.stack([owner, owner[::-1, ::-1]]),
            np.stack([sgn, sgn[:, ::-1]]))


def _hgrn_kernel(q_ref, ff_ref, fb_ref, v_ref, g_ref, lbf_ref, lbb_ref, ng_ref, t_ref, own_ref, sgn_ref,
                 a_ref, of_scr, ob_scr, *, n_ctx, n_all):
    C = SCAN_CHUNK
    n_lv = len(SCAN_LEVELS)


    def gates(rows, z_ref, lb, d):
        z = z_ref[rows, :]
        lf = jnp.log2(jnp.maximum(lb + (1.0 - lb) * _sigmoid(z), F_FLOOR))
        k = (1.0 - lb) * _sigmoid(z, -1.0)
        qz = q_ref[rows, :]
        q = qz * _sigmoid(qz)
        v = v_ref[rows, :].astype(bf16)
        hi = lf.astype(bf16)
        r1 = lf - hi.astype(f32)
        mid = r1.astype(bf16)
        lo = (r1 - mid.astype(f32)).astype(bf16)
        d3 = jnp.dot(t_ref[d], jnp.concatenate([hi, mid, lo], axis=1), preferred_element_type=f32)
        cum = d3[:, :LANES] + d3[:, LANES:2 * LANES] + d3[:, 2 * LANES:]
        return q, k, v, cum

    def intra(q, k, cum, d):
        last = cum[C - 1:C, :] if d == 0 else cum[0:1, :]

        def rel(block, ref_rows):
            blocks = cum.reshape(C // block, block, LANES)
            if len(ref_rows) == 1:
                ref = blocks[:, ref_rows[0]:ref_rows[0] + 1, :]
            else:
                first = lax.broadcasted_iota(jnp.int32, (1, block, 1), 1) < block // 2
                ref = jnp.where(first, blocks[:, ref_rows[0]:ref_rows[0] + 1, :],
                                blocks[:, ref_rows[1]:ref_rows[1] + 1, :])
            return (blocks - ref).reshape(C, LANES)

        def to_boundary(m):
            blocks = cum.reshape(C // (2 * m), 2 * m, LANES)
            lo, hi = blocks[:, :m, :], blocks[:, m:, :]
            if d == 0:
                ref = lo[:, m - 1:m, :]
                parts = (ref - lo, hi - ref)
            else:
                ref = hi[:, 0:1, :]
                parts = (lo - ref, ref - hi)
            return jnp.concatenate(parts, axis=1).reshape(C, LANES)

        q_in = (q * jnp.exp2(cum)).astype(bf16)
        own = own_ref[d]
        scores = jnp.zeros((C, C), f32)
        for li in range(n_lv + 1):
            m = SCAN_LEVELS[li] if li < n_lv else 0
            if m and m % 8 == 0:
                def halves(x):
                    blocks = x.reshape(C // (2 * m), 2 * m, x.shape[-1])
                    return blocks[:, :m, :], blocks[:, m:, :]

                def join(lo, hi):
                    return jnp.concatenate([lo, hi], axis=1).reshape(C, lo.shape[-1])

                (q_lo, q_hi), (k_lo, k_hi) = halves(q), halves(k)
                mixed = join(k_lo, q_hi) if d == 0 else join(q_lo, k_hi)
                z = (mixed * jnp.exp2(to_boundary(m))).astype(bf16)
                a = lax.dot_general(z, z, _NT, preferred_element_type=f32)
                (a_lo, a_hi), (s_lo, s_hi), (o_lo, o_hi) = halves(a), halves(scores), halves(own)
                if d == 0:
                    scores = join(s_lo, jnp.where(o_hi == li, a_hi, s_hi))
                else:
                    scores = join(jnp.where(o_lo == li, a_lo, s_lo), s_hi)
                continue
            if m:
                wq = jnp.exp2(rel(2 * m, (m - 1,) if d == 0 else (m,)) * sgn_ref[d, li])
                wk = wq
            else:
                nb = SCAN_BASE
                dl = rel(2 * nb, (0, nb) if d == 0 else (nb - 1, 2 * nb - 1))
                wq = jnp.exp2(dl)
                wk = jnp.exp2(-dl)
            a = lax.dot_general((q * wq).astype(bf16), (k * wk).astype(bf16), _NT,
                                preferred_element_type=f32)
            scores = jnp.where(own == li, a, scores)
        kdec = (k * jnp.exp2(last - cum)).astype(bf16)
        return q_in, scores.astype(bf16), kdec, jnp.exp2(last)

    def finish(pending, S, o_scr):
        q_in, scores, kdec, decay, v, row0 = pending
        o = (lax.dot_general(q_in, S.astype(bf16), _NT, preferred_element_type=f32)
             + jnp.dot(scores, v, preferred_element_type=f32))
        o_scr[pl.ds(pl.multiple_of(row0, C), C), :] = o
        return S * decay + lax.dot_general(v, kdec, _TN, preferred_element_type=f32)

    lbf = lbf_ref[...]
    lbb = lbb_ref[...]

    def bwd_chunk(i):
        return jnp.where(i < n_ctx, n_ctx - 1 - i, n_all - 1 - (i - n_ctx))

    def scan_body(i, carry):
        s_f, s_b, pend_f, pend_b = carry
        row_f = i * C
        row_b = bwd_chunk(i) * C
        s_f = finish(pend_f, s_f, of_scr)
        s_b = finish(pend_b, s_b, ob_scr)
        qf, kf, vf, cum_f = gates(pl.ds(pl.multiple_of(row_f, C), C), ff_ref, lbf, 0)
        qb, kb, vb, cum_b = gates(pl.ds(pl.multiple_of(row_b, C), C), fb_ref, lbb, 1)
        pend_f = intra(qf, kf, cum_f, 0) + (vf, row_f)
        pend_b = intra(qb, kb, cum_b, 1) + (vb, row_b)
        return s_f, s_b, pend_f, pend_b

    def idle(row0):
        zc = jnp.zeros((C, LANES), bf16)
        return (zc, jnp.zeros((C, C), bf16), zc, jnp.ones((1, LANES), f32), zc, row0)

    zero = jnp.zeros((A_DK, A_DK), f32)
    init = (zero, zero, idle(jnp.int32(0)), idle(bwd_chunk(jnp.int32(0)) * C))
    s_f, s_b, pend_f, pend_b = lax.fori_loop(0, n_all, scan_body, init, unroll=3)
    finish(pend_f, s_f, of_scr)
    finish(pend_b, s_b, ob_scr)

    ng = ng_ref[...]

    def readout_body(i, carry):
        rows = pl.ds(pl.multiple_of(i * ROW_TILE, ROW_TILE), ROW_TILE)
        o = of_scr[rows, :] + ob_scr[rows, :]
        o = o * lax.rsqrt(jnp.mean(o * o, axis=-1, keepdims=True) + RMS_EPS) * ng
        gz = g_ref[rows, :]
        a_ref[rows, :] = (o * (gz * _sigmoid(gz))).astype(a_ref.dtype)
        return carry

    lax.fori_loop(0, n_all * C // ROW_TILE, readout_body, 0)


def _hgrn(u, lb_f, lb_b, norm_g, scan_consts):
    B, L, _ = u.shape
    hb = A_WIDTH // LANES

    def col(sec):
        return pl.BlockSpec((None, L, LANES), lambda b, h: (b, 0, sec * hb + h))

    def whole(arr):
        return pl.BlockSpec(arr.shape, lambda b, h: (0,) * arr.ndim)

    lb_spec = pl.BlockSpec((None, 1, LANES), lambda b, h: (h, 0, 0))
    return pl.pallas_call(
        functools.partial(_hgrn_kernel, n_ctx=CTX_LEN // SCAN_CHUNK, n_all=L // SCAN_CHUNK),
        grid=(B, A_HEADS),
        in_specs=[col(0), col(1), col(2), col(3), col(4), lb_spec, lb_spec,
                  pl.BlockSpec((1, LANES), lambda b, h: (0, 0))] + [whole(a) for a in scan_consts],
        out_specs=pl.BlockSpec((None, L, LANES), lambda b, h: (b, 0, h)),
        out_shape=jax.ShapeDtypeStruct((B, L, A_WIDTH), bf16),
        scratch_shapes=[pltpu.VMEM((L, LANES), f32), pltpu.VMEM((L, LANES), f32)],
        compiler_params=_params("parallel", "parallel"),
        name="hgrn2",
    )(u, u, u, u, u, lb_f.reshape(A_HEADS, 1, LANES), lb_b.reshape(A_HEADS, 1, LANES),
      norm_g.reshape(1, LANES), *scan_consts)


def _rope_tables(seq):
    t = np.arange(seq)
    pos_r = (t // GRID_W).astype(np.float32)
    pos_c = (t % GRID_W).astype(np.float32)
    quarter = B_HD // 4
    inv = (ROPE_BASE ** (-np.arange(quarter, dtype=np.float32) / quarter)).astype(np.float32)
    lane = np.arange(LANES)
    in_head = lane % B_HD
    pos = np.where((in_head < B_HD // 2)[None, :], pos_r[:, None], pos_c[:, None])
    ang = (pos * inv[lane % quarter][None, :]).astype(np.float32)
    sign = np.where((lane % (2 * quarter)) < quarter, -1.0, 1.0)
    return np.cos(ang).astype(np.float32), (np.sin(ang) * sign[None, :]).astype(np.float32)


def _window_mask():
    c = np.arange(GRID_W)
    cs = np.clip(c - NA_COLS // 2, 0, GRID_W - NA_COLS)
    in_win = (c[None, :] >= cs[:, None]) & (c[None, :] < cs[:, None] + NA_COLS)
    return np.tile(in_win, (1, NA_ROWS)).astype(np.float32)


def _bias_tables(rpb):
    c = np.arange(GRID_W)
    dc = np.clip(c[None, :] - c[:, None], -(NA_COLS - 1), NA_COLS - 1) + (NA_COLS - 1)
    pick = (np.arange(2 * NA_COLS - 1)[:, None, None] == dc[None]).astype(np.float32)
    cols = jnp.einsum('lhdk,kcx->lhdcx', rpb, jnp.asarray(pick), precision=lax.Precision.HIGHEST)
    t = jnp.stack([cols[:, :, o:o + NA_ROWS] for o in range(NA_ROWS)], axis=2)
    t = t.transpose(0, 1, 2, 4, 3, 5)
    return t.reshape(DEPTH, B_HEADS, NA_ROWS, GRID_W, NA_ROWS * GRID_W)


def _na_kernel(q_ref, k_ref, v_ref, cos_ref, sin_ref, bias_ref, mask_ref, o_ref,
               qr_scr, kr_scr, qp_scr, kc_scr, v_scr, *, seq, rows_grid):
    hd = B_HD
    scale = hd ** -0.5 * LOG2E
    n_win = NA_ROWS * GRID_W
    lane = lax.broadcasted_iota(jnp.int32, (ROW_TILE, LANES), 1)
    first_half = (lane % (hd // 2)) < (hd // 4)
    head0 = lane < hd

    def swap_halves(x):
        return jnp.where(first_half, pltpu.roll(x, LANES - hd // 4, axis=1), pltpu.roll(x, hd // 4, axis=1))

    def store_per_head(dst, rows, x):
        dst[0, rows, :] = jnp.where(head0, x, 0.0).astype(bf16)
        dst[1, rows, :] = jnp.where(head0, 0.0, x).astype(bf16)

    ctx_rows = pl.ds(0, CTX_LEN)
    store_per_head(qp_scr, ctx_rows, q_ref[ctx_rows, :] * scale)
    kc_scr[...] = k_ref[ctx_rows, :].astype(bf16)
    v_scr[ctx_rows, :] = v_ref[ctx_rows, :].astype(bf16)

    def prep(i, carry):
        lrows = pl.ds(pl.multiple_of(i * ROW_TILE, ROW_TILE), ROW_TILE)
        rows = pl.ds(pl.multiple_of(CTX_LEN + i * ROW_TILE, ROW_TILE), ROW_TILE)
        q = q_ref[rows, :] * scale
        k = k_ref[rows, :]
        cos = cos_ref[lrows, :]
        sin = sin_ref[lrows, :]
        store_per_head(qp_scr, rows, q)
        v_scr[rows, :] = v_ref[rows, :].astype(bf16)
        store_per_head(qr_scr, lrows, q * cos + swap_halves(q) * sin)
        kr_scr[lrows, :] = (k * cos + swap_halves(k) * sin).astype(bf16)
        return carry

    lax.fori_loop(0, seq // ROW_TILE, prep, 0)

    def pick_heads(o0, o1):
        keep0 = lax.broadcasted_iota(jnp.int32, o0.shape, 1) < hd
        return jnp.where(keep0, o0, o1)

    outs = []
    for h in range(2):
        s = lax.dot_general(qp_scr[h, ctx_rows, :], kc_scr[...], _NT, preferred_element_type=f32)
        e = jnp.exp2(s - jnp.max(s, axis=-1, keepdims=True))
        o = jnp.dot(e.astype(bf16), v_scr[ctx_rows, :], preferred_element_type=f32)
        outs.append(o / jnp.sum(e, axis=-1, keepdims=True))
    o_ref[ctx_rows, :] = pick_heads(*outs).astype(o_ref.dtype)

    def rows_body(it, carry):
        chains = []
        for g in range(NA_GROUP):
            r = it * NA_GROUP + g
            rs = jnp.clip(r - NA_ROWS // 2, 0, rows_grid - NA_ROWS)
            for h in range(2):
                chains.append(dict(
                    h=h, off=rs - r + (NA_ROWS - 1),
                    qrows=pl.ds(pl.multiple_of(r * GRID_W, GRID_W), GRID_W),
                    krows=pl.ds(pl.multiple_of(rs * GRID_W, GRID_W), n_win),
                    arows=pl.ds(pl.multiple_of(CTX_LEN + r * GRID_W, GRID_W), GRID_W),
                    vrows=pl.ds(pl.multiple_of(CTX_LEN + rs * GRID_W, GRID_W), n_win)))
        for c in chains:
            h = c["h"]
            c["s_lat"] = lax.dot_general(qr_scr[h, c["qrows"], :], kr_scr[c["krows"], :], _NT,
                                         preferred_element_type=f32)
            c["s_ctx"] = lax.dot_general(qp_scr[h, c["arows"], :], kc_scr[...], _NT, preferred_element_type=f32)
        for c in chains:
            s_lat = jnp.where(mask_ref[...] > 0.5, c["s_lat"] + bias_ref[c["h"], c["off"]], MASK_VALUE)
            m = jnp.maximum(jnp.max(s_lat, axis=-1, keepdims=True), jnp.max(c["s_ctx"], axis=-1, keepdims=True))
            e_lat = jnp.exp2(s_lat - m)
            e_ctx = jnp.exp2(c["s_ctx"] - m)
            c["den"] = jnp.sum(e_lat, axis=-1, keepdims=True) + jnp.sum(e_ctx, axis=-1, keepdims=True)
            c["e_lat"] = e_lat.astype(bf16)
            c["e_ctx"] = e_ctx.astype(bf16)
        for c in chains:
            c["o"] = (jnp.dot(c["e_lat"], v_scr[c["vrows"], :], preferred_element_type=f32)
                      + jnp.dot(c["e_ctx"], v_scr[ctx_rows, :], preferred_element_type=f32))
        for g in range(NA_GROUP):
            c0, c1 = chains[2 * g:2 * g + 2]
            o_ref[c0["arows"], :] = pick_heads(c0["o"] / c0["den"], c1["o"] / c1["den"]).astype(o_ref.dtype)
        return carry

    lax.fori_loop(0, rows_grid // NA_GROUP, rows_body, 0)


def _na(l, u, cos_t, sin_t, bias_t, mask):
    B, L, _ = u.shape
    seq = L - CTX_LEN
    first = 5 * A_WIDTH // LANES
    pairs = B_WIDTH // LANES

    def col(sec):
        return pl.BlockSpec((None, L, LANES), lambda b, h: (b, 0, first + sec * pairs + h))

    n_win = NA_ROWS * GRID_W
    return pl.pallas_call(
        functools.partial(_na_kernel, seq=seq, rows_grid=seq // GRID_W),
        grid=(B, pairs),
        in_specs=[col(0), col(1), col(2),
                  pl.BlockSpec((seq, LANES), lambda b, h: (0, 0)),
                  pl.BlockSpec((seq, LANES), lambda b, h: (0, 0)),
                  pl.BlockSpec((None, 2, NA_ROWS, GRID_W, n_win), lambda b, h: (l, h, 0, 0, 0)),
                  pl.BlockSpec((GRID_W, n_win), lambda b, h: (0, 0))],
        out_specs=pl.BlockSpec((None, L, LANES), lambda b, h: (b, 0, h)),
        out_shape=jax.ShapeDtypeStruct((B, L, B_WIDTH), bf16),
        scratch_shapes=[pltpu.VMEM((2, seq, LANES), bf16), pltpu.VMEM((seq, LANES), bf16),
                        pltpu.VMEM((2, L, LANES), bf16), pltpu.VMEM((CTX_LEN, LANES), bf16),
                        pltpu.VMEM((L, LANES), bf16)],
        compiler_params=_params("parallel", "parallel"),
        name="na_attn",
    )(u, u, u, cos_t, sin_t, bias_t, mask)


def _route(logits, b_router, sel_ref, gate_ref):
    aff = _sigmoid(logits)
    sel = aff + b_router
    s = [sel[e:e + 1, :] for e in range(N_EXPERTS)]
    one = lambda cond: jnp.where(cond, 1.0, 0.0)
    scores = []
    for g in range(N_GROUPS):
        s0, s1, s2, s3 = s[g * EXPERTS_PER_GROUP:(g + 1) * EXPERTS_PER_GROUP]
        hi01, lo01 = jnp.maximum(s0, s1), jnp.minimum(s0, s1)
        hi23, lo23 = jnp.maximum(s2, s3), jnp.minimum(s2, s3)
        scores.append(jnp.maximum(hi01, hi23) + jnp.maximum(jnp.minimum(hi01, hi23), jnp.maximum(lo01, lo23)))
    weights = []
    chosen = []
    for g in range(N_GROUPS):
        gsel = None
        for g2 in range(N_GROUPS):
            if g2 == g:
                continue
            t = one(scores[g2] < scores[g]) if g2 < g else one(scores[g2] <= scores[g])
            gsel = t if gsel is None else gsel * t
        for i in range(EXPERTS_PER_GROUP):
            e = g * EXPERTS_PER_GROUP + i
            rank = None
            for j in range(EXPERTS_PER_GROUP):
                if j == i:
                    continue
                e2 = g * EXPERTS_PER_GROUP + j
                t = one(s[e2] >= s[e]) if j < i else one(s[e2] > s[e])
                rank = t if rank is None else rank + t
            chosen.append(gsel * one(rank < 1.5))
            weights.append(chosen[-1] * aff[e:e + 1, :])
    total = weights[0]
    for w in weights[1:]:
        total = total + w
    inv = 1.0 / total
    for e in range(N_EXPERTS):
        sel_ref[e:e + 1, :] = chosen[e]
        gate_ref[e:e + 1, :] = weights[e] * inv


def _out_kernel(a_ref, b_ref, x_ref, mod_ref, wa_ref, wb_ref, g_ref, bt_ref, wr_ref, br_ref, tri_ref, low_ref,
                x1_ref, pos_ref, tab_ref, cnt_ref, xs_hbm,
                sel_scr, gate_scr, meta_v, meta_s, xs_vmem, sem, cnt_smem, nblk_smem, *, cap):
    t, _, has_next, is_first = _tile_id(0)
    slot = lax.rem(t, 2)

    @pl.when(is_first)
    def _():
        for e in range(N_EXPERTS):
            cnt_smem[e] = 0
        nblk_smem[0] = 0
        nblk_smem[1] = 0

    y = (jnp.dot(a_ref[...], wa_ref[...], preferred_element_type=f32)
         + jnp.dot(b_ref[...], wb_ref[...], preferred_element_type=f32))
    x1 = _layer_norm(ALPHA * x_ref[...] + mod_ref[2:3, :] * y, g_ref[...], bt_ref[...])
    x1_ref[...] = x1
    xm = x1 * (1.0 + mod_ref[4:5, :]) + mod_ref[3:4, :]
    xm_hi = xm.astype(bf16)
    xm_res = xm - xm_hi.astype(f32)
    xm_mid = xm_res.astype(bf16)
    xm_lo = (xm_res - xm_mid.astype(f32)).astype(bf16)
    E = N_EXPERTS
    r_hi = lax.dot_general(wr_ref[...], xm_hi, _NT, preferred_element_type=f32)
    r_mid = lax.dot_general(wr_ref[0:2 * E, :], xm_mid, _NT, preferred_element_type=f32)
    r_lo = lax.dot_general(wr_ref[0:E, :], xm_lo, _NT, preferred_element_type=f32)
    logits = ((r_hi[0:E] + r_mid[0:E]) + (r_hi[E:2 * E] + r_mid[E:2 * E])) + (r_hi[2 * E:3 * E] + r_lo)
    _route(logits, br_ref[...], sel_scr, gate_scr)

    sel = sel_scr[...]
    gate = gate_scr[...]
    sel_b = sel.astype(bf16)
    rank = jnp.dot(sel_b, tri_ref[...], preferred_element_type=f32)
    below = jnp.dot(low_ref[...], sel_b, preferred_element_type=f32)
    n = jnp.sum(sel, axis=1, keepdims=True)
    npad = jnp.broadcast_to(jnp.floor((n + (XS_BLK - 1)) * (1.0 / XS_BLK)) * XS_BLK, (N_EXPERTS, LANES))
    start = jnp.dot(low_ref[...], npad.astype(bf16), preferred_element_type=f32)
    pos = start[:, :1] + rank
    first = jnp.where(below < 0.5, sel, 0.0)
    second = sel - first
    pos0 = jnp.sum(first * pos, axis=0, keepdims=True).astype(jnp.int32)
    pos1 = jnp.sum(second * pos, axis=0, keepdims=True).astype(jnp.int32)
    g0 = jnp.sum(first * gate, axis=0, keepdims=True)
    g1 = jnp.sum(second * gate, axis=0, keepdims=True)
    pos_ref[0:1, :] = pos0
    pos_ref[1:2, :] = pos1
    meta_v[0] = start.astype(jnp.int32)
    meta_v[1] = (npad * (1.0 / XS_BLK)).astype(jnp.int32)
    pltpu.sync_copy(meta_v, meta_s)

    r = lax.broadcasted_iota(jnp.int32, (XS_ROWS, ROW_TILE), 0)
    hit0 = r == pos0
    hit1 = r == pos1
    perm = jnp.where(hit0, 1.0, jnp.where(hit1, 1.0, 0.0)).astype(bf16)
    gate_sorted = jnp.sum(jnp.where(hit0, g0, jnp.where(hit1, g1, 0.0)), axis=1, keepdims=True)
    xs = jnp.dot(perm, xm_hi, preferred_element_type=f32)

    def block_copy(src_row, dst_row, sl, rows=XS_BLK):
        return pltpu.make_async_copy(xs_vmem.at[sl, pl.ds(pl.multiple_of(src_row, XS_BLK), rows), :],
                                     xs_hbm.at[pl.ds(pl.multiple_of(dst_row, XS_BLK), rows), :],
                                     sem.at[sl])

    def wait_blocks(count, sl):
        _drain(lambda rows: block_copy(0, 0, sl, rows), count)

    wait_blocks(nblk_smem[slot], slot)
    xs_vmem[slot, :, :PACK_W] = _pack_pairs(xs)
    xs_vmem[slot, :, PACK_W:] = lax.bitcast_convert_type(
        jnp.broadcast_to(gate_sorted, (XS_ROWS, LANES)), jnp.uint32)
    total = 0
    for e in range(N_EXPERTS):
        start_e = meta_s[0, e, 0]
        nblk_e = meta_s[1, e, 0]
        base = e * cap + cnt_smem[e]
        tab_ref[t, 3 * e] = base
        tab_ref[t, 3 * e + 1] = start_e
        tab_ref[t, 3 * e + 2] = nblk_e

        def body(i, carry):
            block_copy(start_e + i * XS_BLK, base + i * XS_BLK, slot).start()
            return carry

        lax.fori_loop(0, nblk_e, body, 0)
        cnt_smem[e] = cnt_smem[e] + nblk_e * XS_BLK
        total = total + nblk_e
    nblk_smem[slot] = total

    @pl.when(jnp.logical_not(has_next))
    def _():
        wait_blocks(nblk_smem[slot], slot)
        wait_blocks(nblk_smem[1 - slot], 1 - slot)
        for e in range(N_EXPERTS):
            cnt_ref[0, e] = cnt_smem[e]


def _dispatch_constants():
    t = np.arange(ROW_TILE)
    e = np.arange(N_EXPERTS)
    return ((t[:, None] < t[None, :]).astype(np.float32), (e[None, :] < e[:, None]).astype(np.float32))


def _out_proj(l, a, bmix, x, mod, w_out, ln_g, ln_b, w_router_t, b_router, tri, low):
    B, L, _ = x.shape
    half = A_WIDTH
    n_tiles = B * (L // ROW_TILE)
    cap = _expert_capacity(n_tiles)
    smem = pl.BlockSpec(memory_space=pltpu.SMEM)
    return pl.pallas_call(
        functools.partial(_out_kernel, cap=cap),
        grid=(B, L // ROW_TILE),
        in_specs=[_row_spec(A_WIDTH), _row_spec(B_WIDTH), _row_spec(D_MODEL), _mod_spec(),
                  pl.BlockSpec((None, half, D_MODEL), lambda b, j: (l, 0, 0)),
                  pl.BlockSpec((None, half, D_MODEL), lambda b, j: (l, 1, 0)),
                  _vec_spec(l), _vec_spec(l),
                  pl.BlockSpec((3 * N_EXPERTS, D_MODEL), lambda b, j: (0, 0)),
                  pl.BlockSpec((N_EXPERTS, 1), lambda b, j: (0, 0)),
                  pl.BlockSpec((ROW_TILE, ROW_TILE), lambda b, j: (0, 0)),
                  pl.BlockSpec((N_EXPERTS, N_EXPERTS), lambda b, j: (0, 0))],
        out_specs=[_row_spec(D_MODEL), _pos_spec(), smem, smem, pl.BlockSpec(memory_space=pl.ANY)],
        out_shape=[jax.ShapeDtypeStruct(x.shape, f32), jax.ShapeDtypeStruct((B, 2, L), jnp.int32),
                   jax.ShapeDtypeStruct((n_tiles, 3 * N_EXPERTS), jnp.int32),
                   jax.ShapeDtypeStruct((1, N_EXPERTS), jnp.int32),
                   jax.ShapeDtypeStruct((N_EXPERTS * cap, XS_W), jnp.uint32)],
        scratch_shapes=[pltpu.VMEM((N_EXPERTS, ROW_TILE), f32), pltpu.VMEM((N_EXPERTS, ROW_TILE), f32),
                        pltpu.VMEM((2, N_EXPERTS, LANES), jnp.int32), pltpu.SMEM((2, N_EXPERTS, LANES), jnp.int32),
                        pltpu.VMEM((2, XS_ROWS, XS_W), jnp.uint32), pltpu.SemaphoreType.DMA((2,)),
                        pltpu.SMEM((N_EXPERTS,), jnp.int32), pltpu.SMEM((2,), jnp.int32)],
        compiler_params=_params("arbitrary", "arbitrary"),
        name="out_proj",
    )(a, bmix, x, mod, w_out, w_out, ln_g, ln_b, w_router_t, b_router, tri, low)


def _expert_capacity(n_tiles):
    rows = n_tiles * (ROW_TILE + XS_BLK)
    return -(-rows // EXP_TILE) * EXP_TILE


def _moe_kernel(te_ref, tb_ref, tv_ref, xs_ref, wg_ref, wu_ref, wd_ref, ys_ref, wg_b, wu_b, wd_b):
    i = pl.program_id(0)
    valid = tv_ref[i]
    new_expert = jnp.logical_or(i == 0, te_ref[i] != te_ref[jnp.maximum(i - 1, 0)])

    @pl.when(new_expert)
    def _():
        wg_b[...] = wg_ref[...].astype(bf16)
        wu_b[...] = wu_ref[...].astype(bf16)
        wd_b[...] = wd_ref[...].astype(bf16)

    @pl.when(valid > 0)
    def _():
        chunk = MOE_CHUNK
        w32 = xs_ref[...]
        ok = lax.broadcasted_iota(jnp.int32, (EXP_TILE, 1), 0) < valid
        x = _unpack_pairs(jnp.where(ok, w32[:, :PACK_W], jnp.uint32(0)))
        gate = jnp.where(ok, lax.bitcast_convert_type(w32[:, PACK_W:PACK_W + 1], f32), 0.0)
        hid = []
        for c in range(D_EXPERT // chunk):
            cols = slice(c * chunk, (c + 1) * chunk)
            hg = jnp.dot(x, wg_b[:, cols], preferred_element_type=f32)
            hu = jnp.dot(x, wu_b[:, cols], preferred_element_type=f32)
            hid.append((hg * _sigmoid(hg) * hu * gate).astype(bf16))
        hid = jnp.concatenate(hid, axis=1)
        for c in range(PACK_W // chunk):
            lo = jnp.dot(hid, wd_b[:, c * chunk:(c + 1) * chunk], preferred_element_type=f32)
            hi = jnp.dot(hid, wd_b[:, PACK_W + c * chunk:PACK_W + (c + 1) * chunk], preferred_element_type=f32)
            both = jnp.concatenate([lo, hi], axis=1).astype(bf16).astype(f32)
            ys_ref[:, c * chunk:(c + 1) * chunk] = _pack_pairs(both)


def _expert_tiles(cnt, cap, n_steps):
    cnt = cnt.reshape(N_EXPERTS)
    tiles = (cnt + (EXP_TILE - 1)) // EXP_TILE
    ends = jnp.cumsum(tiles)
    n_valid = ends[-1]
    i = jnp.clip(jnp.arange(n_steps, dtype=jnp.int32), 0, jnp.maximum(n_valid - 1, 0))
    e = jnp.sum((i[:, None] >= ends[None, :]).astype(jnp.int32), axis=1)
    local = i - (ends - tiles)[e]
    rows = jnp.clip(cnt[e] - local * EXP_TILE, 0, EXP_TILE)
    rows = jnp.where(jnp.arange(n_steps) < n_valid, rows, 0)
    return e.astype(jnp.int32), (e * (cap // EXP_TILE) + local).astype(jnp.int32), rows.astype(jnp.int32)


def _moe(l, xs, cnt, n_tiles, w_gate, w_up, w_down):
    cap = _expert_capacity(n_tiles)
    n_steps = n_tiles * XS_ROWS // EXP_TILE + N_EXPERTS
    te, tb, tv = _expert_tiles(cnt, cap, n_steps)
    grid_spec = pltpu.PrefetchScalarGridSpec(
        num_scalar_prefetch=3,
        grid=(n_steps,),
        in_specs=[pl.BlockSpec((EXP_TILE, XS_W), lambda i, te, tb, tv: (tb[i], 0)),
                  pl.BlockSpec((None, None, D_MODEL, D_EXPERT), lambda i, te, tb, tv: (l, te[i], 0, 0)),
                  pl.BlockSpec((None, None, D_MODEL, D_EXPERT), lambda i, te, tb, tv: (l, te[i], 0, 0)),
                  pl.BlockSpec((None, None, D_EXPERT, D_MODEL), lambda i, te, tb, tv: (l, te[i], 0, 0))],
        out_specs=pl.BlockSpec((EXP_TILE, PACK_W), lambda i, te, tb, tv: (tb[i], 0)),
        scratch_shapes=[pltpu.VMEM((D_MODEL, D_EXPERT), bf16), pltpu.VMEM((D_MODEL, D_EXPERT), bf16),
                        pltpu.VMEM((D_EXPERT, D_MODEL), bf16)],
    )
    return pl.pallas_call(
        _moe_kernel,
        grid_spec=grid_spec,
        out_shape=jax.ShapeDtypeStruct((N_EXPERTS * cap, PACK_W), jnp.uint32),
        compiler_params=_params("arbitrary"),
        name="moe",
    )(te, tb, tv, xs, w_gate, w_up, w_down)


def _final_kernel(tab_ref, x_ref, pos_ref, ys_hbm, mod_ref, g_ref, b_ref, o_ref, y_vmem, sem, *, j_off):
    f = _combine_experts(tab_ref, pos_ref, ys_hbm, y_vmem, sem, j_off)
    o_ref[...] = _layer_norm(ALPHA * x_ref[...] + mod_ref[5:6, :] * f, g_ref[...], b_ref[...])


def _final_ln(x, moe, mod, ln_g, ln_b):
    B, L, _ = x.shape
    seq = L - CTX_LEN
    skip = CTX_LEN // ROW_TILE
    tab, pos, ys = moe
    lat = pl.BlockSpec((None, ROW_TILE, D_MODEL), lambda b, j: (b, j + skip, 0))
    return pl.pallas_call(
        functools.partial(_final_kernel, j_off=skip),
        grid=(B, seq // ROW_TILE),
        in_specs=[pl.BlockSpec(memory_space=pltpu.SMEM), lat, _pos_spec(skip), pl.BlockSpec(memory_space=pl.ANY),
                  pl.BlockSpec((None, None, 6, D_MODEL), lambda b, j: (b, 1, 0, 0)),
                  _vec_spec(DEPTH - 1), _vec_spec(DEPTH - 1)],
        out_specs=_row_spec(D_MODEL),
        out_shape=jax.ShapeDtypeStruct((B, seq, D_MODEL), f32),
        scratch_shapes=_combine_scratch(),
        compiler_params=_params("arbitrary", "arbitrary"),
        name="final_ln",
    )(tab, x, pos, ys, mod, ln_g, ln_b)


def kernel(x, c, ctx, c_ctx, w_ada, b_ada, w_in, lb_logits, a_norm_g, rpb, w_out, ln1_g, ln1_b,
           w_router, b_router, w_gate, w_up, w_down, ln2_g, ln2_b):
    B, seq, D = x.shape
    L = CTX_LEN + seq
    xs = jnp.concatenate([ctx, x], axis=1)

    cc = jnp.concatenate([c, c_ctx[None, :], jnp.zeros((16 - B - 1, D), f32)], axis=0)
    mods = _ada_mods(cc, w_ada, b_ada)
    lat_mod = mods[:, :B].reshape(DEPTH, B, 1, 6, D)
    ctx_mod = jnp.broadcast_to(mods[:, B].reshape(DEPTH, 1, 1, 6, D), (DEPTH, B, 1, 6, D))
    mod_all = jnp.concatenate([ctx_mod, lat_mod], axis=2)

    sm = jax.nn.softmax(lb_logits.astype(f32), axis=1)
    lower = jnp.cumsum(sm, axis=1) - sm[:, :1]

    tri_np, own_np, sgn_np = _scan_constants()
    scan_consts = (jnp.asarray(tri_np, bf16), jnp.asarray(own_np), jnp.asarray(sgn_np))
    cos_np, sin_np = _rope_tables(seq)
    cos_t, sin_t = jnp.asarray(cos_np), jnp.asarray(sin_np)
    mask = jnp.asarray(_window_mask())
    bias_t = _bias_tables(rpb.astype(f32) * LOG2E)

    w_in_b = w_in.astype(bf16)
    w_out_b = w_out.astype(bf16)
    wr_hi = w_router.T.astype(bf16)
    wr_res = w_router.T - wr_hi.astype(f32)
    wr_mid = wr_res.astype(bf16)
    wr_lo = (wr_res - wr_mid.astype(f32)).astype(bf16)
    w_router_t = jnp.concatenate([wr_hi, wr_mid, wr_lo], axis=0)
    b_router_c = b_router.reshape(N_EXPERTS, 1)
    ln1_g3, ln1_b3 = ln1_g.reshape(DEPTH, 1, D), ln1_b.reshape(DEPTH, 1, D)
    ln2_g3, ln2_b3 = ln2_g.reshape(DEPTH, 1, D), ln2_b.reshape(DEPTH, 1, D)

    tri_np, low_np = _dispatch_constants()
    tri, low = jnp.asarray(tri_np, bf16), jnp.asarray(low_np, bf16)
    n_tiles = B * (L // ROW_TILE)

    moe = None
    for l in range(DEPTH):
        pmod = mod_all[l - 1] if l > 0 else None
        xs, u = _in_proj(l, xs, moe, pmod, ln2_g3, ln2_b3, mod_all[l], w_in_b)
        a = _hgrn(u, lower[0, l], lower[1, l], a_norm_g[l], scan_consts)
        bmix = _na(l, u, cos_t, sin_t, bias_t, mask)
        xs, pos, tab, cnt, x_sorted = _out_proj(l, a, bmix, xs, mod_all[l], w_out_b, ln1_g3, ln1_b3,
                                                w_router_t, b_router_c, tri, low)
        y_sorted = _moe(l, x_sorted, cnt, n_tiles, w_gate, w_up, w_down)
        moe = (tab, pos, y_sorted)
    return _final_ln(xs, moe, mod_all[DEPTH - 1], ln2_g3, ln2_b3)
```

```python
import functools

import numpy as np
import jax
import jax.numpy as jnp
from jax import lax
from jax.experimental import pallas as pl
from jax.experimental.pallas import tpu as pltpu

D_MODEL = 1024
DEPTH = 4
GRID_W = 64
CTX_LEN = 256
A_HEADS = 4
A_DK = 128
A_WIDTH = A_HEADS * A_DK
B_HEADS = 8
B_HD = 64
B_WIDTH = B_HEADS * B_HD
D_IN = 5 * A_WIDTH + 3 * B_WIDTH
NA_ROWS = 8
NA_COLS = 16
ROPE_BASE = 10000.0
N_EXPERTS = 16
N_GROUPS = 4
EXPERTS_PER_GROUP = N_EXPERTS // N_GROUPS
D_EXPERT = 512
ALPHA = (2 * DEPTH) ** 0.25
LN_EPS = 1e-5
RMS_EPS = 1e-6
F_FLOOR = 1e-6
MASK_VALUE = -1e30
LOG2E = 1.4426950408889634

LANES = 128
ROW_TILE = 256
XS_BLK = 16
XS_ROWS = 2 * ROW_TILE + N_EXPERTS * XS_BLK
XS_W = D_MODEL + 2 * LANES
DRAIN_GROUP = 8
EXP_TILE = 1024
MOE_CHUNK = 256
SCAN_CHUNK = 128
SCAN_LEVELS = (64, 32, 16, 8, 4)
SCAN_BASE = 4
NA_GROUP = 4
VMEM_LIMIT = 56 * 1024 * 1024

f32 = jnp.float32
bf16 = jnp.bfloat16

_NT = (((1,), (1,)), ((), ()))
_TN = (((0,), (0,)), ((), ()))


def _params(*sem):
    return pltpu.CompilerParams(dimension_semantics=sem, vmem_limit_bytes=VMEM_LIMIT)


def _sigmoid(z, sign=1.0):
    return 1.0 / (1.0 + jnp.exp2(z * (-sign * LOG2E)))


def _layer_norm(h, g, b):
    mu = jnp.mean(h, axis=-1, keepdims=True)
    d = h - mu
    var = jnp.mean(d * d, axis=-1, keepdims=True)
    return d * lax.rsqrt(var + LN_EPS) * g + b


def _ada_kernel(c_ref, w_ref, b_ref, o_ref):
    c = c_ref[...]
    o_ref[...] = jnp.dot(c * _sigmoid(c), w_ref[...], preferred_element_type=f32,
                         precision=lax.Precision.HIGHEST) + b_ref[...]


def _ada_mods(cc, w_ada, b_ada):
    n_col = 4
    tn = 6 * D_MODEL // n_col
    return pl.pallas_call(
        _ada_kernel,
        grid=(DEPTH, n_col),
        in_specs=[
            pl.BlockSpec((16, D_MODEL), lambda l, j: (0, 0)),
            pl.BlockSpec((None, D_MODEL, tn), lambda l, j: (l, 0, j)),
            pl.BlockSpec((None, 1, tn), lambda l, j: (l, 0, j)),
        ],
        out_specs=pl.BlockSpec((None, 16, tn), lambda l, j: (l, 0, j)),
        out_shape=jax.ShapeDtypeStruct((DEPTH, 16, 6 * D_MODEL), f32),
        compiler_params=_params("parallel", "parallel"),
        name="ada_mods",
    )(cc, w_ada, b_ada.reshape(DEPTH, 1, 6 * D_MODEL))


def _drain(copy_of_rows, n_blocks):
    def grouped(i, carry):
        copy_of_rows(DRAIN_GROUP * XS_BLK).wait()
        return carry

    def single(i, carry):
        copy_of_rows(XS_BLK).wait()
        return carry

    lax.fori_loop(0, n_blocks // DRAIN_GROUP, grouped, 0)
    lax.fori_loop(0, lax.rem(n_blocks, DRAIN_GROUP), single, 0)


def _tile_id(j_off):
    b, j = pl.program_id(0), pl.program_id(1)
    nj = pl.num_programs(1)
    tiles_per_batch = nj + j_off
    t = b * tiles_per_batch + j + j_off
    last_j = j == nj - 1
    t_next = jnp.where(last_j, (b + 1) * tiles_per_batch + j_off, t + 1)
    has_next = jnp.logical_not(jnp.logical_and(last_j, b == pl.num_programs(0) - 1))
    return t, t_next, has_next, jnp.logical_and(b == 0, j == 0)


def _combine_experts(tab_ref, pos_ref, ys_hbm, y_vmem, sem, step, t, t_next, has_next):
    is_first = step == 0
    slot = lax.rem(step, 2)

    def block_copy(src_row, dst_row, sl, rows=XS_BLK):
        return pltpu.make_async_copy(ys_hbm.at[pl.ds(pl.multiple_of(src_row, XS_BLK), rows), :],
                                     y_vmem.at[sl, pl.ds(pl.multiple_of(dst_row, XS_BLK), rows), :],
                                     sem.at[sl])

    def fetch(tile, sl):
        y_vmem[sl] = jnp.zeros((XS_ROWS, D_MODEL), bf16)
        for e in range(N_EXPERTS):
            base, start, nblk = tab_ref[tile, 3 * e], tab_ref[tile, 3 * e + 1], tab_ref[tile, 3 * e + 2]

            def body(i, carry):
                block_copy(base + i * XS_BLK, start + i * XS_BLK, sl).start()
                return carry

            lax.fori_loop(0, nblk, body, 0)

    def wait_all(tile, sl):
        total = tab_ref[tile, 2]
        for e in range(1, N_EXPERTS):
            total = total + tab_ref[tile, 3 * e + 2]
        _drain(lambda rows: block_copy(0, 0, sl, rows), total)

    @pl.when(is_first)
    def _():
        fetch(t, slot)

    @pl.when(has_next)
    def _():
        fetch(t_next, 1 - slot)

    wait_all(t, slot)
    y = y_vmem[slot]
    r = lax.broadcasted_iota(jnp.int32, (XS_ROWS, ROW_TILE), 0)
    perm = jnp.where(r == pos_ref[0:1, :], 1.0, jnp.where(r == pos_ref[1:2, :], 1.0, 0.0)).astype(bf16)
    return lax.dot_general(perm, y, _TN, preferred_element_type=f32)


def _in_kernel(*refs, with_ln):
    if with_ln:
        (tab_ref, x_ref, pos_ref, ys_hbm, pmod_ref, g_ref, b_ref, mod_ref, w_ref,
         xo_ref, u_ref, y_vmem, sem) = refs
        t, t_next, has_next, _ = _tile_id(0)
        step = pl.program_id(0) * pl.num_programs(1) + pl.program_id(1)
        f = _combine_experts(tab_ref, pos_ref, ys_hbm, y_vmem, sem, step, t, t_next, has_next)
        h = ALPHA * x_ref[...] + pmod_ref[5:6, :] * f
        x = _layer_norm(h, g_ref[...], b_ref[...])
        xo_ref[...] = x
    else:
        x_ref, mod_ref, w_ref, u_ref = refs
        x = x_ref[...]
    xm = x * (1.0 + mod_ref[1:2, :]) + mod_ref[0:1, :]
    u_ref[...] = jnp.dot(xm.astype(bf16), w_ref[...], preferred_element_type=f32)


def _mod_spec():
    return pl.BlockSpec((None, None, 6, D_MODEL), lambda b, j: (b, jnp.minimum(j, 1), 0, 0))


def _row_spec(width):
    return pl.BlockSpec((None, ROW_TILE, width), lambda b, j: (b, j, 0))


def _vec_spec(l):
    return pl.BlockSpec((None, 1, D_MODEL), lambda b, j: (l, 0, 0))


def _pos_spec(j_off=0):
    return pl.BlockSpec((None, 2, ROW_TILE), lambda b, j: (b, 0, j + j_off))


def _combine_scratch():
    return [pltpu.VMEM((2, XS_ROWS, D_MODEL), bf16), pltpu.SemaphoreType.DMA((2,))]


def _in_proj(l, x, moe, pmod, ln_g, ln_b, mod, w_in):
    B, L, _ = x.shape
    w_spec = pl.BlockSpec((None, D_MODEL, D_IN), lambda b, j: (l, 0, 0))
    u_shape = jax.ShapeDtypeStruct((B, L, D_IN), f32)
    if moe is None:
        u = pl.pallas_call(
            functools.partial(_in_kernel, with_ln=False),
            grid=(B, L // ROW_TILE),
            in_specs=[_row_spec(D_MODEL), _mod_spec(), w_spec],
            out_specs=_row_spec(D_IN),
            out_shape=u_shape,
            compiler_params=_params("parallel", "parallel"),
            name="in_proj0",
        )(x, mod, w_in)
        return x, u
    tab, pos, ys = moe
    return pl.pallas_call(
        functools.partial(_in_kernel, with_ln=True),
        grid=(B, L // ROW_TILE),
        in_specs=[pl.BlockSpec(memory_space=pltpu.SMEM), _row_spec(D_MODEL), _pos_spec(),
                  pl.BlockSpec(memory_space=pl.ANY), _mod_spec(), _vec_spec(l - 1), _vec_spec(l - 1),
                  _mod_spec(), w_spec],
        out_specs=[_row_spec(D_MODEL), _row_spec(D_IN)],
        out_shape=[jax.ShapeDtypeStruct(x.shape, f32), u_shape],
        scratch_shapes=_combine_scratch(),
        compiler_params=_params("arbitrary", "arbitrary"),
        name="in_proj",
    )(tab, x, pos, ys, pmod, ln_g, ln_b, mod, w_in)


def _scan_constants():
    C = SCAN_CHUNK
    p = np.arange(C)
    tri = (p[None, :] <= p[:, None]).astype(np.float32)
    owner = np.full((C, C), -1, np.int32)
    signs = []
    for li, m in enumerate(SCAN_LEVELS):
        same = (p[:, None] // (2 * m)) == (p[None, :] // (2 * m))
        owner[same & ((p[:, None] // m) % 2 == 1) & ((p[None, :] // m) % 2 == 0)] = li
        signs.append(np.where((p // m) % 2 == 1, 1.0, -1.0))
    owner[((p[:, None] // SCAN_BASE) == (p[None, :] // SCAN_BASE)) & (p[None, :] <= p[:, None])] = len(SCAN_LEVELS)
    sgn = np.broadcast_to(np.stack(signs)[:, :, None], (len(SCAN_LEVELS), C, LANES)).astype(np.float32)
    return (np.stack([tri, tri[::-1, ::-1]]), np.stack([owner, owner[::-1, ::-1]]),
            np.stack([sgn, sgn[:, ::-1]]))


def _hgrn_kernel(q_ref, ff_ref, fb_ref, v_ref, g_ref, lbf_ref, lbb_ref, ng_ref, t_ref, own_ref, sgn_ref,
                 a_ref, of_scr, ob_scr, *, n_ctx, n_all):
    C = SCAN_CHUNK
    n_lv = len(SCAN_LEVELS)


    def gates(rows, z_ref, lb, d):
        z = z_ref[rows, :]
        lf = jnp.log2(jnp.maximum(lb + (1.0 - lb) * _sigmoid(z), F_FLOOR))
        k = (1.0 - lb) * _sigmoid(z, -1.0)
        qz = q_ref[rows, :]
        q = qz * _sigmoid(qz)
        v = v_ref[rows, :].astype(bf16)
        hi = lf.astype(bf16)
        r1 = lf - hi.astype(f32)
        mid = r1.astype(bf16)
        lo = (r1 - mid.astype(f32)).astype(bf16)
        d3 = jnp.dot(t_ref[d], jnp.concatenate([hi, mid, lo], axis=1), preferred_element_type=f32)
        cum = d3[:, :LANES] + d3[:, LANES:2 * LANES] + d3[:, 2 * LANES:]
        return q, k, v, cum

    def intra(q, k, cum, d):
        last = cum[C - 1:C, :] if d == 0 else cum[0:1, :]

        def rel(block, ref_rows):
            blocks = cum.reshape(C // block, block, LANES)
            if len(ref_rows) == 1:
                ref = blocks[:, ref_rows[0]:ref_rows[0] + 1, :]
            else:
                first = lax.broadcasted_iota(jnp.int32, (1, block, 1), 1) < block // 2
                ref = jnp.where(first, blocks[:, ref_rows[0]:ref_rows[0] + 1, :],
                                blocks[:, ref_rows[1]:ref_rows[1] + 1, :])
            return (blocks - ref).reshape(C, LANES)

        def to_boundary(m):
            blocks = cum.reshape(C // (2 * m), 2 * m, LANES)
            lo, hi = blocks[:, :m, :], blocks[:, m:, :]
            if d == 0:
                ref = lo[:, m - 1:m, :]
                parts = (ref - lo, hi - ref)
            else:
                ref = hi[:, 0:1, :]
                parts = (lo - ref, ref - hi)
            return jnp.concatenate(parts, axis=1).reshape(C, LANES)

        q_in = (q * jnp.exp2(cum)).astype(bf16)
        own = own_ref[d]
        scores = jnp.zeros((C, C), f32)
        for li in range(n_lv + 1):
            m = SCAN_LEVELS[li] if li < n_lv else 0
            if m and m % 8 == 0:
                def halves(x):
                    blocks = x.reshape(C // (2 * m), 2 * m, x.shape[-1])
                    return blocks[:, :m, :], blocks[:, m:, :]

                def join(lo, hi):
                    return jnp.concatenate([lo, hi], axis=1).reshape(C, lo.shape[-1])

                (q_lo, q_hi), (k_lo, k_hi) = halves(q), halves(k)
                mixed = join(k_lo, q_hi) if d == 0 else join(q_lo, k_hi)
                z = (mixed * jnp.exp2(to_boundary(m))).astype(bf16)
                a = lax.dot_general(z, z, _NT, preferred_element_type=f32)
                (a_lo, a_hi), (s_lo, s_hi), (o_lo, o_hi) = halves(a), halves(scores), halves(own)
                if d == 0:
                    scores = join(s_lo, jnp.where(o_hi == li, a_hi, s_hi))
                else:
                    scores = join(jnp.where(o_lo == li, a_lo, s_lo), s_hi)
                continue
            if m:
                wq = jnp.exp2(rel(2 * m, (m - 1,) if d == 0 else (m,)) * sgn_ref[d, li])
                wk = wq
            else:
                nb = SCAN_BASE
                dl = rel(2 * nb, (0, nb) if d == 0 else (nb - 1, 2 * nb - 1))
                wq = jnp.exp2(dl)
                wk = jnp.exp2(-dl)
            a = lax.dot_general((q * wq).astype(bf16), (k * wk).astype(bf16), _NT,
                                preferred_element_type=f32)
            scores = jnp.where(own == li, a, scores)
        kdec = (k * jnp.exp2(last - cum)).astype(bf16)
        return q_in, scores.astype(bf16), kdec, jnp.exp2(last)

    def finish(pending, S, o_scr):
        q_in, scores, kdec, decay, v, row0 = pending
        o = (lax.dot_general(q_in, S.astype(bf16), _NT, preferred_element_type=f32)
             + jnp.dot(scores, v, preferred_element_type=f32))
        o_scr[pl.ds(pl.multiple_of(row0, C), C), :] = o
        return S * decay + lax.dot_general(v, kdec, _TN, preferred_element_type=f32)

    lbf = lbf_ref[...]
    lbb = lbb_ref[...]

    def bwd_chunk(i):
        return jnp.where(i < n_ctx, n_ctx - 1 - i, n_all - 1 - (i - n_ctx))

    def scan_body(i, carry):
        s_f, s_b, pend_f, pend_b = carry
        row_f = i * C
        row_b = bwd_chunk(i) * C
        s_f = finish(pend_f, s_f, of_scr)
        s_b = finish(pend_b, s_b, ob_scr)
        qf, kf, vf, cum_f = gates(pl.ds(pl.multiple_of(row_f, C), C), ff_ref, lbf, 0)
        qb, kb, vb, cum_b = gates(pl.ds(pl.multiple_of(row_b, C), C), fb_ref, lbb, 1)
        pend_f = intra(qf, kf, cum_f, 0) + (vf, row_f)
        pend_b = intra(qb, kb, cum_b, 1) + (vb, row_b)
        return s_f, s_b, pend_f, pend_b

    def idle(row0):
        zc = jnp.zeros((C, LANES), bf16)
        return (zc, jnp.zeros((C, C), bf16), zc, jnp.ones((1, LANES), f32), zc, row0)

    zero = jnp.zeros((A_DK, A_DK), f32)
    init = (zero, zero, idle(jnp.int32(0)), idle(bwd_chunk(jnp.int32(0)) * C))
    s_f, s_b, pend_f, pend_b = lax.fori_loop(0, n_all, scan_body, init, unroll=3)
    finish(pend_f, s_f, of_scr)
    finish(pend_b, s_b, ob_scr)

    ng = ng_ref[...]

    def readout_body(i, carry):
        rows = pl.ds(pl.multiple_of(i * ROW_TILE, ROW_TILE), ROW_TILE)
        o = of_scr[rows, :] + ob_scr[rows, :]
        o = o * lax.rsqrt(jnp.mean(o * o, axis=-1, keepdims=True) + RMS_EPS) * ng
        gz = g_ref[rows, :]
        a_ref[rows, :] = (o * (gz * _sigmoid(gz))).astype(a_ref.dtype)
        return carry

    lax.fori_loop(0, n_all * C // ROW_TILE, readout_body, 0)


def _hgrn(u, lb_f, lb_b, norm_g, scan_consts):
    B, L, _ = u.shape
    hb = A_WIDTH // LANES

    def col(sec):
        return pl.BlockSpec((None, L, LANES), lambda b, h: (b, 0, sec * hb + h))

    def whole(arr):
        return pl.BlockSpec(arr.shape, lambda b, h: (0,) * arr.ndim)

    lb_spec = pl.BlockSpec((None, 1, LANES), lambda b, h: (h, 0, 0))
    return pl.pallas_call(
        functools.partial(_hgrn_kernel, n_ctx=CTX_LEN // SCAN_CHUNK, n_all=L // SCAN_CHUNK),
        grid=(B, A_HEADS),
        in_specs=[col(0), col(1), col(2), col(3), col(4), lb_spec, lb_spec,
                  pl.BlockSpec((1, LANES), lambda b, h: (0, 0))] + [whole(a) for a in scan_consts],
        out_specs=pl.BlockSpec((None, L, LANES), lambda b, h: (b, 0, h)),
        out_shape=jax.ShapeDtypeStruct((B, L, A_WIDTH), bf16),
        scratch_shapes=[pltpu.VMEM((L, LANES), f32), pltpu.VMEM((L, LANES), f32)],
        compiler_params=_params("parallel", "parallel"),
        name="hgrn2",
    )(u, u, u, u, u, lb_f.reshape(A_HEADS, 1, LANES), lb_b.reshape(A_HEADS, 1, LANES),
      norm_g.reshape(1, LANES), *scan_consts)


def _rope_tables(seq):
    t = np.arange(seq)
    pos_r = (t // GRID_W).astype(np.float32)
    pos_c = (t % GRID_W).astype(np.float32)
    quarter = B_HD // 4
    inv = (ROPE_BASE ** (-np.arange(quarter, dtype=np.float32) / quarter)).astype(np.float32)
    lane = np.arange(LANES)
    in_head = lane % B_HD
    pos = np.where((in_head < B_HD // 2)[None, :], pos_r[:, None], pos_c[:, None])
    ang = (pos * inv[lane % quarter][None, :]).astype(np.float32)
    sign = np.where((lane % (2 * quarter)) < quarter, -1.0, 1.0)
    return np.cos(ang).astype(np.float32), (np.sin(ang) * sign[None, :]).astype(np.float32)


def _window_mask():
    c = np.arange(GRID_W)
    cs = np.clip(c - NA_COLS // 2, 0, GRID_W - NA_COLS)
    in_win = (c[None, :] >= cs[:, None]) & (c[None, :] < cs[:, None] + NA_COLS)
    return np.tile(in_win, (1, NA_ROWS)).astype(np.float32)


def _bias_tables(rpb):
    c = np.arange(GRID_W)
    dc = np.clip(c[None, :] - c[:, None], -(NA_COLS - 1), NA_COLS - 1) + (NA_COLS - 1)
    pick = (np.arange(2 * NA_COLS - 1)[:, None, None] == dc[None]).astype(np.float32)
    cols = jnp.einsum('lhdk,kcx->lhdcx', rpb, jnp.asarray(pick), precision=lax.Precision.HIGHEST)
    n_rel = 2 * NA_ROWS - 1
    n_win = NA_ROWS * GRID_W
    t = pl.pallas_call(
        _bias_expand_kernel,
        grid=(DEPTH * B_HEADS,),
        in_specs=[pl.BlockSpec((None, n_rel, GRID_W, GRID_W), lambda i: (i, 0, 0, 0))],
        out_specs=pl.BlockSpec((None, NA_ROWS, GRID_W, n_win), lambda i: (i, 0, 0, 0)),
        out_shape=jax.ShapeDtypeStruct((DEPTH * B_HEADS, NA_ROWS, GRID_W, n_win), f32),
        compiler_params=_params("parallel"),
        name="bias_expand",
    )(cols.reshape(DEPTH * B_HEADS, n_rel, GRID_W, GRID_W))
    return t.reshape(DEPTH, B_HEADS, NA_ROWS, GRID_W, n_win)


def _bias_expand_kernel(c_ref, o_ref):
    per_vreg = LANES // GRID_W
    for o in range(NA_ROWS):
        for g in range(NA_ROWS // per_vreg):
            parts = [c_ref[o + g * per_vreg + p] for p in range(per_vreg)]
            o_ref[o, :, g * LANES:(g + 1) * LANES] = jnp.concatenate(parts, axis=1)


def _na_kernel(q_ref, k_ref, v_ref, cos_ref, sin_ref, bias_ref, mask_ref, o_ref,
               qr_scr, kr_scr, qp_scr, kc_scr, v_scr, *, seq, rows_grid):
    hd = B_HD
    scale = hd ** -0.5 * LOG2E
    n_win = NA_ROWS * GRID_W
    lane = lax.broadcasted_iota(jnp.int32, (ROW_TILE, LANES), 1)
    first_half = (lane % (hd // 2)) < (hd // 4)
    head0 = lane < hd

    def swap_halves(x):
        return jnp.where(first_half, pltpu.roll(x, LANES - hd // 4, axis=1), pltpu.roll(x, hd // 4, axis=1))

    def store_per_head(dst, rows, x):
        dst[0, rows, :] = jnp.where(head0, x, 0.0).astype(bf16)
        dst[1, rows, :] = jnp.where(head0, 0.0, x).astype(bf16)

    ctx_rows = pl.ds(0, CTX_LEN)
    store_per_head(qp_scr, ctx_rows, q_ref[ctx_rows, :] * scale)
    kc_scr[...] = k_ref[ctx_rows, :].astype(bf16)
    v_scr[ctx_rows, :] = v_ref[ctx_rows, :].astype(bf16)

    def prep(i, carry):
        lrows = pl.ds(pl.multiple_of(i * ROW_TILE, ROW_TILE), ROW_TILE)
        rows = pl.ds(pl.multiple_of(CTX_LEN + i * ROW_TILE, ROW_TILE), ROW_TILE)
        q = q_ref[rows, :] * scale
        k = k_ref[rows, :]
        cos = cos_ref[lrows, :]
        sin = sin_ref[lrows, :]
        store_per_head(qp_scr, rows, q)
        v_scr[rows, :] = v_ref[rows, :].astype(bf16)
        store_per_head(qr_scr, lrows, q * cos + swap_halves(q) * sin)
        kr_scr[lrows, :] = (k * cos + swap_halves(k) * sin).astype(bf16)
        return carry

    lax.fori_loop(0, seq // ROW_TILE, prep, 0)

    def pick_heads(o0, o1):
        keep0 = lax.broadcasted_iota(jnp.int32, o0.shape, 1) < hd
        return jnp.where(keep0, o0, o1)

    outs = []
    for h in range(2):
        s = lax.dot_general(qp_scr[h, ctx_rows, :], kc_scr[...], _NT, preferred_element_type=f32)
        e = jnp.exp2(s - jnp.max(s, axis=-1, keepdims=True))
        o = jnp.dot(e.astype(bf16), v_scr[ctx_rows, :], preferred_element_type=f32)
        outs.append(o / jnp.sum(e, axis=-1, keepdims=True))
    o_ref[ctx_rows, :] = pick_heads(*outs).astype(o_ref.dtype)

    def rows_body(it, carry):
        chains = []
        for g in range(NA_GROUP):
            r = it * NA_GROUP + g
            rs = jnp.clip(r - NA_ROWS // 2, 0, rows_grid - NA_ROWS)
            for h in range(2):
                chains.append(dict(
                    h=h, off=rs - r + (NA_ROWS - 1),
                    qrows=pl.ds(pl.multiple_of(r * GRID_W, GRID_W), GRID_W),
                    krows=pl.ds(pl.multiple_of(rs * GRID_W, GRID_W), n_win),
                    arows=pl.ds(pl.multiple_of(CTX_LEN + r * GRID_W, GRID_W), GRID_W),
                    vrows=pl.ds(pl.multiple_of(CTX_LEN + rs * GRID_W, GRID_W), n_win)))
        for c in chains:
            h = c["h"]
            c["s_lat"] = lax.dot_general(qr_scr[h, c["qrows"], :], kr_scr[c["krows"], :], _NT,
                                         preferred_element_type=f32)
            c["s_ctx"] = lax.dot_general(qp_scr[h, c["arows"], :], kc_scr[...], _NT, preferred_element_type=f32)
        in_win = mask_ref[...] > 0.5
        for c in chains:
            s_lat = jnp.where(in_win, c["s_lat"] + bias_ref[c["h"], c["off"]], MASK_VALUE)
            m = jnp.maximum(jnp.max(s_lat, axis=-1, keepdims=True), jnp.max(c["s_ctx"], axis=-1, keepdims=True))
            e_lat = jnp.exp2(s_lat - m)
            e_ctx = jnp.exp2(c["s_ctx"] - m)
            c["den"] = jnp.sum(e_lat, axis=-1, keepdims=True) + jnp.sum(e_ctx, axis=-1, keepdims=True)
            c["e_lat"] = e_lat.astype(bf16)
            c["e_ctx"] = e_ctx.astype(bf16)
        for c in chains:
            c["o"] = (jnp.dot(c["e_lat"], v_scr[c["vrows"], :], preferred_element_type=f32)
                      + jnp.dot(c["e_ctx"], v_scr[ctx_rows, :], preferred_element_type=f32))
        for g in range(NA_GROUP):
            c0, c1 = chains[2 * g:2 * g + 2]
            o_ref[c0["arows"], :] = pick_heads(c0["o"] / c0["den"], c1["o"] / c1["den"]).astype(o_ref.dtype)
        return carry

    lax.fori_loop(0, rows_grid // NA_GROUP, rows_body, 0)


def _na(l, u, cos_t, sin_t, bias_t, mask):
    B, L, _ = u.shape
    seq = L - CTX_LEN
    first = 5 * A_WIDTH // LANES
    pairs = B_WIDTH // LANES

    def col(sec):
        return pl.BlockSpec((None, L, LANES), lambda b, h: (b, 0, first + sec * pairs + h))

    n_win = NA_ROWS * GRID_W
    return pl.pallas_call(
        functools.partial(_na_kernel, seq=seq, rows_grid=seq // GRID_W),
        grid=(B, pairs),
        in_specs=[col(0), col(1), col(2),
                  pl.BlockSpec((seq, LANES), lambda b, h: (0, 0)),
                  pl.BlockSpec((seq, LANES), lambda b, h: (0, 0)),
                  pl.BlockSpec((None, 2, NA_ROWS, GRID_W, n_win), lambda b, h: (l, h, 0, 0, 0)),
                  pl.BlockSpec((GRID_W, n_win), lambda b, h: (0, 0))],
        out_specs=pl.BlockSpec((None, L, LANES), lambda b, h: (b, 0, h)),
        out_shape=jax.ShapeDtypeStruct((B, L, B_WIDTH), bf16),
        scratch_shapes=[pltpu.VMEM((2, seq, LANES), bf16), pltpu.VMEM((seq, LANES), bf16),
                        pltpu.VMEM((2, L, LANES), bf16), pltpu.VMEM((CTX_LEN, LANES), bf16),
                        pltpu.VMEM((L, LANES), bf16)],
        compiler_params=_params("parallel", "parallel"),
        name="na_attn",
    )(u, u, u, cos_t, sin_t, bias_t, mask)


def _route(logits, b_router, sel_ref, gate_ref):
    aff = _sigmoid(logits)
    sel = aff + b_router
    s = [sel[e:e + 1, :] for e in range(N_EXPERTS)]
    one = lambda cond: jnp.where(cond, 1.0, 0.0)
    scores = []
    for g in range(N_GROUPS):
        s0, s1, s2, s3 = s[g * EXPERTS_PER_GROUP:(g + 1) * EXPERTS_PER_GROUP]
        hi01, lo01 = jnp.maximum(s0, s1), jnp.minimum(s0, s1)
        hi23, lo23 = jnp.maximum(s2, s3), jnp.minimum(s2, s3)
        scores.append(jnp.maximum(hi01, hi23) + jnp.maximum(jnp.minimum(hi01, hi23), jnp.maximum(lo01, lo23)))
    weights = []
    chosen = []
    for g in range(N_GROUPS):
        gsel = None
        for g2 in range(N_GROUPS):
            if g2 == g:
                continue
            t = one(scores[g2] < scores[g]) if g2 < g else one(scores[g2] <= scores[g])
            gsel = t if gsel is None else gsel * t
        for i in range(EXPERTS_PER_GROUP):
            e = g * EXPERTS_PER_GROUP + i
            rank = None
            for j in range(EXPERTS_PER_GROUP):
                if j == i:
                    continue
                e2 = g * EXPERTS_PER_GROUP + j
                t = one(s[e2] >= s[e]) if j < i else one(s[e2] > s[e])
                rank = t if rank is None else rank + t
            chosen.append(gsel * one(rank < 1.5))
            weights.append(chosen[-1] * aff[e:e + 1, :])
    total = weights[0]
    for w in weights[1:]:
        total = total + w
    inv = 1.0 / total
    for e in range(N_EXPERTS):
        sel_ref[e:e + 1, :] = chosen[e]
        gate_ref[e:e + 1, :] = weights[e] * inv


def _out_kernel(a_ref, b_ref, x_ref, mod_ref, wa_ref, wb_ref, g_ref, bt_ref, wr_ref, br_ref, tri_ref, low_ref,
                x1_ref, pos_ref, tab_ref, cnt_ref, xs_hbm,
                sel_scr, gate_scr, meta_v, meta_s, xs_vmem, sem, cnt_smem, nblk_smem, *, cap):
    t, _, has_next, is_first = _tile_id(0)
    slot = lax.rem(t, 2)

    @pl.when(is_first)
    def _():
        for e in range(N_EXPERTS):
            cnt_smem[e] = 0
        nblk_smem[0] = 0
        nblk_smem[1] = 0

    y = (jnp.dot(a_ref[...], wa_ref[...], preferred_element_type=f32)
         + jnp.dot(b_ref[...], wb_ref[...], preferred_element_type=f32))
    x1 = _layer_norm(ALPHA * x_ref[...] + mod_ref[2:3, :] * y, g_ref[...], bt_ref[...])
    x1_ref[...] = x1
    xm = x1 * (1.0 + mod_ref[4:5, :]) + mod_ref[3:4, :]
    xm_hi = xm.astype(bf16)
    xm_res = xm - xm_hi.astype(f32)
    xm_mid = xm_res.astype(bf16)
    xm_lo = (xm_res - xm_mid.astype(f32)).astype(bf16)
    E = N_EXPERTS
    r_hi = lax.dot_general(wr_ref[...], xm_hi, _NT, preferred_element_type=f32)
    r_mid = lax.dot_general(wr_ref[0:2 * E, :], xm_mid, _NT, preferred_element_type=f32)
    r_lo = lax.dot_general(wr_ref[0:E, :], xm_lo, _NT, preferred_element_type=f32)
    logits = ((r_hi[0:E] + r_mid[0:E]) + (r_hi[E:2 * E] + r_mid[E:2 * E])) + (r_hi[2 * E:3 * E] + r_lo)
    _route(logits, br_ref[...], sel_scr, gate_scr)

    sel = sel_scr[...]
    gate = gate_scr[...]
    sel_b = sel.astype(bf16)
    rank = jnp.dot(sel_b, tri_ref[...], preferred_element_type=f32)
    below = jnp.dot(low_ref[...], sel_b, preferred_element_type=f32)
    n = jnp.sum(sel, axis=1, keepdims=True)
    npad = jnp.broadcast_to(jnp.floor((n + (XS_BLK - 1)) * (1.0 / XS_BLK)) * XS_BLK, (N_EXPERTS, LANES))
    start = jnp.dot(low_ref[...], npad.astype(bf16), preferred_element_type=f32)
    pos = start[:, :1] + rank
    first = jnp.where(below < 0.5, sel, 0.0)
    second = sel - first
    pos0 = jnp.sum(first * pos, axis=0, keepdims=True).astype(jnp.int32)
    pos1 = jnp.sum(second * pos, axis=0, keepdims=True).astype(jnp.int32)
    g0 = jnp.sum(first * gate, axis=0, keepdims=True)
    g1 = jnp.sum(second * gate, axis=0, keepdims=True)
    pos_ref[0:1, :] = pos0
    pos_ref[1:2, :] = pos1
    meta_v[0] = start.astype(jnp.int32)
    meta_v[1] = (npad * (1.0 / XS_BLK)).astype(jnp.int32)
    pltpu.sync_copy(meta_v, meta_s)

    r = lax.broadcasted_iota(jnp.int32, (XS_ROWS, ROW_TILE), 0)
    hit0 = r == pos0
    hit1 = r == pos1
    perm = jnp.where(hit0, 1.0, jnp.where(hit1, 1.0, 0.0)).astype(bf16)
    gate_sorted = jnp.sum(jnp.where(hit0, g0, jnp.where(hit1, g1, 0.0)), axis=1, keepdims=True)
    xs = jnp.dot(perm, xm_hi, preferred_element_type=f32)

    def block_copy(src_row, dst_row, sl, rows=XS_BLK):
        return pltpu.make_async_copy(xs_vmem.at[sl, pl.ds(pl.multiple_of(src_row, XS_BLK), rows), :],
                                     xs_hbm.at[pl.ds(pl.multiple_of(dst_row, XS_BLK), rows), :],
                                     sem.at[sl])

    def wait_blocks(count, sl):
        _drain(lambda rows: block_copy(0, 0, sl, rows), count)

    wait_blocks(nblk_smem[slot], slot)
    xs_vmem[slot, :, :D_MODEL] = xs.astype(bf16)
    gate_hi = gate_sorted.astype(bf16)
    gate_lo = (gate_sorted - gate_hi.astype(f32)).astype(bf16)
    xs_vmem[slot, :, D_MODEL:D_MODEL + LANES] = jnp.broadcast_to(gate_hi, (XS_ROWS, LANES))
    xs_vmem[slot, :, D_MODEL + LANES:] = jnp.broadcast_to(gate_lo, (XS_ROWS, LANES))
    total = 0
    for e in range(N_EXPERTS):
        start_e = meta_s[0, e, 0]
        nblk_e = meta_s[1, e, 0]
        base = e * cap + cnt_smem[e]
        tab_ref[t, 3 * e] = base
        tab_ref[t, 3 * e + 1] = start_e
        tab_ref[t, 3 * e + 2] = nblk_e

        def body(i, carry):
            block_copy(start_e + i * XS_BLK, base + i * XS_BLK, slot).start()
            return carry

        lax.fori_loop(0, nblk_e, body, 0)
        cnt_smem[e] = cnt_smem[e] + nblk_e * XS_BLK
        total = total + nblk_e
    nblk_smem[slot] = total

    @pl.when(jnp.logical_not(has_next))
    def _():
        wait_blocks(nblk_smem[slot], slot)
        wait_blocks(nblk_smem[1 - slot], 1 - slot)
        for e in range(N_EXPERTS):
            cnt_ref[0, e] = cnt_smem[e]


def _dispatch_constants():
    t = np.arange(ROW_TILE)
    e = np.arange(N_EXPERTS)
    return ((t[:, None] < t[None, :]).astype(np.float32), (e[None, :] < e[:, None]).astype(np.float32))


def _out_proj(l, a, bmix, x, mod, w_out, ln_g, ln_b, w_router_t, b_router, tri, low):
    B, L, _ = x.shape
    half = A_WIDTH
    n_tiles = B * (L // ROW_TILE)
    cap = _expert_capacity(n_tiles)
    smem = pl.BlockSpec(memory_space=pltpu.SMEM)
    return pl.pallas_call(
        functools.partial(_out_kernel, cap=cap),
        grid=(B, L // ROW_TILE),
        in_specs=[_row_spec(A_WIDTH), _row_spec(B_WIDTH), _row_spec(D_MODEL), _mod_spec(),
                  pl.BlockSpec((None, half, D_MODEL), lambda b, j: (l, 0, 0)),
                  pl.BlockSpec((None, half, D_MODEL), lambda b, j: (l, 1, 0)),
                  _vec_spec(l), _vec_spec(l),
                  pl.BlockSpec((3 * N_EXPERTS, D_MODEL), lambda b, j: (0, 0)),
                  pl.BlockSpec((N_EXPERTS, 1), lambda b, j: (0, 0)),
                  pl.BlockSpec((ROW_TILE, ROW_TILE), lambda b, j: (0, 0)),
                  pl.BlockSpec((N_EXPERTS, N_EXPERTS), lambda b, j: (0, 0))],
        out_specs=[_row_spec(D_MODEL), _pos_spec(), smem, smem, pl.BlockSpec(memory_space=pl.ANY)],
        out_shape=[jax.ShapeDtypeStruct(x.shape, f32), jax.ShapeDtypeStruct((B, 2, L), jnp.int32),
                   jax.ShapeDtypeStruct((n_tiles, 3 * N_EXPERTS), jnp.int32),
                   jax.ShapeDtypeStruct((1, N_EXPERTS), jnp.int32),
                   jax.ShapeDtypeStruct((N_EXPERTS * cap, XS_W), bf16)],
        scratch_shapes=[pltpu.VMEM((N_EXPERTS, ROW_TILE), f32), pltpu.VMEM((N_EXPERTS, ROW_TILE), f32),
                        pltpu.VMEM((2, N_EXPERTS, LANES), jnp.int32), pltpu.SMEM((2, N_EXPERTS, LANES), jnp.int32),
                        pltpu.VMEM((2, XS_ROWS, XS_W), bf16), pltpu.SemaphoreType.DMA((2,)),
                        pltpu.SMEM((N_EXPERTS,), jnp.int32), pltpu.SMEM((2,), jnp.int32)],
        compiler_params=_params("arbitrary", "arbitrary"),
        name="out_proj",
    )(a, bmix, x, mod, w_out, w_out, ln_g, ln_b, w_router_t, b_router, tri, low)


def _expert_capacity(n_tiles):
    rows = n_tiles * (ROW_TILE + XS_BLK)
    return -(-rows // EXP_TILE) * EXP_TILE


def _moe_kernel(te_ref, tb_ref, tv_ref, xs_ref, wg_ref, wu_ref, wd_ref, ys_ref, wg_b, wu_b, wd_b):
    i = pl.program_id(0)
    valid = tv_ref[i]
    new_expert = jnp.logical_or(i == 0, te_ref[i] != te_ref[jnp.maximum(i - 1, 0)])

    @pl.when(new_expert)
    def _():
        wg_b[...] = wg_ref[...].astype(bf16)
        wu_b[...] = wu_ref[...].astype(bf16)
        wd_b[...] = wd_ref[...].astype(bf16)

    @pl.when(valid > 0)
    def _():
        chunk = MOE_CHUNK
        ok = lax.broadcasted_iota(jnp.int32, (EXP_TILE, 1), 0) < valid
        x = jnp.where(ok, xs_ref[:, :D_MODEL], jnp.zeros((), bf16))
        gate_terms = (xs_ref[:, D_MODEL:D_MODEL + 1].astype(f32)
                      + xs_ref[:, D_MODEL + LANES:D_MODEL + LANES + 1].astype(f32))
        gate = jnp.where(ok, gate_terms, 0.0)
        hid = []
        for c in range(D_EXPERT // chunk):
            cols = slice(c * chunk, (c + 1) * chunk)
            hg = jnp.dot(x, wg_b[:, cols], preferred_element_type=f32)
            hu = jnp.dot(x, wu_b[:, cols], preferred_element_type=f32)
            hid.append((hg * _sigmoid(hg) * hu * gate).astype(bf16))
        hid = jnp.concatenate(hid, axis=1)
        for c in range(D_MODEL // chunk):
            cols = slice(c * chunk, (c + 1) * chunk)
            ys_ref[:, cols] = jnp.dot(hid, wd_b[:, cols], preferred_element_type=f32).astype(bf16)


def _expert_tiles(cnt, cap, n_steps):
    cnt = cnt.reshape(N_EXPERTS)
    tiles = (cnt + (EXP_TILE - 1)) // EXP_TILE
    ends = jnp.cumsum(tiles)
    n_valid = ends[-1]
    i = jnp.clip(jnp.arange(n_steps, dtype=jnp.int32), 0, jnp.maximum(n_valid - 1, 0))
    e = jnp.sum((i[:, None] >= ends[None, :]).astype(jnp.int32), axis=1)
    local = i - (ends - tiles)[e]
    rows = jnp.clip(cnt[e] - local * EXP_TILE, 0, EXP_TILE)
    rows = jnp.where(jnp.arange(n_steps) < n_valid, rows, 0)
    return e.astype(jnp.int32), (e * (cap // EXP_TILE) + local).astype(jnp.int32), rows.astype(jnp.int32)


def _moe(l, xs, cnt, n_tiles, w_gate, w_up, w_down):
    cap = _expert_capacity(n_tiles)
    n_steps = n_tiles * XS_ROWS // EXP_TILE + N_EXPERTS
    te, tb, tv = _expert_tiles(cnt, cap, n_steps)
    grid_spec = pltpu.PrefetchScalarGridSpec(
        num_scalar_prefetch=3,
        grid=(n_steps,),
        in_specs=[pl.BlockSpec((EXP_TILE, XS_W), lambda i, te, tb, tv: (tb[i], 0)),
                  pl.BlockSpec((None, None, D_MODEL, D_EXPERT), lambda i, te, tb, tv: (l, te[i], 0, 0)),
                  pl.BlockSpec((None, None, D_MODEL, D_EXPERT), lambda i, te, tb, tv: (l, te[i], 0, 0)),
                  pl.BlockSpec((None, None, D_EXPERT, D_MODEL), lambda i, te, tb, tv: (l, te[i], 0, 0))],
        out_specs=pl.BlockSpec((EXP_TILE, D_MODEL), lambda i, te, tb, tv: (tb[i], 0)),
        scratch_shapes=[pltpu.VMEM((D_MODEL, D_EXPERT), bf16), pltpu.VMEM((D_MODEL, D_EXPERT), bf16),
                        pltpu.VMEM((D_EXPERT, D_MODEL), bf16)],
    )
    return pl.pallas_call(
        _moe_kernel,
        grid_spec=grid_spec,
        out_shape=jax.ShapeDtypeStruct((N_EXPERTS * cap, D_MODEL), bf16),
        compiler_params=_params("arbitrary"),
        name="moe",
    )(te, tb, tv, xs, w_gate, w_up, w_down)


def _final_kernel(tab_ref, x_ref, pos_ref, ys_hbm, mod_ref, g_ref, b_ref, o_ref, y_vmem, sem, *, j_off):
    t, t_next, has_next, _ = _tile_id(j_off)
    step = pl.program_id(0) * pl.num_programs(1) + pl.program_id(1)
    f = _combine_experts(tab_ref, pos_ref, ys_hbm, y_vmem, sem, step, t, t_next, has_next)
    o_ref[...] = _layer_norm(ALPHA * x_ref[...] + mod_ref[5:6, :] * f, g_ref[...], b_ref[...])


def _final_ln(x, moe, mod, ln_g, ln_b):
    B, L, _ = x.shape
    seq = L - CTX_LEN
    skip = CTX_LEN // ROW_TILE
    tab, pos, ys = moe
    lat = pl.BlockSpec((None, ROW_TILE, D_MODEL), lambda b, j: (b, j + skip, 0))
    return pl.pallas_call(
        functools.partial(_final_kernel, j_off=skip),
        grid=(B, seq // ROW_TILE),
        in_specs=[pl.BlockSpec(memory_space=pltpu.SMEM), lat, _pos_spec(skip), pl.BlockSpec(memory_space=pl.ANY),
                  pl.BlockSpec((None, None, 6, D_MODEL), lambda b, j: (b, 1, 0, 0)),
                  _vec_spec(DEPTH - 1), _vec_spec(DEPTH - 1)],
        out_specs=_row_spec(D_MODEL),
        out_shape=jax.ShapeDtypeStruct((B, seq, D_MODEL), f32),
        scratch_shapes=_combine_scratch(),
        compiler_params=_params("arbitrary", "arbitrary"),
        name="final_ln",
    )(tab, x, pos, ys, mod, ln_g, ln_b)


def kernel(x, c, ctx, c_ctx, w_ada, b_ada, w_in, lb_logits, a_norm_g, rpb, w_out, ln1_g, ln1_b,
           w_router, b_router, w_gate, w_up, w_down, ln2_g, ln2_b):
    B, seq, D = x.shape
    L = CTX_LEN + seq
    xs = jnp.concatenate([ctx, x], axis=1)

    cc = jnp.concatenate([c, c_ctx[None, :], jnp.zeros((16 - B - 1, D), f32)], axis=0)
    mods = _ada_mods(cc, w_ada, b_ada)
    lat_mod = mods[:, :B].reshape(DEPTH, B, 1, 6, D)
    ctx_mod = jnp.broadcast_to(mods[:, B].reshape(DEPTH, 1, 1, 6, D), (DEPTH, B, 1, 6, D))
    mod_all = jnp.concatenate([ctx_mod, lat_mod], axis=2)

    sm = jax.nn.softmax(lb_logits.astype(f32), axis=1)
    lower = jnp.cumsum(sm, axis=1) - sm[:, :1]

    tri_np, own_np, sgn_np = _scan_constants()
    scan_consts = (jnp.asarray(tri_np, bf16), jnp.asarray(own_np), jnp.asarray(sgn_np))
    cos_np, sin_np = _rope_tables(seq)
    cos_t, sin_t = jnp.asarray(cos_np), jnp.asarray(sin_np)
    mask = jnp.asarray(_window_mask())
    bias_t = _bias_tables(rpb.astype(f32) * LOG2E)

    w_in_b = w_in.astype(bf16)
    w_out_b = w_out.astype(bf16)
    wr_hi = w_router.T.astype(bf16)
    wr_res = w_router.T - wr_hi.astype(f32)
    wr_mid = wr_res.astype(bf16)
    wr_lo = (wr_res - wr_mid.astype(f32)).astype(bf16)
    w_router_t = jnp.concatenate([wr_hi, wr_mid, wr_lo], axis=0)
    b_router_c = b_router.reshape(N_EXPERTS, 1)
    ln1_g3, ln1_b3 = ln1_g.reshape(DEPTH, 1, D), ln1_b.reshape(DEPTH, 1, D)
    ln2_g3, ln2_b3 = ln2_g.reshape(DEPTH, 1, D), ln2_b.reshape(DEPTH, 1, D)

    tri_np, low_np = _dispatch_constants()
    tri, low = jnp.asarray(tri_np, bf16), jnp.asarray(low_np, bf16)
    n_tiles = B * (L // ROW_TILE)

    moe = None
    for l in range(DEPTH):
        pmod = mod_all[l - 1] if l > 0 else None
        xs, u = _in_proj(l, xs, moe, pmod, ln2_g3, ln2_b3, mod_all[l], w_in_b)
        a = _hgrn(u, lower[0, l], lower[1, l], a_norm_g[l], scan_consts)
        bmix = _na(l, u, cos_t, sin_t, bias_t, mask)
        xs, pos, tab, cnt, x_sorted = _out_proj(l, a, bmix, xs, mod_all[l], w_out_b, ln1_g3, ln1_b3,
                                                w_router_t, b_router_c, tri, low)
        y_sorted = _moe(l, x_sorted, cnt, n_tiles, w_gate, w_up, w_down)
        moe = (tab, pos, y_sorted)
    return _final_ln(xs, moe, mod_all[DEPTH - 1], ln2_g3, ln2_b3)
```

```python
import functools

import numpy as np
import jax
import jax.numpy as jnp
from jax import lax
from jax.experimental import pallas as pl
from jax.experimental.pallas import tpu as pltpu

D_MODEL = 1024
DEPTH = 4
GRID_W = 64
CTX_LEN = 256
A_HEADS = 4
A_DK = 128
A_WIDTH = A_HEADS * A_DK
B_HEADS = 8
B_HD = 64
B_WIDTH = B_HEADS * B_HD
D_IN = 5 * A_WIDTH + 3 * B_WIDTH
NA_ROWS = 8
NA_COLS = 16
ROPE_BASE = 10000.0
N_EXPERTS = 16
N_GROUPS = 4
EXPERTS_PER_GROUP = N_EXPERTS // N_GROUPS
D_EXPERT = 512
ALPHA = (2 * DEPTH) ** 0.25
LN_EPS = 1e-5
RMS_EPS = 1e-6
F_FLOOR = 1e-6
MASK_VALUE = -1e30
LOG2E = 1.4426950408889634

LANES = 128
ROW_TILE = 256
XS_BLK = 16
XS_ROWS = 2 * ROW_TILE + N_EXPERTS * XS_BLK
XS_W = D_MODEL + 2 * LANES
DRAIN_GROUP = 8
EXP_TILE = 1024
MOE_CHUNK = 256
SCAN_CHUNK = 128
SCAN_LEVELS = (64, 32, 16, 8, 4)
SCAN_BASE = 4
NA_GROUP = 4
VMEM_LIMIT = 56 * 1024 * 1024

f32 = jnp.float32
bf16 = jnp.bfloat16

_NT = (((1,), (1,)), ((), ()))
_TN = (((0,), (0,)), ((), ()))


def _params(*sem):
    return pltpu.CompilerParams(dimension_semantics=sem, vmem_limit_bytes=VMEM_LIMIT)


def _sigmoid(z, sign=1.0):
    return 1.0 / (1.0 + jnp.exp2(z * (-sign * LOG2E)))


def _layer_norm(h, g, b):
    mu = jnp.mean(h, axis=-1, keepdims=True)
    d = h - mu
    var = jnp.mean(d * d, axis=-1, keepdims=True)
    return d * lax.rsqrt(var + LN_EPS) * g + b


def _ada_kernel(c_ref, w_ref, b_ref, o_ref):
    c = c_ref[...]
    o_ref[...] = jnp.dot(c * _sigmoid(c), w_ref[...], preferred_element_type=f32,
                         precision=lax.Precision.HIGHEST) + b_ref[...]


def _ada_mods(cc, w_ada, b_ada):
    n_col = 4
    tn = 6 * D_MODEL // n_col
    return pl.pallas_call(
        _ada_kernel,
        grid=(DEPTH, n_col),
        in_specs=[
            pl.BlockSpec((16, D_MODEL), lambda l, j: (0, 0)),
            pl.BlockSpec((None, D_MODEL, tn), lambda l, j: (l, 0, j)),
            pl.BlockSpec((None, 1, tn), lambda l, j: (l, 0, j)),
        ],
        out_specs=pl.BlockSpec((None, 16, tn), lambda l, j: (l, 0, j)),
        out_shape=jax.ShapeDtypeStruct((DEPTH, 16, 6 * D_MODEL), f32),
        compiler_params=_params("parallel", "parallel"),
        name="ada_mods",
    )(cc, w_ada, b_ada.reshape(DEPTH, 1, 6 * D_MODEL))


def _drain(copy_of_rows, n_blocks):
    def grouped(i, carry):
        copy_of_rows(DRAIN_GROUP * XS_BLK).wait()
        return carry

    def single(i, carry):
        copy_of_rows(XS_BLK).wait()
        return carry

    lax.fori_loop(0, n_blocks // DRAIN_GROUP, grouped, 0)
    lax.fori_loop(0, lax.rem(n_blocks, DRAIN_GROUP), single, 0)


def _tile_id(j_off):
    b, j = pl.program_id(0), pl.program_id(1)
    nj = pl.num_programs(1)
    tiles_per_batch = nj + j_off
    t = b * tiles_per_batch + j + j_off
    last_j = j == nj - 1
    t_next = jnp.where(last_j, (b + 1) * tiles_per_batch + j_off, t + 1)
    has_next = jnp.logical_not(jnp.logical_and(last_j, b == pl.num_programs(0) - 1))
    return t, t_next, has_next, jnp.logical_and(b == 0, j == 0)


def _combine_experts(tab_ref, pos_ref, ys_hbm, y_vmem, sem, step, t, t_next, has_next):
    is_first = step == 0
    slot = lax.rem(step, 2)

    def block_copy(src_row, dst_row, sl, rows=XS_BLK):
        return pltpu.make_async_copy(ys_hbm.at[pl.ds(pl.multiple_of(src_row, XS_BLK), rows), :],
                                     y_vmem.at[sl, pl.ds(pl.multiple_of(dst_row, XS_BLK), rows), :],
                                     sem.at[sl])

    def fetch(tile, sl):
        y_vmem[sl] = jnp.zeros((XS_ROWS, D_MODEL), bf16)
        for e in range(N_EXPERTS):
            base, start, nblk = tab_ref[tile, 3 * e], tab_ref[tile, 3 * e + 1], tab_ref[tile, 3 * e + 2]

            def body(i, carry):
                block_copy(base + i * XS_BLK, start + i * XS_BLK, sl).start()
                return carry

            lax.fori_loop(0, nblk, body, 0)

    def wait_all(tile, sl):
        total = tab_ref[tile, 2]
        for e in range(1, N_EXPERTS):
            total = total + tab_ref[tile, 3 * e + 2]
        _drain(lambda rows: block_copy(0, 0, sl, rows), total)

    @pl.when(is_first)
    def _():
        fetch(t, slot)

    @pl.when(has_next)
    def _():
        fetch(t_next, 1 - slot)

    wait_all(t, slot)
    y = y_vmem[slot]
    r = lax.broadcasted_iota(jnp.int32, (XS_ROWS, ROW_TILE), 0)
    perm = jnp.where(r == pos_ref[0:1, :], 1.0, jnp.where(r == pos_ref[1:2, :], 1.0, 0.0)).astype(bf16)
    return lax.dot_general(perm, y, _TN, preferred_element_type=f32)


def _in_kernel(*refs, with_ln):
    if with_ln:
        (tab_ref, x_ref, pos_ref, ys_hbm, pmod_ref, g_ref, b_ref, mod_ref, w_ref,
         xo_ref, u_ref, y_vmem, sem) = refs
        t, t_next, has_next, _ = _tile_id(0)
        step = pl.program_id(0) * pl.num_programs(1) + pl.program_id(1)
        f = _combine_experts(tab_ref, pos_ref, ys_hbm, y_vmem, sem, step, t, t_next, has_next)
        h = ALPHA * x_ref[...] + pmod_ref[5:6, :] * f
        x = _layer_norm(h, g_ref[...], b_ref[...])
        xo_ref[...] = x
    else:
        ctx_ref, lat_ref, mod_ref, w_ref, xo_ref, u_ref = refs
        x = jnp.where(pl.program_id(1) == 0, ctx_ref[...], lat_ref[...])
        xo_ref[...] = x
    xm = x * (1.0 + mod_ref[1:2, :]) + mod_ref[0:1, :]
    u_ref[...] = jnp.dot(xm.astype(bf16), w_ref[...], preferred_element_type=f32)


def _mod_spec(j_off=0):
    return pl.BlockSpec((None, None, 6, D_MODEL), lambda b, j: (b, jnp.minimum(j + j_off, 1), 0, 0))


def _row_spec(width, j_off=0):
    return pl.BlockSpec((None, ROW_TILE, width), lambda b, j: (b, j + j_off, 0))


def _vec_spec(l):
    return pl.BlockSpec((None, 1, D_MODEL), lambda b, j: (l, 0, 0))


def _pos_spec(j_off=0):
    return pl.BlockSpec((None, 2, ROW_TILE), lambda b, j: (b, 0, j + j_off))


def _combine_scratch():
    return [pltpu.VMEM((2, XS_ROWS, D_MODEL), bf16), pltpu.SemaphoreType.DMA((2,))]


def _in_proj(l, x, moe, pmod, ln_g, ln_b, mod, w_in):
    if moe is None:
        B, L = x[0].shape[0], x[0].shape[1] + x[1].shape[1]
    else:
        B, L, _ = x.shape
    w_spec = pl.BlockSpec((None, D_MODEL, D_IN), lambda b, j: (l, 0, 0))
    u_shape = jax.ShapeDtypeStruct((B, L, D_IN), f32)
    if moe is None:
        ctx, lat = x
        skip = CTX_LEN // ROW_TILE
        return pl.pallas_call(
            functools.partial(_in_kernel, with_ln=False),
            grid=(B, L // ROW_TILE),
            in_specs=[pl.BlockSpec((None, ROW_TILE, D_MODEL), lambda b, j: (b, 0, 0)),
                      pl.BlockSpec((None, ROW_TILE, D_MODEL), lambda b, j: (b, jnp.maximum(j - skip, 0), 0)),
                      _mod_spec(), w_spec],
            out_specs=[_row_spec(D_MODEL), _row_spec(D_IN)],
            out_shape=[jax.ShapeDtypeStruct((B, L, D_MODEL), f32), u_shape],
            compiler_params=_params("parallel", "parallel"),
            name="in_proj0",
        )(ctx, lat, mod, w_in)
    tab, pos, ys = moe
    return pl.pallas_call(
        functools.partial(_in_kernel, with_ln=True),
        grid=(B, L // ROW_TILE),
        in_specs=[pl.BlockSpec(memory_space=pltpu.SMEM), _row_spec(D_MODEL), _pos_spec(),
                  pl.BlockSpec(memory_space=pl.ANY), _mod_spec(), _vec_spec(l - 1), _vec_spec(l - 1),
                  _mod_spec(), w_spec],
        out_specs=[_row_spec(D_MODEL), _row_spec(D_IN)],
        out_shape=[jax.ShapeDtypeStruct(x.shape, f32), u_shape],
        scratch_shapes=_combine_scratch(),
        compiler_params=_params("arbitrary", "arbitrary"),
        name="in_proj",
    )(tab, x, pos, ys, pmod, ln_g, ln_b, mod, w_in)


def _scan_constants():
    C = SCAN_CHUNK
    p = np.arange(C)
    tri = (p[None, :] <= p[:, None]).astype(np.float32)
    owner = np.full((C, C), -1, np.int32)
    signs = []
    for li, m in enumerate(SCAN_LEVELS):
        same = (p[:, None] // (2 * m)) == (p[None, :] // (2 * m))
        owner[same & ((p[:, None] // m) % 2 == 1) & ((p[None, :] // m) % 2 == 0)] = li
        signs.append(np.where((p // m) % 2 == 1, 1.0, -1.0))
    owner[((p[:, None] // SCAN_BASE) == (p[None, :] // SCAN_BASE)) & (p[None, :] <= p[:, None])] = len(SCAN_LEVELS)
    sgn = np.broadcast_to(np.stack(signs)[:, :, None], (len(SCAN_LEVELS), C, LANES)).astype(np.float32)
    return (np.stack([tri, tri[::-1, ::-1]]), np.stack([owner, owner[::-1, ::-1]]),
            np.stack([sgn, sgn[:, ::-1]]))


def _hgrn_kernel(q_ref, ff_ref, fb_ref, v_ref, g_ref, lbf_ref, lbb_ref, ng_ref, t_ref, own_ref, sgn_ref,
                 a_ref, of_scr, ob_scr, *, n_ctx, n_all):
    C = SCAN_CHUNK
    n_lv = len(SCAN_LEVELS)


    def gates(rows, z_ref, lb, d):
        z = z_ref[rows, :]
        lf = jnp.log2(jnp.maximum(lb + (1.0 - lb) * _sigmoid(z), F_FLOOR))
        k = (1.0 - lb) * _sigmoid(z, -1.0)
        qz = q_ref[rows, :]
        q = qz * _sigmoid(qz)
        v = v_ref[rows, :].astype(bf16)
        hi = lf.astype(bf16)
        r1 = lf - hi.astype(f32)
        mid = r1.astype(bf16)
        lo = (r1 - mid.astype(f32)).astype(bf16)
        d3 = jnp.dot(t_ref[d], jnp.concatenate([hi, mid, lo], axis=1), preferred_element_type=f32)
        cum = d3[:, :LANES] + d3[:, LANES:2 * LANES] + d3[:, 2 * LANES:]
        return q, k, v, cum

    def intra(q, k, cum, d):
        last = cum[C - 1:C, :] if d == 0 else cum[0:1, :]

        def rel(block, ref_rows):
            blocks = cum.reshape(C // block, block, LANES)
            if len(ref_rows) == 1:
                ref = blocks[:, ref_rows[0]:ref_rows[0] + 1, :]
            else:
                first = lax.broadcasted_iota(jnp.int32, (1, block, 1), 1) < block // 2
                ref = jnp.where(first, blocks[:, ref_rows[0]:ref_rows[0] + 1, :],
                                blocks[:, ref_rows[1]:ref_rows[1] + 1, :])
            return (blocks - ref).reshape(C, LANES)

        def to_boundary(m):
            blocks = cum.reshape(C // (2 * m), 2 * m, LANES)
            lo, hi = blocks[:, :m, :], blocks[:, m:, :]
            if d == 0:
                ref = lo[:, m - 1:m, :]
                parts = (ref - lo, hi - ref)
            else:
                ref = hi[:, 0:1, :]
                parts = (lo - ref, ref - hi)
            return jnp.concatenate(parts, axis=1).reshape(C, LANES)

        q_in = (q * jnp.exp2(cum)).astype(bf16)
        own = own_ref[d]
        scores = jnp.zeros((C, C), f32)
        for li in range(n_lv + 1):
            m = SCAN_LEVELS[li] if li < n_lv else 0
            if m and m % 8 == 0:
                def halves(x):
                    blocks = x.reshape(C // (2 * m), 2 * m, x.shape[-1])
                    return blocks[:, :m, :], blocks[:, m:, :]

                def join(lo, hi):
                    return jnp.concatenate([lo, hi], axis=1).reshape(C, lo.shape[-1])

                (q_lo, q_hi), (k_lo, k_hi) = halves(q), halves(k)
                mixed = join(k_lo, q_hi) if d == 0 else join(q_lo, k_hi)
                z = (mixed * jnp.exp2(to_boundary(m))).astype(bf16)
                a = lax.dot_general(z, z, _NT, preferred_element_type=f32)
                (a_lo, a_hi), (s_lo, s_hi), (o_lo, o_hi) = halves(a), halves(scores), halves(own)
                if d == 0:
                    scores = join(s_lo, jnp.where(o_hi == li, a_hi, s_hi))
                else:
                    scores = join(jnp.where(o_lo == li, a_lo, s_lo), s_hi)
                continue
            if m:
                wq = jnp.exp2(rel(2 * m, (m - 1,) if d == 0 else (m,)) * sgn_ref[d, li])
                wk = wq
            else:
                nb = SCAN_BASE
                dl = rel(2 * nb, (0, nb) if d == 0 else (nb - 1, 2 * nb - 1))
                wq = jnp.exp2(dl)
                wk = jnp.exp2(-dl)
            a = lax.dot_general((q * wq).astype(bf16), (k * wk).astype(bf16), _NT,
                                preferred_element_type=f32)
            scores = jnp.where(own == li, a, scores)
        kdec = (k * jnp.exp2(last - cum)).astype(bf16)
        return q_in, scores.astype(bf16), kdec, jnp.exp2(last)

    def finish(pending, S, o_scr):
        q_in, scores, kdec, decay, v, row0 = pending
        o = (lax.dot_general(q_in, S.astype(bf16), _NT, preferred_element_type=f32)
             + jnp.dot(scores, v, preferred_element_type=f32))
        o_scr[pl.ds(pl.multiple_of(row0, C), C), :] = o
        return S * decay + lax.dot_general(v, kdec, _TN, preferred_element_type=f32)

    lbf = lbf_ref[...]
    lbb = lbb_ref[...]

    def bwd_chunk(i):
        return jnp.where(i < n_ctx, n_ctx - 1 - i, n_all - 1 - (i - n_ctx))

    def scan_body(i, carry):
        s_f, s_b, pend_f, pend_b = carry
        row_f = i * C
        row_b = bwd_chunk(i) * C
        s_f = finish(pend_f, s_f, of_scr)
        s_b = finish(pend_b, s_b, ob_scr)
        qf, kf, vf, cum_f = gates(pl.ds(pl.multiple_of(row_f, C), C), ff_ref, lbf, 0)
        qb, kb, vb, cum_b = gates(pl.ds(pl.multiple_of(row_b, C), C), fb_ref, lbb, 1)
        pend_f = intra(qf, kf, cum_f, 0) + (vf, row_f)
        pend_b = intra(qb, kb, cum_b, 1) + (vb, row_b)
        return s_f, s_b, pend_f, pend_b

    def idle(row0):
        zc = jnp.zeros((C, LANES), bf16)
        return (zc, jnp.zeros((C, C), bf16), zc, jnp.ones((1, LANES), f32), zc, row0)

    zero = jnp.zeros((A_DK, A_DK), f32)
    init = (zero, zero, idle(jnp.int32(0)), idle(bwd_chunk(jnp.int32(0)) * C))
    s_f, s_b, pend_f, pend_b = lax.fori_loop(0, n_all, scan_body, init, unroll=3)
    finish(pend_f, s_f, of_scr)
    finish(pend_b, s_b, ob_scr)

    ng = ng_ref[...]

    def readout_body(i, carry):
        rows = pl.ds(pl.multiple_of(i * ROW_TILE, ROW_TILE), ROW_TILE)
        o = of_scr[rows, :] + ob_scr[rows, :]
        o = o * lax.rsqrt(jnp.mean(o * o, axis=-1, keepdims=True) + RMS_EPS) * ng
        gz = g_ref[rows, :]
        a_ref[rows, :] = (o * (gz * _sigmoid(gz))).astype(a_ref.dtype)
        return carry

    lax.fori_loop(0, n_all * C // ROW_TILE, readout_body, 0)


def _hgrn(u, lb_f, lb_b, norm_g, scan_consts):
    B, L, _ = u.shape
    hb = A_WIDTH // LANES

    def col(sec):
        return pl.BlockSpec((None, L, LANES), lambda b, h: (b, 0, sec * hb + h))

    def whole(arr):
        return pl.BlockSpec(arr.shape, lambda b, h: (0,) * arr.ndim)

    lb_spec = pl.BlockSpec((None, 1, LANES), lambda b, h: (h, 0, 0))
    return pl.pallas_call(
        functools.partial(_hgrn_kernel, n_ctx=CTX_LEN // SCAN_CHUNK, n_all=L // SCAN_CHUNK),
        grid=(B, A_HEADS),
        in_specs=[col(0), col(1), col(2), col(3), col(4), lb_spec, lb_spec,
                  pl.BlockSpec((1, LANES), lambda b, h: (0, 0))] + [whole(a) for a in scan_consts],
        out_specs=pl.BlockSpec((None, L, LANES), lambda b, h: (b, 0, h)),
        out_shape=jax.ShapeDtypeStruct((B, L, A_WIDTH), bf16),
        scratch_shapes=[pltpu.VMEM((L, LANES), f32), pltpu.VMEM((L, LANES), f32)],
        compiler_params=_params("parallel", "parallel"),
        name="hgrn2",
    )(u, u, u, u, u, lb_f.reshape(A_HEADS, 1, LANES), lb_b.reshape(A_HEADS, 1, LANES),
      norm_g.reshape(1, LANES), *scan_consts)


def _rope_tables(seq):
    t = np.arange(seq)
    pos_r = (t // GRID_W).astype(np.float32)
    pos_c = (t % GRID_W).astype(np.float32)
    quarter = B_HD // 4
    inv = (ROPE_BASE ** (-np.arange(quarter, dtype=np.float32) / quarter)).astype(np.float32)
    lane = np.arange(LANES)
    in_head = lane % B_HD
    pos = np.where((in_head < B_HD // 2)[None, :], pos_r[:, None], pos_c[:, None])
    ang = (pos * inv[lane % quarter][None, :]).astype(np.float32)
    sign = np.where((lane % (2 * quarter)) < quarter, -1.0, 1.0)
    return np.cos(ang).astype(np.float32), (np.sin(ang) * sign[None, :]).astype(np.float32)


def _window_mask():
    c = np.arange(GRID_W)
    cs = np.clip(c - NA_COLS // 2, 0, GRID_W - NA_COLS)
    in_win = (c[None, :] >= cs[:, None]) & (c[None, :] < cs[:, None] + NA_COLS)
    return np.tile(in_win, (1, NA_ROWS)).astype(np.float32)


def _bias_tables(rpb):
    c = np.arange(GRID_W)
    dc = np.clip(c[None, :] - c[:, None], -(NA_COLS - 1), NA_COLS - 1) + (NA_COLS - 1)
    pick = (np.arange(2 * NA_COLS - 1)[:, None, None] == dc[None]).astype(np.float32)
    cols = jnp.einsum('lhdk,kcx->lhdcx', rpb, jnp.asarray(pick), precision=lax.Precision.HIGHEST)
    n_rel = 2 * NA_ROWS - 1
    n_win = NA_ROWS * GRID_W
    t = pl.pallas_call(
        _bias_expand_kernel,
        grid=(DEPTH * B_HEADS,),
        in_specs=[pl.BlockSpec((None, n_rel, GRID_W, GRID_W), lambda i: (i, 0, 0, 0))],
        out_specs=pl.BlockSpec((None, NA_ROWS, GRID_W, n_win), lambda i: (i, 0, 0, 0)),
        out_shape=jax.ShapeDtypeStruct((DEPTH * B_HEADS, NA_ROWS, GRID_W, n_win), f32),
        compiler_params=_params("parallel"),
        name="bias_expand",
    )(cols.reshape(DEPTH * B_HEADS, n_rel, GRID_W, GRID_W))
    return t.reshape(DEPTH, B_HEADS, NA_ROWS, GRID_W, n_win)


def _bias_expand_kernel(c_ref, o_ref):
    per_vreg = LANES // GRID_W
    for o in range(NA_ROWS):
        for g in range(NA_ROWS // per_vreg):
            parts = [c_ref[o + g * per_vreg + p] for p in range(per_vreg)]
            o_ref[o, :, g * LANES:(g + 1) * LANES] = jnp.concatenate(parts, axis=1)


def _na_kernel(q_ref, k_ref, v_ref, cos_ref, sin_ref, bias_ref, mask_ref, o_ref,
               qr_scr, kr_scr, qp_scr, kc_scr, v_scr, *, seq, rows_grid):
    hd = B_HD
    scale = hd ** -0.5 * LOG2E
    n_win = NA_ROWS * GRID_W
    lane = lax.broadcasted_iota(jnp.int32, (ROW_TILE, LANES), 1)
    first_half = (lane % (hd // 2)) < (hd // 4)
    head0 = lane < hd

    def swap_halves(x):
        return jnp.where(first_half, pltpu.roll(x, LANES - hd // 4, axis=1), pltpu.roll(x, hd // 4, axis=1))

    def store_per_head(dst, rows, x):
        dst[0, rows, :] = jnp.where(head0, x, 0.0).astype(bf16)
        dst[1, rows, :] = jnp.where(head0, 0.0, x).astype(bf16)

    ctx_rows = pl.ds(0, CTX_LEN)
    store_per_head(qp_scr, ctx_rows, q_ref[ctx_rows, :] * scale)
    kc_scr[...] = k_ref[ctx_rows, :].astype(bf16)
    v_scr[ctx_rows, :] = v_ref[ctx_rows, :].astype(bf16)

    def prep(i, carry):
        lrows = pl.ds(pl.multiple_of(i * ROW_TILE, ROW_TILE), ROW_TILE)
        rows = pl.ds(pl.multiple_of(CTX_LEN + i * ROW_TILE, ROW_TILE), ROW_TILE)
        q = q_ref[rows, :] * scale
        k = k_ref[rows, :]
        cos = cos_ref[lrows, :]
        sin = sin_ref[lrows, :]
        store_per_head(qp_scr, rows, q)
        v_scr[rows, :] = v_ref[rows, :].astype(bf16)
        store_per_head(qr_scr, lrows, q * cos + swap_halves(q) * sin)
        kr_scr[lrows, :] = (k * cos + swap_halves(k) * sin).astype(bf16)
        return carry

    lax.fori_loop(0, seq // ROW_TILE, prep, 0)

    def pick_heads(o0, o1):
        keep0 = lax.broadcasted_iota(jnp.int32, o0.shape, 1) < hd
        return jnp.where(keep0, o0, o1)

    outs = []
    for h in range(2):
        s = lax.dot_general(qp_scr[h, ctx_rows, :], kc_scr[...], _NT, preferred_element_type=f32)
        e = jnp.exp2(s - jnp.max(s, axis=-1, keepdims=True))
        o = jnp.dot(e.astype(bf16), v_scr[ctx_rows, :], preferred_element_type=f32)
        outs.append(o / jnp.sum(e, axis=-1, keepdims=True))
    o_ref[ctx_rows, :] = pick_heads(*outs).astype(o_ref.dtype)

    def rows_body(it, carry):
        chains = []
        for g in range(NA_GROUP):
            r = it * NA_GROUP + g
            rs = jnp.clip(r - NA_ROWS // 2, 0, rows_grid - NA_ROWS)
            for h in range(2):
                chains.append(dict(
                    h=h, off=rs - r + (NA_ROWS - 1),
                    qrows=pl.ds(pl.multiple_of(r * GRID_W, GRID_W), GRID_W),
                    krows=pl.ds(pl.multiple_of(rs * GRID_W, GRID_W), n_win),
                    arows=pl.ds(pl.multiple_of(CTX_LEN + r * GRID_W, GRID_W), GRID_W),
                    vrows=pl.ds(pl.multiple_of(CTX_LEN + rs * GRID_W, GRID_W), n_win)))
        for c in chains:
            h = c["h"]
            c["s_lat"] = lax.dot_general(qr_scr[h, c["qrows"], :], kr_scr[c["krows"], :], _NT,
                                         preferred_element_type=f32)
            c["s_ctx"] = lax.dot_general(qp_scr[h, c["arows"], :], kc_scr[...], _NT, preferred_element_type=f32)
        in_win = mask_ref[...] > 0.5
        for c in chains:
            s_lat = jnp.where(in_win, c["s_lat"] + bias_ref[c["h"], c["off"]], MASK_VALUE)
            m = jnp.maximum(jnp.max(s_lat, axis=-1, keepdims=True), jnp.max(c["s_ctx"], axis=-1, keepdims=True))
            e_lat = jnp.exp2(s_lat - m)
            e_ctx = jnp.exp2(c["s_ctx"] - m)
            c["den"] = jnp.sum(e_lat, axis=-1, keepdims=True) + jnp.sum(e_ctx, axis=-1, keepdims=True)
            c["e_lat"] = e_lat.astype(bf16)
            c["e_ctx"] = e_ctx.astype(bf16)
        for c in chains:
            c["o"] = (jnp.dot(c["e_lat"], v_scr[c["vrows"], :], preferred_element_type=f32)
                      + jnp.dot(c["e_ctx"], v_scr[ctx_rows, :], preferred_element_type=f32))
        for g in range(NA_GROUP):
            c0, c1 = chains[2 * g:2 * g + 2]
            o_ref[c0["arows"], :] = pick_heads(c0["o"] / c0["den"], c1["o"] / c1["den"]).astype(o_ref.dtype)
        return carry

    lax.fori_loop(0, rows_grid // NA_GROUP, rows_body, 0)


def _na(l, u, cos_t, sin_t, bias_t, mask):
    B, L, _ = u.shape
    seq = L - CTX_LEN
    first = 5 * A_WIDTH // LANES
    pairs = B_WIDTH // LANES

    def col(sec):
        return pl.BlockSpec((None, L, LANES), lambda b, h: (b, 0, first + sec * pairs + h))

    n_win = NA_ROWS * GRID_W
    return pl.pallas_call(
        functools.partial(_na_kernel, seq=seq, rows_grid=seq // GRID_W),
        grid=(B, pairs),
        in_specs=[col(0), col(1), col(2),
                  pl.BlockSpec((seq, LANES), lambda b, h: (0, 0)),
                  pl.BlockSpec((seq, LANES), lambda b, h: (0, 0)),
                  pl.BlockSpec((None, 2, NA_ROWS, GRID_W, n_win), lambda b, h: (l, h, 0, 0, 0)),
                  pl.BlockSpec((GRID_W, n_win), lambda b, h: (0, 0))],
        out_specs=pl.BlockSpec((None, L, LANES), lambda b, h: (b, 0, h)),
        out_shape=jax.ShapeDtypeStruct((B, L, B_WIDTH), bf16),
        scratch_shapes=[pltpu.VMEM((2, seq, LANES), bf16), pltpu.VMEM((seq, LANES), bf16),
                        pltpu.VMEM((2, L, LANES), bf16), pltpu.VMEM((CTX_LEN, LANES), bf16),
                        pltpu.VMEM((L, LANES), bf16)],
        compiler_params=_params("parallel", "parallel"),
        name="na_attn",
    )(u, u, u, cos_t, sin_t, bias_t, mask)


def _route(logits, b_router, sel_ref, gate_ref):
    aff = _sigmoid(logits)
    sel = aff + b_router
    s = [sel[e:e + 1, :] for e in range(N_EXPERTS)]
    one = lambda cond: jnp.where(cond, 1.0, 0.0)
    scores = []
    for g in range(N_GROUPS):
        s0, s1, s2, s3 = s[g * EXPERTS_PER_GROUP:(g + 1) * EXPERTS_PER_GROUP]
        hi01, lo01 = jnp.maximum(s0, s1), jnp.minimum(s0, s1)
        hi23, lo23 = jnp.maximum(s2, s3), jnp.minimum(s2, s3)
        scores.append(jnp.maximum(hi01, hi23) + jnp.maximum(jnp.minimum(hi01, hi23), jnp.maximum(lo01, lo23)))
    weights = []
    chosen = []
    for g in range(N_GROUPS):
        gsel = None
        for g2 in range(N_GROUPS):
            if g2 == g:
                continue
            t = one(scores[g2] < scores[g]) if g2 < g else one(scores[g2] <= scores[g])
            gsel = t if gsel is None else gsel * t
        for i in range(EXPERTS_PER_GROUP):
            e = g * EXPERTS_PER_GROUP + i
            rank = None
            for j in range(EXPERTS_PER_GROUP):
                if j == i:
                    continue
                e2 = g * EXPERTS_PER_GROUP + j
                t = one(s[e2] >= s[e]) if j < i else one(s[e2] > s[e])
                rank = t if rank is None else rank + t
            chosen.append(gsel * one(rank < 1.5))
            weights.append(chosen[-1] * aff[e:e + 1, :])
    total = weights[0]
    for w in weights[1:]:
        total = total + w
    inv = 1.0 / total
    for e in range(N_EXPERTS):
        sel_ref[e:e + 1, :] = chosen[e]
        gate_ref[e:e + 1, :] = weights[e] * inv


def _out_kernel(a_ref, b_ref, x_ref, mod_ref, wa_ref, wb_ref, g_ref, bt_ref, wr_ref, br_ref, tri_ref, low_ref,
                x1_ref, pos_ref, tab_ref, cnt_ref, xs_hbm,
                sel_scr, gate_scr, meta_v, meta_s, xs_vmem, sem, cnt_smem, nblk_smem, *, cap, j_off):
    t, _, has_next, is_first = _tile_id(j_off)
    slot = lax.rem(pl.program_id(0) * pl.num_programs(1) + pl.program_id(1), 2)

    @pl.when(is_first)
    def _():
        for e in range(N_EXPERTS):
            cnt_smem[e] = 0
        nblk_smem[0] = 0
        nblk_smem[1] = 0

    y = (jnp.dot(a_ref[...], wa_ref[...], preferred_element_type=f32)
         + jnp.dot(b_ref[...], wb_ref[...], preferred_element_type=f32))
    x1 = _layer_norm(ALPHA * x_ref[...] + mod_ref[2:3, :] * y, g_ref[...], bt_ref[...])
    x1_ref[...] = x1
    xm = x1 * (1.0 + mod_ref[4:5, :]) + mod_ref[3:4, :]
    xm_hi = xm.astype(bf16)
    xm_res = xm - xm_hi.astype(f32)
    xm_mid = xm_res.astype(bf16)
    xm_lo = (xm_res - xm_mid.astype(f32)).astype(bf16)
    E = N_EXPERTS
    r_hi = lax.dot_general(wr_ref[...], xm_hi, _NT, preferred_element_type=f32)
    r_mid = lax.dot_general(wr_ref[0:2 * E, :], xm_mid, _NT, preferred_element_type=f32)
    r_lo = lax.dot_general(wr_ref[0:E, :], xm_lo, _NT, preferred_element_type=f32)
    logits = ((r_hi[0:E] + r_mid[0:E]) + (r_hi[E:2 * E] + r_mid[E:2 * E])) + (r_hi[2 * E:3 * E] + r_lo)
    _route(logits, br_ref[...], sel_scr, gate_scr)

    sel = sel_scr[...]
    gate = gate_scr[...]
    sel_b = sel.astype(bf16)
    rank = jnp.dot(sel_b, tri_ref[...], preferred_element_type=f32)
    below = jnp.dot(low_ref[...], sel_b, preferred_element_type=f32)
    n = jnp.sum(sel, axis=1, keepdims=True)
    npad = jnp.broadcast_to(jnp.floor((n + (XS_BLK - 1)) * (1.0 / XS_BLK)) * XS_BLK, (N_EXPERTS, LANES))
    start = jnp.dot(low_ref[...], npad.astype(bf16), preferred_element_type=f32)
    pos = start[:, :1] + rank
    first = jnp.where(below < 0.5, sel, 0.0)
    second = sel - first
    pos0 = jnp.sum(first * pos, axis=0, keepdims=True).astype(jnp.int32)
    pos1 = jnp.sum(second * pos, axis=0, keepdims=True).astype(jnp.int32)
    g0 = jnp.sum(first * gate, axis=0, keepdims=True)
    g1 = jnp.sum(second * gate, axis=0, keepdims=True)
    pos_ref[0:1, :] = pos0
    pos_ref[1:2, :] = pos1
    meta_v[0] = start.astype(jnp.int32)
    meta_v[1] = (npad * (1.0 / XS_BLK)).astype(jnp.int32)
    pltpu.sync_copy(meta_v, meta_s)

    r = lax.broadcasted_iota(jnp.int32, (XS_ROWS, ROW_TILE), 0)
    hit0 = r == pos0
    hit1 = r == pos1
    perm = jnp.where(hit0, 1.0, jnp.where(hit1, 1.0, 0.0)).astype(bf16)
    gate_sorted = jnp.sum(jnp.where(hit0, g0, jnp.where(hit1, g1, 0.0)), axis=1, keepdims=True)
    xs = jnp.dot(perm, xm_hi, preferred_element_type=f32)

    def block_copy(src_row, dst_row, sl, rows=XS_BLK):
        return pltpu.make_async_copy(xs_vmem.at[sl, pl.ds(pl.multiple_of(src_row, XS_BLK), rows), :],
                                     xs_hbm.at[pl.ds(pl.multiple_of(dst_row, XS_BLK), rows), :],
                                     sem.at[sl])

    def wait_blocks(count, sl):
        _drain(lambda rows: block_copy(0, 0, sl, rows), count)

    wait_blocks(nblk_smem[slot], slot)
    xs_vmem[slot, :, :D_MODEL] = xs.astype(bf16)
    gate_hi = gate_sorted.astype(bf16)
    gate_lo = (gate_sorted - gate_hi.astype(f32)).astype(bf16)
    xs_vmem[slot, :, D_MODEL:D_MODEL + LANES] = jnp.broadcast_to(gate_hi, (XS_ROWS, LANES))
    xs_vmem[slot, :, D_MODEL + LANES:] = jnp.broadcast_to(gate_lo, (XS_ROWS, LANES))
    total = 0
    for e in range(N_EXPERTS):
        start_e = meta_s[0, e, 0]
        nblk_e = meta_s[1, e, 0]
        base = e * cap + cnt_smem[e]
        tab_ref[t, 3 * e] = base
        tab_ref[t, 3 * e + 1] = start_e
        tab_ref[t, 3 * e + 2] = nblk_e

        def body(i, carry):
            block_copy(start_e + i * XS_BLK, base + i * XS_BLK, slot).start()
            return carry

        lax.fori_loop(0, nblk_e, body, 0)
        cnt_smem[e] = cnt_smem[e] + nblk_e * XS_BLK
        total = total + nblk_e
    nblk_smem[slot] = total

    @pl.when(jnp.logical_not(has_next))
    def _():
        wait_blocks(nblk_smem[slot], slot)
        wait_blocks(nblk_smem[1 - slot], 1 - slot)
        for e in range(N_EXPERTS):
            cnt_ref[0, e] = cnt_smem[e]


def _dispatch_constants():
    t = np.arange(ROW_TILE)
    e = np.arange(N_EXPERTS)
    return ((t[:, None] < t[None, :]).astype(np.float32), (e[None, :] < e[:, None]).astype(np.float32))


def _out_proj(l, a, bmix, x, mod, w_out, ln_g, ln_b, w_router_t, b_router, tri, low, skip_ctx):
    B, L, _ = x.shape
    half = A_WIDTH
    n_tiles = B * (L // ROW_TILE)
    cap = _expert_capacity(n_tiles)
    smem = pl.BlockSpec(memory_space=pltpu.SMEM)
    j_off = CTX_LEN // ROW_TILE if skip_ctx else 0
    rows = functools.partial(_row_spec, j_off=j_off)
    return pl.pallas_call(
        functools.partial(_out_kernel, cap=cap, j_off=j_off),
        grid=(B, L // ROW_TILE - j_off),
        in_specs=[rows(A_WIDTH), rows(B_WIDTH), rows(D_MODEL), _mod_spec(j_off),
                  pl.BlockSpec((None, half, D_MODEL), lambda b, j: (l, 0, 0)),
                  pl.BlockSpec((None, half, D_MODEL), lambda b, j: (l, 1, 0)),
                  _vec_spec(l), _vec_spec(l),
                  pl.BlockSpec((3 * N_EXPERTS, D_MODEL), lambda b, j: (0, 0)),
                  pl.BlockSpec((N_EXPERTS, 1), lambda b, j: (0, 0)),
                  pl.BlockSpec((ROW_TILE, ROW_TILE), lambda b, j: (0, 0)),
                  pl.BlockSpec((N_EXPERTS, N_EXPERTS), lambda b, j: (0, 0))],
        out_specs=[rows(D_MODEL), _pos_spec(j_off), smem, smem, pl.BlockSpec(memory_space=pl.ANY)],
        out_shape=[jax.ShapeDtypeStruct(x.shape, f32), jax.ShapeDtypeStruct((B, 2, L), jnp.int32),
                   jax.ShapeDtypeStruct((n_tiles, 3 * N_EXPERTS), jnp.int32),
                   jax.ShapeDtypeStruct((1, N_EXPERTS), jnp.int32),
                   jax.ShapeDtypeStruct((N_EXPERTS * cap, XS_W), bf16)],
        scratch_shapes=[pltpu.VMEM((N_EXPERTS, ROW_TILE), f32), pltpu.VMEM((N_EXPERTS, ROW_TILE), f32),
                        pltpu.VMEM((2, N_EXPERTS, LANES), jnp.int32), pltpu.SMEM((2, N_EXPERTS, LANES), jnp.int32),
                        pltpu.VMEM((2, XS_ROWS, XS_W), bf16), pltpu.SemaphoreType.DMA((2,)),
                        pltpu.SMEM((N_EXPERTS,), jnp.int32), pltpu.SMEM((2,), jnp.int32)],
        compiler_params=_params("arbitrary", "arbitrary"),
        name="out_proj",
    )(a, bmix, x, mod, w_out, w_out, ln_g, ln_b, w_router_t, b_router, tri, low)


def _expert_capacity(n_tiles):
    rows = n_tiles * (ROW_TILE + XS_BLK)
    return -(-rows // EXP_TILE) * EXP_TILE


def _moe_kernel(te_ref, tb_ref, tv_ref, xs_ref, wg_ref, wu_ref, wd_ref, ys_ref, wg_b, wu_b, wd_b):
    i = pl.program_id(0)
    valid = tv_ref[i]
    new_expert = jnp.logical_or(i == 0, te_ref[i] != te_ref[jnp.maximum(i - 1, 0)])

    @pl.when(new_expert)
    def _():
        wg_b[...] = wg_ref[...].astype(bf16)
        wu_b[...] = wu_ref[...].astype(bf16)
        wd_b[...] = wd_ref[...].astype(bf16)

    @pl.when(valid > 0)
    def _():
        chunk = MOE_CHUNK
        ok = lax.broadcasted_iota(jnp.int32, (EXP_TILE, 1), 0) < valid
        x = jnp.where(ok, xs_ref[:, :D_MODEL], jnp.zeros((), bf16))
        gate_terms = (xs_ref[:, D_MODEL:D_MODEL + 1].astype(f32)
                      + xs_ref[:, D_MODEL + LANES:D_MODEL + LANES + 1].astype(f32))
        gate = jnp.where(ok, gate_terms, 0.0)
        hid = []
        for c in range(D_EXPERT // chunk):
            cols = slice(c * chunk, (c + 1) * chunk)
            hg = jnp.dot(x, wg_b[:, cols], preferred_element_type=f32)
            hu = jnp.dot(x, wu_b[:, cols], preferred_element_type=f32)
            hid.append((hg * _sigmoid(hg) * hu * gate).astype(bf16))
        hid = jnp.concatenate(hid, axis=1)
        for c in range(D_MODEL // chunk):
            cols = slice(c * chunk, (c + 1) * chunk)
            ys_ref[:, cols] = jnp.dot(hid, wd_b[:, cols], preferred_element_type=f32).astype(bf16)


def _expert_tiles(cnt, cap, n_steps):
    cnt = cnt.reshape(N_EXPERTS)
    tiles = (cnt + (EXP_TILE - 1)) // EXP_TILE
    ends = jnp.cumsum(tiles)
    n_valid = ends[-1]
    i = jnp.clip(jnp.arange(n_steps, dtype=jnp.int32), 0, jnp.maximum(n_valid - 1, 0))
    e = jnp.sum((i[:, None] >= ends[None, :]).astype(jnp.int32), axis=1)
    local = i - (ends - tiles)[e]
    rows = jnp.clip(cnt[e] - local * EXP_TILE, 0, EXP_TILE)
    rows = jnp.where(jnp.arange(n_steps) < n_valid, rows, 0)
    return e.astype(jnp.int32), (e * (cap // EXP_TILE) + local).astype(jnp.int32), rows.astype(jnp.int32)


def _moe(l, xs, cnt, n_tiles, w_gate, w_up, w_down):
    cap = _expert_capacity(n_tiles)
    n_steps = n_tiles * XS_ROWS // EXP_TILE + N_EXPERTS
    te, tb, tv = _expert_tiles(cnt, cap, n_steps)
    grid_spec = pltpu.PrefetchScalarGridSpec(
        num_scalar_prefetch=3,
        grid=(n_steps,),
        in_specs=[pl.BlockSpec((EXP_TILE, XS_W), lambda i, te, tb, tv: (tb[i], 0)),
                  pl.BlockSpec((None, None, D_MODEL, D_EXPERT), lambda i, te, tb, tv: (l, te[i], 0, 0)),
                  pl.BlockSpec((None, None, D_MODEL, D_EXPERT), lambda i, te, tb, tv: (l, te[i], 0, 0)),
                  pl.BlockSpec((None, None, D_EXPERT, D_MODEL), lambda i, te, tb, tv: (l, te[i], 0, 0))],
        out_specs=pl.BlockSpec((EXP_TILE, D_MODEL), lambda i, te, tb, tv: (tb[i], 0)),
        scratch_shapes=[pltpu.VMEM((D_MODEL, D_EXPERT), bf16), pltpu.VMEM((D_MODEL, D_EXPERT), bf16),
                        pltpu.VMEM((D_EXPERT, D_MODEL), bf16)],
    )
    return pl.pallas_call(
        _moe_kernel,
        grid_spec=grid_spec,
        out_shape=jax.ShapeDtypeStruct((N_EXPERTS * cap, D_MODEL), bf16),
        compiler_params=_params("arbitrary"),
        name="moe",
    )(te, tb, tv, xs, w_gate, w_up, w_down)


def _final_kernel(tab_ref, x_ref, pos_ref, ys_hbm, mod_ref, g_ref, b_ref, o_ref, y_vmem, sem, *, j_off):
    t, t_next, has_next, _ = _tile_id(j_off)
    step = pl.program_id(0) * pl.num_programs(1) + pl.program_id(1)
    f = _combine_experts(tab_ref, pos_ref, ys_hbm, y_vmem, sem, step, t, t_next, has_next)
    o_ref[...] = _layer_norm(ALPHA * x_ref[...] + mod_ref[5:6, :] * f, g_ref[...], b_ref[...])


def _final_ln(x, moe, mod, ln_g, ln_b):
    B, L, _ = x.shape
    seq = L - CTX_LEN
    skip = CTX_LEN // ROW_TILE
    tab, pos, ys = moe
    lat = pl.BlockSpec((None, ROW_TILE, D_MODEL), lambda b, j: (b, j + skip, 0))
    return pl.pallas_call(
        functools.partial(_final_kernel, j_off=skip),
        grid=(B, seq // ROW_TILE),
        in_specs=[pl.BlockSpec(memory_space=pltpu.SMEM), lat, _pos_spec(skip), pl.BlockSpec(memory_space=pl.ANY),
                  pl.BlockSpec((None, None, 6, D_MODEL), lambda b, j: (b, 1, 0, 0)),
                  _vec_spec(DEPTH - 1), _vec_spec(DEPTH - 1)],
        out_specs=_row_spec(D_MODEL),
        out_shape=jax.ShapeDtypeStruct((B, seq, D_MODEL), f32),
        scratch_shapes=_combine_scratch(),
        compiler_params=_params("arbitrary", "arbitrary"),
        name="final_ln",
    )(tab, x, pos, ys, mod, ln_g, ln_b)


def kernel(x, c, ctx, c_ctx, w_ada, b_ada, w_in, lb_logits, a_norm_g, rpb, w_out, ln1_g, ln1_b,
           w_router, b_router, w_gate, w_up, w_down, ln2_g, ln2_b):
    B, seq, D = x.shape
    L = CTX_LEN + seq
    xs = (ctx, x)

    cc = jnp.concatenate([c, c_ctx[None, :], jnp.zeros((16 - B - 1, D), f32)], axis=0)
    mods = _ada_mods(cc, w_ada, b_ada)
    lat_mod = mods[:, :B].reshape(DEPTH, B, 1, 6, D)
    ctx_mod = jnp.broadcast_to(mods[:, B].reshape(DEPTH, 1, 1, 6, D), (DEPTH, B, 1, 6, D))
    mod_all = jnp.concatenate([ctx_mod, lat_mod], axis=2)

    sm = jax.nn.softmax(lb_logits.astype(f32), axis=1)
    lower = jnp.cumsum(sm, axis=1) - sm[:, :1]

    tri_np, own_np, sgn_np = _scan_constants()
    scan_consts = (jnp.asarray(tri_np, bf16), jnp.asarray(own_np), jnp.asarray(sgn_np))
    cos_np, sin_np = _rope_tables(seq)
    cos_t, sin_t = jnp.asarray(cos_np), jnp.asarray(sin_np)
    mask = jnp.asarray(_window_mask())
    bias_t = _bias_tables(rpb.astype(f32) * LOG2E)

    w_in_b = w_in.astype(bf16)
    w_out_b = w_out.astype(bf16)
    wr_hi = w_router.T.astype(bf16)
    wr_res = w_router.T - wr_hi.astype(f32)
    wr_mid = wr_res.astype(bf16)
    wr_lo = (wr_res - wr_mid.astype(f32)).astype(bf16)
    w_router_t = jnp.concatenate([wr_hi, wr_mid, wr_lo], axis=0)
    b_router_c = b_router.reshape(N_EXPERTS, 1)
    ln1_g3, ln1_b3 = ln1_g.reshape(DEPTH, 1, D), ln1_b.reshape(DEPTH, 1, D)
    ln2_g3, ln2_b3 = ln2_g.reshape(DEPTH, 1, D), ln2_b.reshape(DEPTH, 1, D)

    tri_np, low_np = _dispatch_constants()
    tri, low = jnp.asarray(tri_np, bf16), jnp.asarray(low_np, bf16)
    n_tiles = B * (L // ROW_TILE)

    moe = None
    for l in range(DEPTH):
        pmod = mod_all[l - 1] if l > 0 else None
        xs, u = _in_proj(l, xs, moe, pmod, ln2_g3, ln2_b3, mod_all[l], w_in_b)
        a = _hgrn(u, lower[0, l], lower[1, l], a_norm_g[l], scan_consts)
        bmix = _na(l, u, cos_t, sin_t, bias_t, mask)
        xs, pos, tab, cnt, x_sorted = _out_proj(l, a, bmix, xs, mod_all[l], w_out_b, ln1_g3, ln1_b3,
                                                w_router_t, b_router_c, tri, low, skip_ctx=l == DEPTH - 1)
        y_sorted = _moe(l, x_sorted, cnt, n_tiles, w_gate, w_up, w_down)
        moe = (tab, pos, y_sorted)
    return _final_ln(xs, moe, mod_all[DEPTH - 1], ln2_g3, ln2_b3)
```

```python
import functools

import numpy as np
import jax
import jax.numpy as jnp
from jax import lax
from jax.experimental import pallas as pl
from jax.experimental.pallas import tpu as pltpu

D_MODEL = 1024
DEPTH = 4
GRID_W = 64
CTX_LEN = 256
A_HEADS = 4
A_DK = 128
A_WIDTH = A_HEADS * A_DK
B_HEADS = 8
B_HD = 64
B_WIDTH = B_HEADS * B_HD
D_IN = 5 * A_WIDTH + 3 * B_WIDTH
NA_ROWS = 8
NA_COLS = 16
ROPE_BASE = 10000.0
N_EXPERTS = 16
N_GROUPS = 4
EXPERTS_PER_GROUP = N_EXPERTS // N_GROUPS
D_EXPERT = 512
ALPHA = (2 * DEPTH) ** 0.25
LN_EPS = 1e-5
RMS_EPS = 1e-6
F_FLOOR = 1e-6
MASK_VALUE = -1e30
LOG2E = 1.4426950408889634

LANES = 128
ROW_TILE = 256
XS_BLK = 16
XS_ROWS = 2 * ROW_TILE + N_EXPERTS * XS_BLK
XS_W = D_MODEL + 2 * LANES
DRAIN_GROUP = 8
EXP_TILE = 1024
MOE_CHUNK = 256
SCAN_CHUNK = 128
SCAN_LEVELS = (64, 32, 16, 8, 4)
SCAN_BASE = 4
NA_GROUP = 4
VMEM_LIMIT = 56 * 1024 * 1024

f32 = jnp.float32
bf16 = jnp.bfloat16

_NT = (((1,), (1,)), ((), ()))
_TN = (((0,), (0,)), ((), ()))


def _params(*sem):
    return pltpu.CompilerParams(dimension_semantics=sem, vmem_limit_bytes=VMEM_LIMIT)


def _sigmoid(z, sign=1.0):
    return 1.0 / (1.0 + jnp.exp2(z * (-sign * LOG2E)))


def _layer_norm(h, g, b):
    mu = jnp.mean(h, axis=-1, keepdims=True)
    d = h - mu
    var = jnp.mean(d * d, axis=-1, keepdims=True)
    return d * lax.rsqrt(var + LN_EPS) * g + b


def _ada_kernel(c_ref, w_ref, b_ref, o_ref):
    c = c_ref[...]
    o_ref[...] = jnp.dot(c * _sigmoid(c), w_ref[...], preferred_element_type=f32,
                         precision=lax.Precision.HIGHEST) + b_ref[...]


def _ada_mods(cc, w_ada, b_ada):
    n_col = 4
    tn = 6 * D_MODEL // n_col
    return pl.pallas_call(
        _ada_kernel,
        grid=(DEPTH, n_col),
        in_specs=[
            pl.BlockSpec((16, D_MODEL), lambda l, j: (0, 0)),
            pl.BlockSpec((None, D_MODEL, tn), lambda l, j: (l, 0, j)),
            pl.BlockSpec((None, 1, tn), lambda l, j: (l, 0, j)),
        ],
        out_specs=pl.BlockSpec((None, 16, tn), lambda l, j: (l, 0, j)),
        out_shape=jax.ShapeDtypeStruct((DEPTH, 16, 6 * D_MODEL), f32),
        compiler_params=_params("parallel", "parallel"),
        name="ada_mods",
    )(cc, w_ada, b_ada.reshape(DEPTH, 1, 6 * D_MODEL))


def _drain(copy_of_rows, n_blocks):
    def grouped(i, carry):
        copy_of_rows(DRAIN_GROUP * XS_BLK).wait()
        return carry

    def single(i, carry):
        copy_of_rows(XS_BLK).wait()
        return carry

    lax.fori_loop(0, n_blocks // DRAIN_GROUP, grouped, 0)
    lax.fori_loop(0, lax.rem(n_blocks, DRAIN_GROUP), single, 0)


def _tile_id(j_off):
    b, j = pl.program_id(0), pl.program_id(1)
    nj = pl.num_programs(1)
    tiles_per_batch = nj + j_off
    t = b * tiles_per_batch + j + j_off
    last_j = j == nj - 1
    t_next = jnp.where(last_j, (b + 1) * tiles_per_batch + j_off, t + 1)
    has_next = jnp.logical_not(jnp.logical_and(last_j, b == pl.num_programs(0) - 1))
    return t, t_next, has_next, jnp.logical_and(b == 0, j == 0)


def _combine_experts(tab_ref, pos_ref, ys_hbm, y_vmem, sem, step, t, t_next, has_next):
    is_first = step == 0
    slot = lax.rem(step, 2)

    def block_copy(src_row, dst_row, sl, rows=XS_BLK):
        return pltpu.make_async_copy(ys_hbm.at[pl.ds(pl.multiple_of(src_row, XS_BLK), rows), :],
                                     y_vmem.at[sl, pl.ds(pl.multiple_of(dst_row, XS_BLK), rows), :],
                                     sem.at[sl])

    def fetch(tile, sl):
        @pl.when(is_first)
        def _():
            y_vmem[sl] = jnp.zeros((XS_ROWS, D_MODEL), bf16)

        for e in range(N_EXPERTS):
            base, start, nblk = tab_ref[tile, 3 * e], tab_ref[tile, 3 * e + 1], tab_ref[tile, 3 * e + 2]

            def body(i, carry):
                block_copy(base + i * XS_BLK, start + i * XS_BLK, sl).start()
                return carry

            lax.fori_loop(0, nblk, body, 0)

    def wait_all(tile, sl):
        total = tab_ref[tile, 2]
        for e in range(1, N_EXPERTS):
            total = total + tab_ref[tile, 3 * e + 2]
        _drain(lambda rows: block_copy(0, 0, sl, rows), total)

    @pl.when(is_first)
    def _():
        fetch(t, slot)

    @pl.when(has_next)
    def _():
        fetch(t_next, 1 - slot)

    wait_all(t, slot)
    y = y_vmem[slot]
    r = lax.broadcasted_iota(jnp.int32, (XS_ROWS, ROW_TILE), 0)
    perm = jnp.where(r == pos_ref[0:1, :], 1.0, jnp.where(r == pos_ref[1:2, :], 1.0, 0.0)).astype(bf16)
    return lax.dot_general(perm, y, _TN, preferred_element_type=f32)


def _in_kernel(*refs, with_ln):
    if with_ln:
        (tab_ref, x_ref, pos_ref, ys_hbm, pmod_ref, g_ref, b_ref, mod_ref, w_ref,
         xo_ref, u_ref, y_vmem, sem) = refs
        t, t_next, has_next, _ = _tile_id(0)
        step = pl.program_id(0) * pl.num_programs(1) + pl.program_id(1)
        f = _combine_experts(tab_ref, pos_ref, ys_hbm, y_vmem, sem, step, t, t_next, has_next)
        h = ALPHA * x_ref[...] + pmod_ref[5:6, :] * f
        x = _layer_norm(h, g_ref[...], b_ref[...])
        xo_ref[...] = x
    else:
        ctx_ref, lat_ref, mod_ref, w_ref, xo_ref, u_ref = refs
        x = jnp.where(pl.program_id(1) == 0, ctx_ref[...], lat_ref[...])
        xo_ref[...] = x
    xm = x * (1.0 + mod_ref[1:2, :]) + mod_ref[0:1, :]
    u_ref[...] = jnp.dot(xm.astype(bf16), w_ref[...], preferred_element_type=f32)


def _mod_spec(j_off=0):
    return pl.BlockSpec((None, None, 6, D_MODEL), lambda b, j: (b, jnp.minimum(j + j_off, 1), 0, 0))


def _row_spec(width, j_off=0):
    return pl.BlockSpec((None, ROW_TILE, width), lambda b, j: (b, j + j_off, 0))


def _vec_spec(l):
    return pl.BlockSpec((None, 1, D_MODEL), lambda b, j: (l, 0, 0))


def _pos_spec(j_off=0):
    return pl.BlockSpec((None, 2, ROW_TILE), lambda b, j: (b, 0, j + j_off))


def _combine_scratch():
    return [pltpu.VMEM((2, XS_ROWS, D_MODEL), bf16), pltpu.SemaphoreType.DMA((2,))]


def _in_proj(l, x, moe, pmod, ln_g, ln_b, mod, w_in):
    if moe is None:
        B, L = x[0].shape[0], x[0].shape[1] + x[1].shape[1]
    else:
        B, L, _ = x.shape
    w_spec = pl.BlockSpec((None, D_MODEL, D_IN), lambda b, j: (l, 0, 0))
    u_shape = jax.ShapeDtypeStruct((B, L, D_IN), f32)
    if moe is None:
        ctx, lat = x
        skip = CTX_LEN // ROW_TILE
        return pl.pallas_call(
            functools.partial(_in_kernel, with_ln=False),
            grid=(B, L // ROW_TILE),
            in_specs=[pl.BlockSpec((None, ROW_TILE, D_MODEL), lambda b, j: (b, 0, 0)),
                      pl.BlockSpec((None, ROW_TILE, D_MODEL), lambda b, j: (b, jnp.maximum(j - skip, 0), 0)),
                      _mod_spec(), w_spec],
            out_specs=[_row_spec(D_MODEL), _row_spec(D_IN)],
            out_shape=[jax.ShapeDtypeStruct((B, L, D_MODEL), f32), u_shape],
            compiler_params=_params("parallel", "parallel"),
            name="in_proj0",
        )(ctx, lat, mod, w_in)
    tab, pos, ys = moe
    return pl.pallas_call(
        functools.partial(_in_kernel, with_ln=True),
        grid=(B, L // ROW_TILE),
        in_specs=[pl.BlockSpec(memory_space=pltpu.SMEM), _row_spec(D_MODEL), _pos_spec(),
                  pl.BlockSpec(memory_space=pl.ANY), _mod_spec(), _vec_spec(l - 1), _vec_spec(l - 1),
                  _mod_spec(), w_spec],
        out_specs=[_row_spec(D_MODEL), _row_spec(D_IN)],
        out_shape=[jax.ShapeDtypeStruct(x.shape, f32), u_shape],
        scratch_shapes=_combine_scratch(),
        compiler_params=_params("arbitrary", "arbitrary"),
        name="in_proj",
    )(tab, x, pos, ys, pmod, ln_g, ln_b, mod, w_in)


def _scan_constants():
    C = SCAN_CHUNK
    p = np.arange(C)
    tri = (p[None, :] <= p[:, None]).astype(np.float32)
    owner = np.full((C, C), -1, np.int32)
    signs = []
    for li, m in enumerate(SCAN_LEVELS):
        same = (p[:, None] // (2 * m)) == (p[None, :] // (2 * m))
        owner[same & ((p[:, None] // m) % 2 == 1) & ((p[None, :] // m) % 2 == 0)] = li
        signs.append(np.where((p // m) % 2 == 1, 1.0, -1.0))
    owner[((p[:, None] // SCAN_BASE) == (p[None, :] // SCAN_BASE)) & (p[None, :] <= p[:, None])] = len(SCAN_LEVELS)
    sgn = np.broadcast_to(np.stack(signs)[:, :, None], (len(SCAN_LEVELS), C, LANES)).astype(np.float32)
    return (np.stack([tri, tri[::-1, ::-1]]), np.stack([owner, owner[::-1, ::-1]]),
            np.stack([sgn, sgn[:, ::-1]]))


def _hgrn_kernel(q_ref, ff_ref, fb_ref, v_ref, g_ref, lbf_ref, lbb_ref, ng_ref, t_ref, own_ref, sgn_ref,
                 a_ref, of_scr, ob_scr, *, n_ctx, n_all):
    C = SCAN_CHUNK
    n_lv = len(SCAN_LEVELS)


    def gates(rows, z_ref, lb, d):
        z = z_ref[rows, :]
        lf = jnp.log2(jnp.maximum(lb + (1.0 - lb) * _sigmoid(z), F_FLOOR))
        k = (1.0 - lb) * _sigmoid(z, -1.0)
        qz = q_ref[rows, :]
        q = qz * _sigmoid(qz)
        v = v_ref[rows, :].astype(bf16)
        hi = lf.astype(bf16)
        r1 = lf - hi.astype(f32)
        mid = r1.astype(bf16)
        lo = (r1 - mid.astype(f32)).astype(bf16)
        d3 = jnp.dot(t_ref[d], jnp.concatenate([hi, mid, lo], axis=1), preferred_element_type=f32)
        cum = d3[:, :LANES] + d3[:, LANES:2 * LANES] + d3[:, 2 * LANES:]
        return q, k, v, cum

    def intra(q, k, cum, d):
        last = cum[C - 1:C, :] if d == 0 else cum[0:1, :]

        def rel(block, ref_rows):
            blocks = cum.reshape(C // block, block, LANES)
            if len(ref_rows) == 1:
                ref = blocks[:, ref_rows[0]:ref_rows[0] + 1, :]
            else:
                first = lax.broadcasted_iota(jnp.int32, (1, block, 1), 1) < block // 2
                ref = jnp.where(first, blocks[:, ref_rows[0]:ref_rows[0] + 1, :],
                                blocks[:, ref_rows[1]:ref_rows[1] + 1, :])
            return (blocks - ref).reshape(C, LANES)

        def to_boundary(m):
            blocks = cum.reshape(C // (2 * m), 2 * m, LANES)
            lo, hi = blocks[:, :m, :], blocks[:, m:, :]
            if d == 0:
                ref = lo[:, m - 1:m, :]
                parts = (ref - lo, hi - ref)
            else:
                ref = hi[:, 0:1, :]
                parts = (lo - ref, ref - hi)
            return jnp.concatenate(parts, axis=1).reshape(C, LANES)

        q_in = (q * jnp.exp2(cum)).astype(bf16)
        own = own_ref[d]
        scores = jnp.zeros((C, C), f32)
        for li in range(n_lv + 1):
            m = SCAN_LEVELS[li] if li < n_lv else 0
            if m and m % 8 == 0:
                def halves(x):
                    blocks = x.reshape(C // (2 * m), 2 * m, x.shape[-1])
                    return blocks[:, :m, :], blocks[:, m:, :]

                def join(lo, hi):
                    return jnp.concatenate([lo, hi], axis=1).reshape(C, lo.shape[-1])

                (q_lo, q_hi), (k_lo, k_hi) = halves(q), halves(k)
                mixed = join(k_lo, q_hi) if d == 0 else join(q_lo, k_hi)
                z = (mixed * jnp.exp2(to_boundary(m))).astype(bf16)
                a = lax.dot_general(z, z, _NT, preferred_element_type=f32)
                (a_lo, a_hi), (s_lo, s_hi), (o_lo, o_hi) = halves(a), halves(scores), halves(own)
                if d == 0:
                    scores = join(s_lo, jnp.where(o_hi == li, a_hi, s_hi))
                else:
                    scores = join(jnp.where(o_lo == li, a_lo, s_lo), s_hi)
                continue
            if m:
                wq = jnp.exp2(rel(2 * m, (m - 1,) if d == 0 else (m,)) * sgn_ref[d, li])
                wk = wq
            else:
                nb = SCAN_BASE
                dl = rel(2 * nb, (0, nb) if d == 0 else (nb - 1, 2 * nb - 1))
                wq = jnp.exp2(dl)
                wk = jnp.exp2(-dl)
            a = lax.dot_general((q * wq).astype(bf16), (k * wk).astype(bf16), _NT,
                                preferred_element_type=f32)
            scores = jnp.where(own == li, a, scores)
        kdec = (k * jnp.exp2(last - cum)).astype(bf16)
        return q_in, scores.astype(bf16), kdec, jnp.exp2(last)

    def finish(pending, S, o_scr):
        q_in, scores, kdec, decay, v, row0 = pending
        o = (lax.dot_general(q_in, S.astype(bf16), _NT, preferred_element_type=f32)
             + jnp.dot(scores, v, preferred_element_type=f32))
        o_scr[pl.ds(pl.multiple_of(row0, C), C), :] = o
        return S * decay + lax.dot_general(v, kdec, _TN, preferred_element_type=f32)

    lbf = lbf_ref[...]
    lbb = lbb_ref[...]

    def bwd_chunk(i):
        return jnp.where(i < n_ctx, n_ctx - 1 - i, n_all - 1 - (i - n_ctx))

    def scan_body(i, carry):
        s_f, s_b, pend_f, pend_b = carry
        row_f = i * C
        row_b = bwd_chunk(i) * C
        s_f = finish(pend_f, s_f, of_scr)
        s_b = finish(pend_b, s_b, ob_scr)
        qf, kf, vf, cum_f = gates(pl.ds(pl.multiple_of(row_f, C), C), ff_ref, lbf, 0)
        qb, kb, vb, cum_b = gates(pl.ds(pl.multiple_of(row_b, C), C), fb_ref, lbb, 1)
        pend_f = intra(qf, kf, cum_f, 0) + (vf, row_f)
        pend_b = intra(qb, kb, cum_b, 1) + (vb, row_b)
        return s_f, s_b, pend_f, pend_b

    def idle(row0):
        zc = jnp.zeros((C, LANES), bf16)
        return (zc, jnp.zeros((C, C), bf16), zc, jnp.ones((1, LANES), f32), zc, row0)

    zero = jnp.zeros((A_DK, A_DK), f32)
    init = (zero, zero, idle(jnp.int32(0)), idle(bwd_chunk(jnp.int32(0)) * C))
    s_f, s_b, pend_f, pend_b = lax.fori_loop(0, n_all, scan_body, init, unroll=3)
    finish(pend_f, s_f, of_scr)
    finish(pend_b, s_b, ob_scr)

    ng = ng_ref[...]

    def readout_body(i, carry):
        rows = pl.ds(pl.multiple_of(i * ROW_TILE, ROW_TILE), ROW_TILE)
        o = of_scr[rows, :] + ob_scr[rows, :]
        o = o * lax.rsqrt(jnp.mean(o * o, axis=-1, keepdims=True) + RMS_EPS) * ng
        gz = g_ref[rows, :]
        a_ref[rows, :] = (o * (gz * _sigmoid(gz))).astype(a_ref.dtype)
        return carry

    lax.fori_loop(0, n_all * C // ROW_TILE, readout_body, 0)


def _hgrn(u, lb_f, lb_b, norm_g, scan_consts):
    B, L, _ = u.shape
    hb = A_WIDTH // LANES

    def col(sec):
        return pl.BlockSpec((None, L, LANES), lambda b, h: (b, 0, sec * hb + h))

    def whole(arr):
        return pl.BlockSpec(arr.shape, lambda b, h: (0,) * arr.ndim)

    lb_spec = pl.BlockSpec((None, 1, LANES), lambda b, h: (h, 0, 0))
    return pl.pallas_call(
        functools.partial(_hgrn_kernel, n_ctx=CTX_LEN // SCAN_CHUNK, n_all=L // SCAN_CHUNK),
        grid=(B, A_HEADS),
        in_specs=[col(0), col(1), col(2), col(3), col(4), lb_spec, lb_spec,
                  pl.BlockSpec((1, LANES), lambda b, h: (0, 0))] + [whole(a) for a in scan_consts],
        out_specs=pl.BlockSpec((None, L, LANES), lambda b, h: (b, 0, h)),
        out_shape=jax.ShapeDtypeStruct((B, L, A_WIDTH), bf16),
        scratch_shapes=[pltpu.VMEM((L, LANES), f32), pltpu.VMEM((L, LANES), f32)],
        compiler_params=_params("parallel", "parallel"),
        name="hgrn2",
    )(u, u, u, u, u, lb_f.reshape(A_HEADS, 1, LANES), lb_b.reshape(A_HEADS, 1, LANES),
      norm_g.reshape(1, LANES), *scan_consts)


def _rope_tables(seq):
    t = np.arange(seq)
    pos_r = (t // GRID_W).astype(np.float32)
    pos_c = (t % GRID_W).astype(np.float32)
    quarter = B_HD // 4
    inv = (ROPE_BASE ** (-np.arange(quarter, dtype=np.float32) / quarter)).astype(np.float32)
    lane = np.arange(LANES)
    in_head = lane % B_HD
    pos = np.where((in_head < B_HD // 2)[None, :], pos_r[:, None], pos_c[:, None])
    ang = (pos * inv[lane % quarter][None, :]).astype(np.float32)
    sign = np.where((lane % (2 * quarter)) < quarter, -1.0, 1.0)
    return np.cos(ang).astype(np.float32), (np.sin(ang) * sign[None, :]).astype(np.float32)


def _window_mask():
    c = np.arange(GRID_W)
    cs = np.clip(c - NA_COLS // 2, 0, GRID_W - NA_COLS)
    in_win = (c[None, :] >= cs[:, None]) & (c[None, :] < cs[:, None] + NA_COLS)
    return np.tile(in_win, (1, NA_ROWS)).astype(np.float32)


def _bias_tables(rpb):
    c = np.arange(GRID_W)
    dc = np.clip(c[None, :] - c[:, None], -(NA_COLS - 1), NA_COLS - 1) + (NA_COLS - 1)
    pick = (np.arange(2 * NA_COLS - 1)[:, None, None] == dc[None]).astype(np.float32)
    cols = jnp.einsum('lhdk,kcx->lhdcx', rpb, jnp.asarray(pick), precision=lax.Precision.HIGHEST)
    n_rel = 2 * NA_ROWS - 1
    n_win = NA_ROWS * GRID_W
    t = pl.pallas_call(
        _bias_expand_kernel,
        grid=(DEPTH * B_HEADS,),
        in_specs=[pl.BlockSpec((None, n_rel, GRID_W, GRID_W), lambda i: (i, 0, 0, 0))],
        out_specs=pl.BlockSpec((None, NA_ROWS, GRID_W, n_win), lambda i: (i, 0, 0, 0)),
        out_shape=jax.ShapeDtypeStruct((DEPTH * B_HEADS, NA_ROWS, GRID_W, n_win), f32),
        compiler_params=_params("parallel"),
        name="bias_expand",
    )(cols.reshape(DEPTH * B_HEADS, n_rel, GRID_W, GRID_W))
    return t.reshape(DEPTH, B_HEADS, NA_ROWS, GRID_W, n_win)


def _bias_expand_kernel(c_ref, o_ref):
    per_vreg = LANES // GRID_W
    for o in range(NA_ROWS):
        for g in range(NA_ROWS // per_vreg):
            parts = [c_ref[o + g * per_vreg + p] for p in range(per_vreg)]
            o_ref[o, :, g * LANES:(g + 1) * LANES] = jnp.concatenate(parts, axis=1)


def _na_kernel(q_ref, k_ref, v_ref, cos_ref, sin_ref, bias_ref, mask_ref, o_ref,
               qr_scr, kr_scr, qp_scr, kc_scr, v_scr, *, seq, rows_grid, ctx_out):
    hd = B_HD
    scale = hd ** -0.5 * LOG2E
    n_win = NA_ROWS * GRID_W
    lane = lax.broadcasted_iota(jnp.int32, (ROW_TILE, LANES), 1)
    first_half = (lane % (hd // 2)) < (hd // 4)
    head0 = lane < hd

    def swap_halves(x):
        return jnp.where(first_half, pltpu.roll(x, LANES - hd // 4, axis=1), pltpu.roll(x, hd // 4, axis=1))

    def store_per_head(dst, rows, x):
        dst[0, rows, :] = jnp.where(head0, x, 0.0).astype(bf16)
        dst[1, rows, :] = jnp.where(head0, 0.0, x).astype(bf16)

    ctx_rows = pl.ds(0, CTX_LEN)
    if ctx_out:
        store_per_head(qp_scr, ctx_rows, q_ref[ctx_rows, :] * scale)
    kc_scr[...] = k_ref[ctx_rows, :].astype(bf16)
    v_scr[ctx_rows, :] = v_ref[ctx_rows, :].astype(bf16)

    def prep(i, carry):
        lrows = pl.ds(pl.multiple_of(i * ROW_TILE, ROW_TILE), ROW_TILE)
        rows = pl.ds(pl.multiple_of(CTX_LEN + i * ROW_TILE, ROW_TILE), ROW_TILE)
        q = q_ref[rows, :] * scale
        k = k_ref[rows, :]
        cos = cos_ref[lrows, :]
        sin = sin_ref[lrows, :]
        store_per_head(qp_scr, rows, q)
        v_scr[rows, :] = v_ref[rows, :].astype(bf16)
        store_per_head(qr_scr, lrows, q * cos + swap_halves(q) * sin)
        kr_scr[lrows, :] = (k * cos + swap_halves(k) * sin).astype(bf16)
        return carry

    lax.fori_loop(0, seq // ROW_TILE, prep, 0)

    def pick_heads(o0, o1):
        keep0 = lax.broadcasted_iota(jnp.int32, o0.shape, 1) < hd
        return jnp.where(keep0, o0, o1)

    if ctx_out:
        outs = []
        for h in range(2):
            s = lax.dot_general(qp_scr[h, ctx_rows, :], kc_scr[...], _NT, preferred_element_type=f32)
            e = jnp.exp2(s - jnp.max(s, axis=-1, keepdims=True))
            o = jnp.dot(e.astype(bf16), v_scr[ctx_rows, :], preferred_element_type=f32)
            outs.append(o / jnp.sum(e, axis=-1, keepdims=True))
        o_ref[ctx_rows, :] = pick_heads(*outs).astype(o_ref.dtype)

    def rows_body(it, carry):
        chains = []
        for g in range(NA_GROUP):
            r = it * NA_GROUP + g
            rs = jnp.clip(r - NA_ROWS // 2, 0, rows_grid - NA_ROWS)
            for h in range(2):
                chains.append(dict(
                    h=h, off=rs - r + (NA_ROWS - 1),
                    qrows=pl.ds(pl.multiple_of(r * GRID_W, GRID_W), GRID_W),
                    krows=pl.ds(pl.multiple_of(rs * GRID_W, GRID_W), n_win),
                    arows=pl.ds(pl.multiple_of(CTX_LEN + r * GRID_W, GRID_W), GRID_W),
                    vrows=pl.ds(pl.multiple_of(CTX_LEN + rs * GRID_W, GRID_W), n_win)))
        for c in chains:
            h = c["h"]
            c["s_lat"] = lax.dot_general(qr_scr[h, c["qrows"], :], kr_scr[c["krows"], :], _NT,
                                         preferred_element_type=f32)
            c["s_ctx"] = lax.dot_general(qp_scr[h, c["arows"], :], kc_scr[...], _NT, preferred_element_type=f32)
        in_win = mask_ref[...] > 0.5
        for c in chains:
            s_lat = jnp.where(in_win, c["s_lat"] + bias_ref[c["h"], c["off"]], MASK_VALUE)
            m = jnp.maximum(jnp.max(s_lat, axis=-1, keepdims=True), jnp.max(c["s_ctx"], axis=-1, keepdims=True))
            e_lat = jnp.exp2(s_lat - m)
            e_ctx = jnp.exp2(c["s_ctx"] - m)
            c["den"] = jnp.sum(e_lat, axis=-1, keepdims=True) + jnp.sum(e_ctx, axis=-1, keepdims=True)
            c["e_lat"] = e_lat.astype(bf16)
            c["e_ctx"] = e_ctx.astype(bf16)
        for c in chains:
            c["o"] = (jnp.dot(c["e_lat"], v_scr[c["vrows"], :], preferred_element_type=f32)
                      + jnp.dot(c["e_ctx"], v_scr[ctx_rows, :], preferred_element_type=f32))
        for g in range(NA_GROUP):
            c0, c1 = chains[2 * g:2 * g + 2]
            o_ref[c0["arows"], :] = pick_heads(c0["o"] / c0["den"], c1["o"] / c1["den"]).astype(o_ref.dtype)
        return carry

    lax.fori_loop(0, rows_grid // NA_GROUP, rows_body, 0)


def _na(l, u, cos_t, sin_t, bias_t, mask):
    B, L, _ = u.shape
    seq = L - CTX_LEN
    first = 5 * A_WIDTH // LANES
    pairs = B_WIDTH // LANES

    def col(sec):
        return pl.BlockSpec((None, L, LANES), lambda b, h: (b, 0, first + sec * pairs + h))

    n_win = NA_ROWS * GRID_W
    return pl.pallas_call(
        functools.partial(_na_kernel, seq=seq, rows_grid=seq // GRID_W, ctx_out=l < DEPTH - 1),
        grid=(B, pairs),
        in_specs=[col(0), col(1), col(2),
                  pl.BlockSpec((seq, LANES), lambda b, h: (0, 0)),
                  pl.BlockSpec((seq, LANES), lambda b, h: (0, 0)),
                  pl.BlockSpec((None, 2, NA_ROWS, GRID_W, n_win), lambda b, h: (l, h, 0, 0, 0)),
                  pl.BlockSpec((GRID_W, n_win), lambda b, h: (0, 0))],
        out_specs=pl.BlockSpec((None, L, LANES), lambda b, h: (b, 0, h)),
        out_shape=jax.ShapeDtypeStruct((B, L, B_WIDTH), bf16),
        scratch_shapes=[pltpu.VMEM((2, seq, LANES), bf16), pltpu.VMEM((seq, LANES), bf16),
                        pltpu.VMEM((2, L, LANES), bf16), pltpu.VMEM((CTX_LEN, LANES), bf16),
                        pltpu.VMEM((L, LANES), bf16)],
        compiler_params=_params("parallel", "parallel"),
        name="na_attn",
    )(u, u, u, cos_t, sin_t, bias_t, mask)


def _route(logits, b_router, sel_ref, gate_ref):
    aff = _sigmoid(logits)
    sel = aff + b_router
    s = [sel[e:e + 1, :] for e in range(N_EXPERTS)]
    one = lambda cond: jnp.where(cond, 1.0, 0.0)
    scores = []
    for g in range(N_GROUPS):
        s0, s1, s2, s3 = s[g * EXPERTS_PER_GROUP:(g + 1) * EXPERTS_PER_GROUP]
        hi01, lo01 = jnp.maximum(s0, s1), jnp.minimum(s0, s1)
        hi23, lo23 = jnp.maximum(s2, s3), jnp.minimum(s2, s3)
        scores.append(jnp.maximum(hi01, hi23) + jnp.maximum(jnp.minimum(hi01, hi23), jnp.maximum(lo01, lo23)))
    weights = []
    chosen = []
    for g in range(N_GROUPS):
        gsel = None
        for g2 in range(N_GROUPS):
            if g2 == g:
                continue
            t = one(scores[g2] < scores[g]) if g2 < g else one(scores[g2] <= scores[g])
            gsel = t if gsel is None else gsel * t
        for i in range(EXPERTS_PER_GROUP):
            e = g * EXPERTS_PER_GROUP + i
            rank = None
            for j in range(EXPERTS_PER_GROUP):
                if j == i:
                    continue
                e2 = g * EXPERTS_PER_GROUP + j
                t = one(s[e2] >= s[e]) if j < i else one(s[e2] > s[e])
                rank = t if rank is None else rank + t
            chosen.append(gsel * one(rank < 1.5))
            weights.append(chosen[-1] * aff[e:e + 1, :])
    total = weights[0]
    for w in weights[1:]:
        total = total + w
    inv = 1.0 / total
    for e in range(N_EXPERTS):
        sel_ref[e:e + 1, :] = chosen[e]
        gate_ref[e:e + 1, :] = weights[e] * inv


def _out_kernel(a_ref, b_ref, x_ref, mod_ref, wa_ref, wb_ref, g_ref, bt_ref, wr_ref, br_ref, tri_ref, low_ref,
                x1_ref, pos_ref, tab_ref, cnt_ref, xs_hbm,
                sel_scr, gate_scr, meta_v, meta_s, xs_vmem, sem, cnt_smem, nblk_smem, *, cap, j_off):
    t, _, has_next, is_first = _tile_id(j_off)
    slot = lax.rem(pl.program_id(0) * pl.num_programs(1) + pl.program_id(1), 2)

    @pl.when(is_first)
    def _():
        for e in range(N_EXPERTS):
            cnt_smem[e] = 0
        nblk_smem[0] = 0
        nblk_smem[1] = 0

    y = (jnp.dot(a_ref[...], wa_ref[...], preferred_element_type=f32)
         + jnp.dot(b_ref[...], wb_ref[...], preferred_element_type=f32))
    x1 = _layer_norm(ALPHA * x_ref[...] + mod_ref[2:3, :] * y, g_ref[...], bt_ref[...])
    x1_ref[...] = x1
    xm = x1 * (1.0 + mod_ref[4:5, :]) + mod_ref[3:4, :]
    xm_hi = xm.astype(bf16)
    xm_mid = (xm - xm_hi.astype(f32)).astype(bf16)
    E = N_EXPERTS
    r_hi = lax.dot_general(wr_ref[...], xm_hi, _NT, preferred_element_type=f32)
    r_mid = lax.dot_general(wr_ref[0:2 * E, :], xm_mid, _NT, preferred_element_type=f32)
    logits = ((r_hi[0:E] + r_mid[0:E]) + (r_hi[E:2 * E] + r_mid[E:2 * E])) + r_hi[2 * E:3 * E]
    _route(logits, br_ref[...], sel_scr, gate_scr)

    sel = sel_scr[...]
    gate = gate_scr[...]
    sel_b = sel.astype(bf16)
    rank = jnp.dot(sel_b, tri_ref[...], preferred_element_type=f32)
    below = jnp.dot(low_ref[...], sel_b, preferred_element_type=f32)
    n = jnp.sum(sel, axis=1, keepdims=True)
    npad = jnp.broadcast_to(jnp.floor((n + (XS_BLK - 1)) * (1.0 / XS_BLK)) * XS_BLK, (N_EXPERTS, LANES))
    start = jnp.dot(low_ref[...], npad.astype(bf16), preferred_element_type=f32)
    pos = start[:, :1] + rank
    first = jnp.where(below < 0.5, sel, 0.0)
    second = sel - first
    pos0 = jnp.sum(first * pos, axis=0, keepdims=True).astype(jnp.int32)
    pos1 = jnp.sum(second * pos, axis=0, keepdims=True).astype(jnp.int32)
    g0 = jnp.sum(first * gate, axis=0, keepdims=True)
    g1 = jnp.sum(second * gate, axis=0, keepdims=True)
    pos_ref[0:1, :] = pos0
    pos_ref[1:2, :] = pos1
    meta_v[0] = start.astype(jnp.int32)
    meta_v[1] = (npad * (1.0 / XS_BLK)).astype(jnp.int32)
    pltpu.sync_copy(meta_v, meta_s)

    r = lax.broadcasted_iota(jnp.int32, (XS_ROWS, ROW_TILE), 0)
    hit0 = r == pos0
    hit1 = r == pos1
    perm = jnp.where(hit0, 1.0, jnp.where(hit1, 1.0, 0.0)).astype(bf16)
    gate_sorted = jnp.sum(jnp.where(hit0, g0, jnp.where(hit1, g1, 0.0)), axis=1, keepdims=True)
    xs = jnp.dot(perm, xm_hi, preferred_element_type=f32)

    def block_copy(src_row, dst_row, sl, rows=XS_BLK):
        return pltpu.make_async_copy(xs_vmem.at[sl, pl.ds(pl.multiple_of(src_row, XS_BLK), rows), :],
                                     xs_hbm.at[pl.ds(pl.multiple_of(dst_row, XS_BLK), rows), :],
                                     sem.at[sl])

    def wait_blocks(count, sl):
        _drain(lambda rows: block_copy(0, 0, sl, rows), count)

    wait_blocks(nblk_smem[slot], slot)
    xs_vmem[slot, :, :D_MODEL] = xs.astype(bf16)
    gate_hi = gate_sorted.astype(bf16)
    gate_lo = (gate_sorted - gate_hi.astype(f32)).astype(bf16)
    xs_vmem[slot, :, D_MODEL:D_MODEL + LANES] = jnp.broadcast_to(gate_hi, (XS_ROWS, LANES))
    xs_vmem[slot, :, D_MODEL + LANES:] = jnp.broadcast_to(gate_lo, (XS_ROWS, LANES))
    total = 0
    for e in range(N_EXPERTS):
        start_e = meta_s[0, e, 0]
        nblk_e = meta_s[1, e, 0]
        base = e * cap + cnt_smem[e]
        tab_ref[t, 3 * e] = base
        tab_ref[t, 3 * e + 1] = start_e
        tab_ref[t, 3 * e + 2] = nblk_e

        def body(i, carry):
            block_copy(start_e + i * XS_BLK, base + i * XS_BLK, slot).start()
            return carry

        lax.fori_loop(0, nblk_e, body, 0)
        cnt_smem[e] = cnt_smem[e] + nblk_e * XS_BLK
        total = total + nblk_e
    nblk_smem[slot] = total

    @pl.when(jnp.logical_not(has_next))
    def _():
        wait_blocks(nblk_smem[slot], slot)
        wait_blocks(nblk_smem[1 - slot], 1 - slot)
        for e in range(N_EXPERTS):
            cnt_ref[0, e] = cnt_smem[e]


def _dispatch_constants():
    t = np.arange(ROW_TILE)
    e = np.arange(N_EXPERTS)
    return ((t[:, None] < t[None, :]).astype(np.float32), (e[None, :] < e[:, None]).astype(np.float32))


def _out_proj(l, a, bmix, x, mod, w_out, ln_g, ln_b, w_router_t, b_router, tri, low, skip_ctx):
    B, L, _ = x.shape
    half = A_WIDTH
    n_tiles = B * (L // ROW_TILE)
    cap = _expert_capacity(n_tiles)
    smem = pl.BlockSpec(memory_space=pltpu.SMEM)
    j_off = CTX_LEN // ROW_TILE if skip_ctx else 0
    rows = functools.partial(_row_spec, j_off=j_off)
    return pl.pallas_call(
        functools.partial(_out_kernel, cap=cap, j_off=j_off),
        grid=(B, L // ROW_TILE - j_off),
        in_specs=[rows(A_WIDTH), rows(B_WIDTH), rows(D_MODEL), _mod_spec(j_off),
                  pl.BlockSpec((None, half, D_MODEL), lambda b, j: (l, 0, 0)),
                  pl.BlockSpec((None, half, D_MODEL), lambda b, j: (l, 1, 0)),
                  _vec_spec(l), _vec_spec(l),
                  pl.BlockSpec((3 * N_EXPERTS, D_MODEL), lambda b, j: (0, 0)),
                  pl.BlockSpec((N_EXPERTS, 1), lambda b, j: (0, 0)),
                  pl.BlockSpec((ROW_TILE, ROW_TILE), lambda b, j: (0, 0)),
                  pl.BlockSpec((N_EXPERTS, N_EXPERTS), lambda b, j: (0, 0))],
        out_specs=[rows(D_MODEL), _pos_spec(j_off), smem, smem, pl.BlockSpec(memory_space=pl.ANY)],
        out_shape=[jax.ShapeDtypeStruct(x.shape, f32), jax.ShapeDtypeStruct((B, 2, L), jnp.int32),
                   jax.ShapeDtypeStruct((n_tiles, 3 * N_EXPERTS), jnp.int32),
                   jax.ShapeDtypeStruct((1, N_EXPERTS), jnp.int32),
                   jax.ShapeDtypeStruct((N_EXPERTS * cap, XS_W), bf16)],
        scratch_shapes=[pltpu.VMEM((N_EXPERTS, ROW_TILE), f32), pltpu.VMEM((N_EXPERTS, ROW_TILE), f32),
                        pltpu.VMEM((2, N_EXPERTS, LANES), jnp.int32), pltpu.SMEM((2, N_EXPERTS, LANES), jnp.int32),
                        pltpu.VMEM((2, XS_ROWS, XS_W), bf16), pltpu.SemaphoreType.DMA((2,)),
                        pltpu.SMEM((N_EXPERTS,), jnp.int32), pltpu.SMEM((2,), jnp.int32)],
        compiler_params=_params("arbitrary", "arbitrary"),
        name="out_proj",
    )(a, bmix, x, mod, w_out, w_out, ln_g, ln_b, w_router_t, b_router, tri, low)


def _expert_capacity(n_tiles):
    rows = n_tiles * (ROW_TILE + XS_BLK)
    return -(-rows // EXP_TILE) * EXP_TILE


def _moe_kernel(te_ref, tb_ref, tv_ref, xs_ref, wg_ref, wu_ref, wd_ref, ys_ref, wg_b, wu_b, wd_b):
    i = pl.program_id(0)
    valid = tv_ref[i]
    new_expert = jnp.logical_or(i == 0, te_ref[i] != te_ref[jnp.maximum(i - 1, 0)])

    @pl.when(new_expert)
    def _():
        wg_b[...] = wg_ref[...].astype(bf16)
        wu_b[...] = wu_ref[...].astype(bf16)
        wd_b[...] = wd_ref[...].astype(bf16)

    @pl.when(valid > 0)
    def _():
        chunk = MOE_CHUNK
        ok = lax.broadcasted_iota(jnp.int32, (EXP_TILE, 1), 0) < valid
        x = jnp.where(ok, xs_ref[:, :D_MODEL], jnp.zeros((), bf16))
        gate_terms = (xs_ref[:, D_MODEL:D_MODEL + 1].astype(f32)
                      + xs_ref[:, D_MODEL + LANES:D_MODEL + LANES + 1].astype(f32))
        gate = jnp.where(ok, gate_terms, 0.0)
        hid = []
        for c in range(D_EXPERT // chunk):
            cols = slice(c * chunk, (c + 1) * chunk)
            hg = jnp.dot(x, wg_b[:, cols], preferred_element_type=f32)
            hu = jnp.dot(x, wu_b[:, cols], preferred_element_type=f32)
            hid.append((hg * _sigmoid(hg) * hu * gate).astype(bf16))
        hid = jnp.concatenate(hid, axis=1)
        for c in range(D_MODEL // chunk):
            cols = slice(c * chunk, (c + 1) * chunk)
            ys_ref[:, cols] = jnp.dot(hid, wd_b[:, cols], preferred_element_type=f32).astype(bf16)


def _expert_tiles(cnt, cap, n_steps):
    cnt = cnt.reshape(N_EXPERTS)
    tiles = (cnt + (EXP_TILE - 1)) // EXP_TILE
    ends = jnp.cumsum(tiles)
    n_valid = ends[-1]
    i = jnp.clip(jnp.arange(n_steps, dtype=jnp.int32), 0, jnp.maximum(n_valid - 1, 0))
    e = jnp.sum((i[:, None] >= ends[None, :]).astype(jnp.int32), axis=1)
    local = i - (ends - tiles)[e]
    rows = jnp.clip(cnt[e] - local * EXP_TILE, 0, EXP_TILE)
    rows = jnp.where(jnp.arange(n_steps) < n_valid, rows, 0)
    return e.astype(jnp.int32), (e * (cap // EXP_TILE) + local).astype(jnp.int32), rows.astype(jnp.int32)


def _moe(l, xs, cnt, n_tiles, w_gate, w_up, w_down):
    cap = _expert_capacity(n_tiles)
    n_steps = n_tiles * XS_ROWS // EXP_TILE + N_EXPERTS
    te, tb, tv = _expert_tiles(cnt, cap, n_steps)
    grid_spec = pltpu.PrefetchScalarGridSpec(
        num_scalar_prefetch=3,
        grid=(n_steps,),
        in_specs=[pl.BlockSpec((EXP_TILE, XS_W), lambda i, te, tb, tv: (tb[i], 0)),
                  pl.BlockSpec((None, None, D_MODEL, D_EXPERT), lambda i, te, tb, tv: (l, te[i], 0, 0)),
                  pl.BlockSpec((None, None, D_MODEL, D_EXPERT), lambda i, te, tb, tv: (l, te[i], 0, 0)),
                  pl.BlockSpec((None, None, D_EXPERT, D_MODEL), lambda i, te, tb, tv: (l, te[i], 0, 0))],
        out_specs=pl.BlockSpec((EXP_TILE, D_MODEL), lambda i, te, tb, tv: (tb[i], 0)),
        scratch_shapes=[pltpu.VMEM((D_MODEL, D_EXPERT), bf16), pltpu.VMEM((D_MODEL, D_EXPERT), bf16),
                        pltpu.VMEM((D_EXPERT, D_MODEL), bf16)],
    )
    return pl.pallas_call(
        _moe_kernel,
        grid_spec=grid_spec,
        out_shape=jax.ShapeDtypeStruct((N_EXPERTS * cap, D_MODEL), bf16),
        compiler_params=_params("arbitrary"),
        name="moe",
    )(te, tb, tv, xs, w_gate, w_up, w_down)


def _final_kernel(tab_ref, x_ref, pos_ref, ys_hbm, mod_ref, g_ref, b_ref, o_ref, y_vmem, sem, *, j_off):
    t, t_next, has_next, _ = _tile_id(j_off)
    step = pl.program_id(0) * pl.num_programs(1) + pl.program_id(1)
    f = _combine_experts(tab_ref, pos_ref, ys_hbm, y_vmem, sem, step, t, t_next, has_next)
    o_ref[...] = _layer_norm(ALPHA * x_ref[...] + mod_ref[5:6, :] * f, g_ref[...], b_ref[...])


def _final_ln(x, moe, mod, ln_g, ln_b):
    B, L, _ = x.shape
    seq = L - CTX_LEN
    skip = CTX_LEN // ROW_TILE
    tab, pos, ys = moe
    lat = pl.BlockSpec((None, ROW_TILE, D_MODEL), lambda b, j: (b, j + skip, 0))
    return pl.pallas_call(
        functools.partial(_final_kernel, j_off=skip),
        grid=(B, seq // ROW_TILE),
        in_specs=[pl.BlockSpec(memory_space=pltpu.SMEM), lat, _pos_spec(skip), pl.BlockSpec(memory_space=pl.ANY),
                  pl.BlockSpec((None, None, 6, D_MODEL), lambda b, j: (b, 1, 0, 0)),
                  _vec_spec(DEPTH - 1), _vec_spec(DEPTH - 1)],
        out_specs=_row_spec(D_MODEL),
        out_shape=jax.ShapeDtypeStruct((B, seq, D_MODEL), f32),
        scratch_shapes=_combine_scratch(),
        compiler_params=_params("arbitrary", "arbitrary"),
        name="final_ln",
    )(tab, x, pos, ys, mod, ln_g, ln_b)


def kernel(x, c, ctx, c_ctx, w_ada, b_ada, w_in, lb_logits, a_norm_g, rpb, w_out, ln1_g, ln1_b,
           w_router, b_router, w_gate, w_up, w_down, ln2_g, ln2_b):
    B, seq, D = x.shape
    L = CTX_LEN + seq
    xs = (ctx, x)

    cc = jnp.concatenate([c, c_ctx[None, :], jnp.zeros((16 - B - 1, D), f32)], axis=0)
    mods = _ada_mods(cc, w_ada, b_ada)
    lat_mod = mods[:, :B].reshape(DEPTH, B, 1, 6, D)
    ctx_mod = jnp.broadcast_to(mods[:, B].reshape(DEPTH, 1, 1, 6, D), (DEPTH, B, 1, 6, D))
    mod_all = jnp.concatenate([ctx_mod, lat_mod], axis=2)

    sm = jax.nn.softmax(lb_logits.astype(f32), axis=1)
    lower = jnp.cumsum(sm, axis=1) - sm[:, :1]

    tri_np, own_np, sgn_np = _scan_constants()
    scan_consts = (jnp.asarray(tri_np, bf16), jnp.asarray(own_np), jnp.asarray(sgn_np))
    cos_np, sin_np = _rope_tables(seq)
    cos_t, sin_t = jnp.asarray(cos_np), jnp.asarray(sin_np)
    mask = jnp.asarray(_window_mask())
    bias_t = _bias_tables(rpb.astype(f32) * LOG2E)

    w_in_b = w_in.astype(bf16)
    w_out_b = w_out.astype(bf16)
    wr_hi = w_router.T.astype(bf16)
    wr_res = w_router.T - wr_hi.astype(f32)
    wr_mid = wr_res.astype(bf16)
    wr_lo = (wr_res - wr_mid.astype(f32)).astype(bf16)
    w_router_t = jnp.concatenate([wr_hi, wr_mid, wr_lo], axis=0)
    b_router_c = b_router.reshape(N_EXPERTS, 1)
    ln1_g3, ln1_b3 = ln1_g.reshape(DEPTH, 1, D), ln1_b.reshape(DEPTH, 1, D)
    ln2_g3, ln2_b3 = ln2_g.reshape(DEPTH, 1, D), ln2_b.reshape(DEPTH, 1, D)

    tri_np, low_np = _dispatch_constants()
    tri, low = jnp.asarray(tri_np, bf16), jnp.asarray(low_np, bf16)
    n_tiles = B * (L // ROW_TILE)

    moe = None
    for l in range(DEPTH):
        pmod = mod_all[l - 1] if l > 0 else None
        xs, u = _in_proj(l, xs, moe, pmod, ln2_g3, ln2_b3, mod_all[l], w_in_b)
        a = _hgrn(u, lower[0, l], lower[1, l], a_norm_g[l], scan_consts)
        bmix = _na(l, u, cos_t, sin_t, bias_t, mask)
        xs, pos, tab, cnt, x_sorted = _out_proj(l, a, bmix, xs, mod_all[l], w_out_b, ln1_g3, ln1_b3,
                                                w_router_t, b_router_c, tri, low, skip_ctx=l == DEPTH - 1)
        y_sorted = _moe(l, x_sorted, cnt, n_tiles, w_gate, w_up, w_down)
        moe = (tab, pos, y_sorted)
    return _final_ln(xs, moe, mod_all[DEPTH - 1], ln2_g3, ln2_b3)
```

```python
import functools

import numpy as np
import jax
import jax.numpy as jnp
from jax import lax
from jax.experimental import pallas as pl
from jax.experimental.pallas import tpu as pltpu

D_MODEL = 1024
DEPTH = 4
GRID_W = 64
CTX_LEN = 256
A_HEADS = 4
A_DK = 128
A_WIDTH = A_HEADS * A_DK
B_HEADS = 8
B_HD = 64
B_WIDTH = B_HEADS * B_HD
D_IN = 5 * A_WIDTH + 3 * B_WIDTH
NA_ROWS = 8
NA_COLS = 16
ROPE_BASE = 10000.0
N_EXPERTS = 16
N_GROUPS = 4
EXPERTS_PER_GROUP = N_EXPERTS // N_GROUPS
D_EXPERT = 512
ALPHA = (2 * DEPTH) ** 0.25
LN_EPS = 1e-5
RMS_EPS = 1e-6
F_FLOOR = 1e-6
MASK_VALUE = -1e30
LOG2E = 1.4426950408889634

LANES = 128
ROW_TILE = 256
XS_BLK = 16
XS_ROWS = 2 * ROW_TILE + N_EXPERTS * XS_BLK
XS_W = D_MODEL + 2 * LANES
DRAIN_GROUP = 8
EXP_TILE = 1024
MOE_CHUNK = 256
SCAN_CHUNK = 256
SCAN_LEVELS = (128, 64, 32, 16, 8, 4)
SCAN_BASE = 4
NA_GROUP = 4
VMEM_LIMIT = 56 * 1024 * 1024

f32 = jnp.float32
bf16 = jnp.bfloat16

_NT = (((1,), (1,)), ((), ()))
_TN = (((0,), (0,)), ((), ()))


def _params(*sem):
    return pltpu.CompilerParams(dimension_semantics=sem, vmem_limit_bytes=VMEM_LIMIT)


def _sigmoid(z, sign=1.0):
    return 1.0 / (1.0 + jnp.exp2(z * (-sign * LOG2E)))


def _layer_norm(h, g, b):
    mu = jnp.mean(h, axis=-1, keepdims=True)
    d = h - mu
    var = jnp.mean(d * d, axis=-1, keepdims=True)
    return d * lax.rsqrt(var + LN_EPS) * g + b


def _ada_kernel(c_ref, w_ref, b_ref, o_ref):
    c = c_ref[...]
    o_ref[...] = jnp.dot(c * _sigmoid(c), w_ref[...], preferred_element_type=f32,
                         precision=lax.Precision.HIGHEST) + b_ref[...]


def _ada_mods(cc, w_ada, b_ada):
    n_col = 4
    tn = 6 * D_MODEL // n_col
    return pl.pallas_call(
        _ada_kernel,
        grid=(DEPTH, n_col),
        in_specs=[
            pl.BlockSpec((16, D_MODEL), lambda l, j: (0, 0)),
            pl.BlockSpec((None, D_MODEL, tn), lambda l, j: (l, 0, j)),
            pl.BlockSpec((None, 1, tn), lambda l, j: (l, 0, j)),
        ],
        out_specs=pl.BlockSpec((None, 16, tn), lambda l, j: (l, 0, j)),
        out_shape=jax.ShapeDtypeStruct((DEPTH, 16, 6 * D_MODEL), f32),
        compiler_params=_params("parallel", "parallel"),
        name="ada_mods",
    )(cc, w_ada, b_ada.reshape(DEPTH, 1, 6 * D_MODEL))


def _drain(copy_of_rows, n_blocks):
    def grouped(i, carry):
        copy_of_rows(DRAIN_GROUP * XS_BLK).wait()
        return carry

    def single(i, carry):
        copy_of_rows(XS_BLK).wait()
        return carry

    lax.fori_loop(0, n_blocks // DRAIN_GROUP, grouped, 0)
    lax.fori_loop(0, lax.rem(n_blocks, DRAIN_GROUP), single, 0)


def _tile_id(j_off):
    b, j = pl.program_id(0), pl.program_id(1)
    nj = pl.num_programs(1)
    tiles_per_batch = nj + j_off
    t = b * tiles_per_batch + j + j_off
    last_j = j == nj - 1
    t_next = jnp.where(last_j, (b + 1) * tiles_per_batch + j_off, t + 1)
    has_next = jnp.logical_not(jnp.logical_and(last_j, b == pl.num_programs(0) - 1))
    return t, t_next, has_next, jnp.logical_and(b == 0, j == 0)


def _combine_experts(tab_ref, pos_ref, ys_hbm, y_vmem, sem, step, t, t_next, has_next):
    is_first = step == 0
    slot = lax.rem(step, 2)

    def block_copy(src_row, dst_row, sl, rows=XS_BLK):
        return pltpu.make_async_copy(ys_hbm.at[pl.ds(pl.multiple_of(src_row, XS_BLK), rows), :],
                                     y_vmem.at[sl, pl.ds(pl.multiple_of(dst_row, XS_BLK), rows), :],
                                     sem.at[sl])

    def fetch(tile, sl):
        @pl.when(is_first)
        def _():
            y_vmem[sl] = jnp.zeros((XS_ROWS, D_MODEL), bf16)

        for e in range(N_EXPERTS):
            base, start, nblk = tab_ref[tile, 3 * e], tab_ref[tile, 3 * e + 1], tab_ref[tile, 3 * e + 2]

            def body(i, carry):
                block_copy(base + i * XS_BLK, start + i * XS_BLK, sl).start()
                return carry

            lax.fori_loop(0, nblk, body, 0)

    def wait_all(tile, sl):
        total = tab_ref[tile, 2]
        for e in range(1, N_EXPERTS):
            total = total + tab_ref[tile, 3 * e + 2]
        _drain(lambda rows: block_copy(0, 0, sl, rows), total)

    @pl.when(is_first)
    def _():
        fetch(t, slot)

    @pl.when(has_next)
    def _():
        fetch(t_next, 1 - slot)

    wait_all(t, slot)
    y = y_vmem[slot]
    r = lax.broadcasted_iota(jnp.int32, (XS_ROWS, ROW_TILE), 0)
    perm = jnp.where(r == pos_ref[0:1, :], 1.0, jnp.where(r == pos_ref[1:2, :], 1.0, 0.0)).astype(bf16)
    return lax.dot_general(perm, y, _TN, preferred_element_type=f32)


def _in_kernel(*refs, with_ln):
    if with_ln:
        (tab_ref, x_ref, pos_ref, ys_hbm, pmod_ref, g_ref, b_ref, mod_ref, w_ref,
         xo_ref, u_ref, y_vmem, sem) = refs
        t, t_next, has_next, _ = _tile_id(0)
        step = pl.program_id(0) * pl.num_programs(1) + pl.program_id(1)
        f = _combine_experts(tab_ref, pos_ref, ys_hbm, y_vmem, sem, step, t, t_next, has_next)
        h = ALPHA * x_ref[...] + pmod_ref[5:6, :] * f
        x = _layer_norm(h, g_ref[...], b_ref[...])
        xo_ref[...] = x
    else:
        ctx_ref, lat_ref, mod_ref, w_ref, xo_ref, u_ref = refs
        x = jnp.where(pl.program_id(1) == 0, ctx_ref[...], lat_ref[...])
        xo_ref[...] = x
    xm = x * (1.0 + mod_ref[1:2, :]) + mod_ref[0:1, :]
    u_ref[...] = jnp.dot(xm.astype(bf16), w_ref[...], preferred_element_type=f32)


def _mod_spec(j_off=0):
    return pl.BlockSpec((None, None, 6, D_MODEL), lambda b, j: (b, jnp.minimum(j + j_off, 1), 0, 0))


def _row_spec(width, j_off=0):
    return pl.BlockSpec((None, ROW_TILE, width), lambda b, j: (b, j + j_off, 0))


def _vec_spec(l):
    return pl.BlockSpec((None, 1, D_MODEL), lambda b, j: (l, 0, 0))


def _pos_spec(j_off=0):
    return pl.BlockSpec((None, 2, ROW_TILE), lambda b, j: (b, 0, j + j_off))


def _combine_scratch():
    return [pltpu.VMEM((2, XS_ROWS, D_MODEL), bf16), pltpu.SemaphoreType.DMA((2,))]


def _in_proj(l, x, moe, pmod, ln_g, ln_b, mod, w_in):
    if moe is None:
        B, L = x[0].shape[0], x[0].shape[1] + x[1].shape[1]
    else:
        B, L, _ = x.shape
    w_spec = pl.BlockSpec((None, D_MODEL, D_IN), lambda b, j: (l, 0, 0))
    u_shape = jax.ShapeDtypeStruct((B, L, D_IN), f32)
    if moe is None:
        ctx, lat = x
        skip = CTX_LEN // ROW_TILE
        return pl.pallas_call(
            functools.partial(_in_kernel, with_ln=False),
            grid=(B, L // ROW_TILE),
            in_specs=[pl.BlockSpec((None, ROW_TILE, D_MODEL), lambda b, j: (b, 0, 0)),
                      pl.BlockSpec((None, ROW_TILE, D_MODEL), lambda b, j: (b, jnp.maximum(j - skip, 0), 0)),
                      _mod_spec(), w_spec],
            out_specs=[_row_spec(D_MODEL), _row_spec(D_IN)],
            out_shape=[jax.ShapeDtypeStruct((B, L, D_MODEL), f32), u_shape],
            compiler_params=_params("parallel", "parallel"),
            name="in_proj0",
        )(ctx, lat, mod, w_in)
    tab, pos, ys = moe
    return pl.pallas_call(
        functools.partial(_in_kernel, with_ln=True),
        grid=(B, L // ROW_TILE),
        in_specs=[pl.BlockSpec(memory_space=pltpu.SMEM), _row_spec(D_MODEL), _pos_spec(),
                  pl.BlockSpec(memory_space=pl.ANY), _mod_spec(), _vec_spec(l - 1), _vec_spec(l - 1),
                  _mod_spec(), w_spec],
        out_specs=[_row_spec(D_MODEL), _row_spec(D_IN)],
        out_shape=[jax.ShapeDtypeStruct(x.shape, f32), u_shape],
        scratch_shapes=_combine_scratch(),
        compiler_params=_params("arbitrary", "arbitrary"),
        name="in_proj",
    )(tab, x, pos, ys, pmod, ln_g, ln_b, mod, w_in)


def _scan_constants():
    C = SCAN_CHUNK
    p = np.arange(C)
    tri = (p[None, :] <= p[:, None]).astype(np.float32)
    owner = np.full((C, C), -1, np.int32)
    signs = []
    for li, m in enumerate(SCAN_LEVELS):
        same = (p[:, None] // (2 * m)) == (p[None, :] // (2 * m))
        owner[same & ((p[:, None] // m) % 2 == 1) & ((p[None, :] // m) % 2 == 0)] = li
        signs.append(np.where((p // m) % 2 == 1, 1.0, -1.0))
    owner[((p[:, None] // SCAN_BASE) == (p[None, :] // SCAN_BASE)) & (p[None, :] <= p[:, None])] = len(SCAN_LEVELS)
    sgn = np.broadcast_to(np.stack(signs)[:, :, None], (len(SCAN_LEVELS), C, LANES)).astype(np.float32)
    return (np.stack([tri, tri[::-1, ::-1]]), np.stack([owner, owner[::-1, ::-1]]),
            np.stack([sgn, sgn[:, ::-1]]))


def _hgrn_kernel(q_ref, ff_ref, fb_ref, v_ref, g_ref, lbf_ref, lbb_ref, ng_ref, t_ref, own_ref, sgn_ref,
                 a_ref, of_scr, ob_scr, *, n_ctx, n_all):
    C = SCAN_CHUNK
    n_lv = len(SCAN_LEVELS)


    def gates(rows, z_ref, lb, d):
        z = z_ref[rows, :]
        lf = jnp.log2(jnp.maximum(lb + (1.0 - lb) * _sigmoid(z), F_FLOOR))
        k = (1.0 - lb) * _sigmoid(z, -1.0)
        qz = q_ref[rows, :]
        q = qz * _sigmoid(qz)
        v = v_ref[rows, :].astype(bf16)
        hi = lf.astype(bf16)
        r1 = lf - hi.astype(f32)
        mid = r1.astype(bf16)
        lo = (r1 - mid.astype(f32)).astype(bf16)
        d3 = jnp.dot(t_ref[d], jnp.concatenate([hi, mid, lo], axis=1), preferred_element_type=f32)
        cum = d3[:, :LANES] + d3[:, LANES:2 * LANES] + d3[:, 2 * LANES:]
        return q, k, v, cum

    def intra(q, k, cum, d):
        last = cum[C - 1:C, :] if d == 0 else cum[0:1, :]

        def rel(block, ref_rows):
            blocks = cum.reshape(C // block, block, LANES)
            if len(ref_rows) == 1:
                ref = blocks[:, ref_rows[0]:ref_rows[0] + 1, :]
            else:
                first = lax.broadcasted_iota(jnp.int32, (1, block, 1), 1) < block // 2
                ref = jnp.where(first, blocks[:, ref_rows[0]:ref_rows[0] + 1, :],
                                blocks[:, ref_rows[1]:ref_rows[1] + 1, :])
            return (blocks - ref).reshape(C, LANES)

        def to_boundary(m):
            blocks = cum.reshape(C // (2 * m), 2 * m, LANES)
            lo, hi = blocks[:, :m, :], blocks[:, m:, :]
            if d == 0:
                ref = lo[:, m - 1:m, :]
                parts = (ref - lo, hi - ref)
            else:
                ref = hi[:, 0:1, :]
                parts = (lo - ref, ref - hi)
            return jnp.concatenate(parts, axis=1).reshape(C, LANES)

        q_in = (q * jnp.exp2(cum)).astype(bf16)
        own = own_ref[d]
        scores = jnp.zeros((C, C), f32)
        for li in range(n_lv + 1):
            m = SCAN_LEVELS[li] if li < n_lv else 0
            if m and m % 8 == 0:
                def halves(x):
                    blocks = x.reshape(C // (2 * m), 2 * m, x.shape[-1])
                    return blocks[:, :m, :], blocks[:, m:, :]

                def join(lo, hi):
                    return jnp.concatenate([lo, hi], axis=1).reshape(C, lo.shape[-1])

                (q_lo, q_hi), (k_lo, k_hi) = halves(q), halves(k)
                mixed = join(k_lo, q_hi) if d == 0 else join(q_lo, k_hi)
                z = (mixed * jnp.exp2(to_boundary(m))).astype(bf16)
                a = lax.dot_general(z, z, _NT, preferred_element_type=f32)
                (a_lo, a_hi), (s_lo, s_hi), (o_lo, o_hi) = halves(a), halves(scores), halves(own)
                if d == 0:
                    scores = join(s_lo, jnp.where(o_hi == li, a_hi, s_hi))
                else:
                    scores = join(jnp.where(o_lo == li, a_lo, s_lo), s_hi)
                continue
            if m:
                wq = jnp.exp2(rel(2 * m, (m - 1,) if d == 0 else (m,)) * sgn_ref[d, li])
                wk = wq
            else:
                nb = SCAN_BASE
                dl = rel(2 * nb, (0, nb) if d == 0 else (nb - 1, 2 * nb - 1))
                wq = jnp.exp2(dl)
                wk = jnp.exp2(-dl)
            a = lax.dot_general((q * wq).astype(bf16), (k * wk).astype(bf16), _NT,
                                preferred_element_type=f32)
            scores = jnp.where(own == li, a, scores)
        kdec = (k * jnp.exp2(last - cum)).astype(bf16)
        return q_in, scores.astype(bf16), kdec, jnp.exp2(last)

    def finish(pending, S, o_scr):
        q_in, scores, kdec, decay, v, row0 = pending
        o = (lax.dot_general(q_in, S.astype(bf16), _NT, preferred_element_type=f32)
             + jnp.dot(scores, v, preferred_element_type=f32))
        o_scr[pl.ds(pl.multiple_of(row0, C), C), :] = o
        return S * decay + lax.dot_general(v, kdec, _TN, preferred_element_type=f32)

    lbf = lbf_ref[...]
    lbb = lbb_ref[...]

    def bwd_chunk(i):
        return jnp.where(i < n_ctx, n_ctx - 1 - i, n_all - 1 - (i - n_ctx))

    def scan_body(i, carry):
        s_f, s_b, pend_f, pend_b = carry
        row_f = i * C
        row_b = bwd_chunk(i) * C
        s_f = finish(pend_f, s_f, of_scr)
        s_b = finish(pend_b, s_b, ob_scr)
        qf, kf, vf, cum_f = gates(pl.ds(pl.multiple_of(row_f, C), C), ff_ref, lbf, 0)
        qb, kb, vb, cum_b = gates(pl.ds(pl.multiple_of(row_b, C), C), fb_ref, lbb, 1)
        pend_f = intra(qf, kf, cum_f, 0) + (vf, row_f)
        pend_b = intra(qb, kb, cum_b, 1) + (vb, row_b)
        return s_f, s_b, pend_f, pend_b

    def idle(row0):
        zc = jnp.zeros((C, LANES), bf16)
        return (zc, jnp.zeros((C, C), bf16), zc, jnp.ones((1, LANES), f32), zc, row0)

    zero = jnp.zeros((A_DK, A_DK), f32)
    init = (zero, zero, idle(jnp.int32(0)), idle(bwd_chunk(jnp.int32(0)) * C))
    s_f, s_b, pend_f, pend_b = lax.fori_loop(0, n_all, scan_body, init, unroll=3)
    finish(pend_f, s_f, of_scr)
    finish(pend_b, s_b, ob_scr)

    ng = ng_ref[...]

    def readout_body(i, carry):
        rows = pl.ds(pl.multiple_of(i * ROW_TILE, ROW_TILE), ROW_TILE)
        o = of_scr[rows, :] + ob_scr[rows, :]
        o = o * lax.rsqrt(jnp.mean(o * o, axis=-1, keepdims=True) + RMS_EPS) * ng
        gz = g_ref[rows, :]
        a_ref[rows, :] = (o * (gz * _sigmoid(gz))).astype(a_ref.dtype)
        return carry

    lax.fori_loop(0, n_all * C // ROW_TILE, readout_body, 0)


def _hgrn(u, lb_f, lb_b, norm_g, scan_consts):
    B, L, _ = u.shape
    hb = A_WIDTH // LANES

    def col(sec):
        return pl.BlockSpec((None, L, LANES), lambda b, h: (b, 0, sec * hb + h))

    def whole(arr):
        return pl.BlockSpec(arr.shape, lambda b, h: (0,) * arr.ndim)

    lb_spec = pl.BlockSpec((None, 1, LANES), lambda b, h: (h, 0, 0))
    return pl.pallas_call(
        functools.partial(_hgrn_kernel, n_ctx=CTX_LEN // SCAN_CHUNK, n_all=L // SCAN_CHUNK),
        grid=(B, A_HEADS),
        in_specs=[col(0), col(1), col(2), col(3), col(4), lb_spec, lb_spec,
                  pl.BlockSpec((1, LANES), lambda b, h: (0, 0))] + [whole(a) for a in scan_consts],
        out_specs=pl.BlockSpec((None, L, LANES), lambda b, h: (b, 0, h)),
        out_shape=jax.ShapeDtypeStruct((B, L, A_WIDTH), bf16),
        scratch_shapes=[pltpu.VMEM((L, LANES), f32), pltpu.VMEM((L, LANES), f32)],
        compiler_params=_params("parallel", "parallel"),
        name="hgrn2",
    )(u, u, u, u, u, lb_f.reshape(A_HEADS, 1, LANES), lb_b.reshape(A_HEADS, 1, LANES),
      norm_g.reshape(1, LANES), *scan_consts)


def _rope_tables(seq):
    t = np.arange(seq)
    pos_r = (t // GRID_W).astype(np.float32)
    pos_c = (t % GRID_W).astype(np.float32)
    quarter = B_HD // 4
    inv = (ROPE_BASE ** (-np.arange(quarter, dtype=np.float32) / quarter)).astype(np.float32)
    lane = np.arange(LANES)
    in_head = lane % B_HD
    pos = np.where((in_head < B_HD // 2)[None, :], pos_r[:, None], pos_c[:, None])
    ang = (pos * inv[lane % quarter][None, :]).astype(np.float32)
    sign = np.where((lane % (2 * quarter)) < quarter, -1.0, 1.0)
    return np.cos(ang).astype(np.float32), (np.sin(ang) * sign[None, :]).astype(np.float32)


def _window_mask():
    c = np.arange(GRID_W)
    cs = np.clip(c - NA_COLS // 2, 0, GRID_W - NA_COLS)
    in_win = (c[None, :] >= cs[:, None]) & (c[None, :] < cs[:, None] + NA_COLS)
    return np.tile(in_win, (1, NA_ROWS)).astype(np.float32)


def _bias_tables(rpb):
    c = np.arange(GRID_W)
    dc = np.clip(c[None, :] - c[:, None], -(NA_COLS - 1), NA_COLS - 1) + (NA_COLS - 1)
    pick = (np.arange(2 * NA_COLS - 1)[:, None, None] == dc[None]).astype(np.float32)
    cols = jnp.einsum('lhdk,kcx->lhdcx', rpb, jnp.asarray(pick), precision=lax.Precision.HIGHEST)
    n_rel = 2 * NA_ROWS - 1
    n_win = NA_ROWS * GRID_W
    t = pl.pallas_call(
        _bias_expand_kernel,
        grid=(DEPTH * B_HEADS,),
        in_specs=[pl.BlockSpec((None, n_rel, GRID_W, GRID_W), lambda i: (i, 0, 0, 0))],
        out_specs=pl.BlockSpec((None, NA_ROWS, GRID_W, n_win), lambda i: (i, 0, 0, 0)),
        out_shape=jax.ShapeDtypeStruct((DEPTH * B_HEADS, NA_ROWS, GRID_W, n_win), f32),
        compiler_params=_params("parallel"),
        name="bias_expand",
    )(cols.reshape(DEPTH * B_HEADS, n_rel, GRID_W, GRID_W))
    return t.reshape(DEPTH, B_HEADS, NA_ROWS, GRID_W, n_win)


def _bias_expand_kernel(c_ref, o_ref):
    per_vreg = LANES // GRID_W
    for o in range(NA_ROWS):
        for g in range(NA_ROWS // per_vreg):
            parts = [c_ref[o + g * per_vreg + p] for p in range(per_vreg)]
            o_ref[o, :, g * LANES:(g + 1) * LANES] = jnp.concatenate(parts, axis=1)


def _na_kernel(q_ref, k_ref, v_ref, cos_ref, sin_ref, bias_ref, mask_ref, o_ref,
               qr_scr, kr_scr, qp_scr, kc_scr, v_scr, *, seq, rows_grid, ctx_out):
    hd = B_HD
    scale = hd ** -0.5 * LOG2E
    n_win = NA_ROWS * GRID_W
    lane = lax.broadcasted_iota(jnp.int32, (ROW_TILE, LANES), 1)
    first_half = (lane % (hd // 2)) < (hd // 4)
    head0 = lane < hd

    def swap_halves(x):
        return jnp.where(first_half, pltpu.roll(x, LANES - hd // 4, axis=1), pltpu.roll(x, hd // 4, axis=1))

    def store_per_head(dst, rows, x):
        dst[0, rows, :] = jnp.where(head0, x, 0.0).astype(bf16)
        dst[1, rows, :] = jnp.where(head0, 0.0, x).astype(bf16)

    ctx_rows = pl.ds(0, CTX_LEN)
    if ctx_out:
        store_per_head(qp_scr, ctx_rows, q_ref[ctx_rows, :] * scale)
    kc_scr[...] = k_ref[ctx_rows, :].astype(bf16)
    v_scr[ctx_rows, :] = v_ref[ctx_rows, :].astype(bf16)

    def prep(i, carry):
        lrows = pl.ds(pl.multiple_of(i * ROW_TILE, ROW_TILE), ROW_TILE)
        rows = pl.ds(pl.multiple_of(CTX_LEN + i * ROW_TILE, ROW_TILE), ROW_TILE)
        q = q_ref[rows, :] * scale
        k = k_ref[rows, :]
        cos = cos_ref[lrows, :]
        sin = sin_ref[lrows, :]
        store_per_head(qp_scr, rows, q)
        v_scr[rows, :] = v_ref[rows, :].astype(bf16)
        store_per_head(qr_scr, lrows, q * cos + swap_halves(q) * sin)
        kr_scr[lrows, :] = (k * cos + swap_halves(k) * sin).astype(bf16)
        return carry

    lax.fori_loop(0, seq // ROW_TILE, prep, 0)

    def pick_heads(o0, o1):
        keep0 = lax.broadcasted_iota(jnp.int32, o0.shape, 1) < hd
        return jnp.where(keep0, o0, o1)

    if ctx_out:
        outs = []
        for h in range(2):
            s = lax.dot_general(qp_scr[h, ctx_rows, :], kc_scr[...], _NT, preferred_element_type=f32)
            e = jnp.exp2(s - jnp.max(s, axis=-1, keepdims=True))
            o = jnp.dot(e.astype(bf16), v_scr[ctx_rows, :], preferred_element_type=f32)
            outs.append(o / jnp.sum(e, axis=-1, keepdims=True))
        o_ref[ctx_rows, :] = pick_heads(*outs).astype(o_ref.dtype)

    def rows_body(it, carry):
        chains = []
        for g in range(NA_GROUP):
            r = it * NA_GROUP + g
            rs = jnp.clip(r - NA_ROWS // 2, 0, rows_grid - NA_ROWS)
            for h in range(2):
                chains.append(dict(
                    h=h, off=rs - r + (NA_ROWS - 1),
                    qrows=pl.ds(pl.multiple_of(r * GRID_W, GRID_W), GRID_W),
                    krows=pl.ds(pl.multiple_of(rs * GRID_W, GRID_W), n_win),
                    arows=pl.ds(pl.multiple_of(CTX_LEN + r * GRID_W, GRID_W), GRID_W),
                    vrows=pl.ds(pl.multiple_of(CTX_LEN + rs * GRID_W, GRID_W), n_win)))
        for c in chains:
            h = c["h"]
            c["s_lat"] = lax.dot_general(qr_scr[h, c["qrows"], :], kr_scr[c["krows"], :], _NT,
                                         preferred_element_type=f32)
            c["s_ctx"] = lax.dot_general(qp_scr[h, c["arows"], :], kc_scr[...], _NT, preferred_element_type=f32)
        in_win = mask_ref[...] > 0.5
        for c in chains:
            s_lat = jnp.where(in_win, c["s_lat"] + bias_ref[c["h"], c["off"]], MASK_VALUE)
            m = jnp.maximum(jnp.max(s_lat, axis=-1, keepdims=True), jnp.max(c["s_ctx"], axis=-1, keepdims=True))
            e_lat = jnp.exp2(s_lat - m)
            e_ctx = jnp.exp2(c["s_ctx"] - m)
            c["den"] = jnp.sum(e_lat, axis=-1, keepdims=True) + jnp.sum(e_ctx, axis=-1, keepdims=True)
            c["e_lat"] = e_lat.astype(bf16)
            c["e_ctx"] = e_ctx.astype(bf16)
        for c in chains:
            c["o"] = (jnp.dot(c["e_lat"], v_scr[c["vrows"], :], preferred_element_type=f32)
                      + jnp.dot(c["e_ctx"], v_scr[ctx_rows, :], preferred_element_type=f32))
        for g in range(NA_GROUP):
            c0, c1 = chains[2 * g:2 * g + 2]
            o_ref[c0["arows"], :] = pick_heads(c0["o"] / c0["den"], c1["o"] / c1["den"]).astype(o_ref.dtype)
        return carry

    lax.fori_loop(0, rows_grid // NA_GROUP, rows_body, 0)


def _na(l, u, cos_t, sin_t, bias_t, mask):
    B, L, _ = u.shape
    seq = L - CTX_LEN
    first = 5 * A_WIDTH // LANES
    pairs = B_WIDTH // LANES

    def col(sec):
        return pl.BlockSpec((None, L, LANES), lambda b, h: (b, 0, first + sec * pairs + h))

    n_win = NA_ROWS * GRID_W
    return pl.pallas_call(
        functools.partial(_na_kernel, seq=seq, rows_grid=seq // GRID_W, ctx_out=l < DEPTH - 1),
        grid=(B, pairs),
        in_specs=[col(0), col(1), col(2),
                  pl.BlockSpec((seq, LANES), lambda b, h: (0, 0)),
                  pl.BlockSpec((seq, LANES), lambda b, h: (0, 0)),
                  pl.BlockSpec((None, 2, NA_ROWS, GRID_W, n_win), lambda b, h: (l, h, 0, 0, 0)),
                  pl.BlockSpec((GRID_W, n_win), lambda b, h: (0, 0))],
        out_specs=pl.BlockSpec((None, L, LANES), lambda b, h: (b, 0, h)),
        out_shape=jax.ShapeDtypeStruct((B, L, B_WIDTH), bf16),
        scratch_shapes=[pltpu.VMEM((2, seq, LANES), bf16), pltpu.VMEM((seq, LANES), bf16),
                        pltpu.VMEM((2, L, LANES), bf16), pltpu.VMEM((CTX_LEN, LANES), bf16),
                        pltpu.VMEM((L, LANES), bf16)],
        compiler_params=_params("parallel", "parallel"),
        name="na_attn",
    )(u, u, u, cos_t, sin_t, bias_t, mask)


def _route(logits, b_router, sel_ref, gate_ref):
    aff = _sigmoid(logits)
    sel = aff + b_router
    s = [sel[e:e + 1, :] for e in range(N_EXPERTS)]
    one = lambda cond: jnp.where(cond, 1.0, 0.0)
    scores = []
    for g in range(N_GROUPS):
        s0, s1, s2, s3 = s[g * EXPERTS_PER_GROUP:(g + 1) * EXPERTS_PER_GROUP]
        hi01, lo01 = jnp.maximum(s0, s1), jnp.minimum(s0, s1)
        hi23, lo23 = jnp.maximum(s2, s3), jnp.minimum(s2, s3)
        scores.append(jnp.maximum(hi01, hi23) + jnp.maximum(jnp.minimum(hi01, hi23), jnp.maximum(lo01, lo23)))
    weights = []
    chosen = []
    for g in range(N_GROUPS):
        gsel = None
        for g2 in range(N_GROUPS):
            if g2 == g:
                continue
            t = one(scores[g2] < scores[g]) if g2 < g else one(scores[g2] <= scores[g])
            gsel = t if gsel is None else gsel * t
        for i in range(EXPERTS_PER_GROUP):
            e = g * EXPERTS_PER_GROUP + i
            rank = None
            for j in range(EXPERTS_PER_GROUP):
                if j == i:
                    continue
                e2 = g * EXPERTS_PER_GROUP + j
                t = one(s[e2] >= s[e]) if j < i else one(s[e2] > s[e])
                rank = t if rank is None else rank + t
            chosen.append(gsel * one(rank < 1.5))
            weights.append(chosen[-1] * aff[e:e + 1, :])
    total = weights[0]
    for w in weights[1:]:
        total = total + w
    inv = 1.0 / total
    for e in range(N_EXPERTS):
        sel_ref[e:e + 1, :] = chosen[e]
        gate_ref[e:e + 1, :] = weights[e] * inv


def _out_kernel(a_ref, b_ref, x_ref, mod_ref, wa_ref, wb_ref, g_ref, bt_ref, wr_ref, br_ref, tri_ref, low_ref,
                x1_ref, pos_ref, tab_ref, cnt_ref, xs_hbm,
                sel_scr, gate_scr, meta_v, meta_s, xs_vmem, sem, cnt_smem, nblk_smem, *, cap, j_off):
    t, _, has_next, is_first = _tile_id(j_off)
    slot = lax.rem(pl.program_id(0) * pl.num_programs(1) + pl.program_id(1), 2)

    @pl.when(is_first)
    def _():
        for e in range(N_EXPERTS):
            cnt_smem[e] = 0
        nblk_smem[0] = 0
        nblk_smem[1] = 0

    y = (jnp.dot(a_ref[...], wa_ref[...], preferred_element_type=f32)
         + jnp.dot(b_ref[...], wb_ref[...], preferred_element_type=f32))
    x1 = _layer_norm(ALPHA * x_ref[...] + mod_ref[2:3, :] * y, g_ref[...], bt_ref[...])
    x1_ref[...] = x1
    xm = x1 * (1.0 + mod_ref[4:5, :]) + mod_ref[3:4, :]
    xm_hi = xm.astype(bf16)
    xm_mid = (xm - xm_hi.astype(f32)).astype(bf16)
    E = N_EXPERTS
    r_hi = lax.dot_general(wr_ref[...], xm_hi, _NT, preferred_element_type=f32)
    r_mid = lax.dot_general(wr_ref[0:2 * E, :], xm_mid, _NT, preferred_element_type=f32)
    logits = ((r_hi[0:E] + r_mid[0:E]) + (r_hi[E:2 * E] + r_mid[E:2 * E])) + r_hi[2 * E:3 * E]
    _route(logits, br_ref[...], sel_scr, gate_scr)

    sel = sel_scr[...]
    gate = gate_scr[...]
    sel_b = sel.astype(bf16)
    rank = jnp.dot(sel_b, tri_ref[...], preferred_element_type=f32)
    below = jnp.dot(low_ref[...], sel_b, preferred_element_type=f32)
    n = jnp.sum(sel, axis=1, keepdims=True)
    npad = jnp.broadcast_to(jnp.floor((n + (XS_BLK - 1)) * (1.0 / XS_BLK)) * XS_BLK, (N_EXPERTS, LANES))
    start = jnp.dot(low_ref[...], npad.astype(bf16), preferred_element_type=f32)
    pos = start[:, :1] + rank
    first = jnp.where(below < 0.5, sel, 0.0)
    second = sel - first
    pos0 = jnp.sum(first * pos, axis=0, keepdims=True).astype(jnp.int32)
    pos1 = jnp.sum(second * pos, axis=0, keepdims=True).astype(jnp.int32)
    g0 = jnp.sum(first * gate, axis=0, keepdims=True)
    g1 = jnp.sum(second * gate, axis=0, keepdims=True)
    pos_ref[0:1, :] = pos0
    pos_ref[1:2, :] = pos1
    meta_v[0] = start.astype(jnp.int32)
    meta_v[1] = (npad * (1.0 / XS_BLK)).astype(jnp.int32)
    pltpu.sync_copy(meta_v, meta_s)

    r = lax.broadcasted_iota(jnp.int32, (XS_ROWS, ROW_TILE), 0)
    hit0 = r == pos0
    hit1 = r == pos1
    perm = jnp.where(hit0, 1.0, jnp.where(hit1, 1.0, 0.0)).astype(bf16)
    gate_sorted = jnp.sum(jnp.where(hit0, g0, jnp.where(hit1, g1, 0.0)), axis=1, keepdims=True)
    xs = jnp.dot(perm, xm_hi, preferred_element_type=f32)

    def block_copy(src_row, dst_row, sl, rows=XS_BLK):
        return pltpu.make_async_copy(xs_vmem.at[sl, pl.ds(pl.multiple_of(src_row, XS_BLK), rows), :],
                                     xs_hbm.at[pl.ds(pl.multiple_of(dst_row, XS_BLK), rows), :],
                                     sem.at[sl])

    def wait_blocks(count, sl):
        _drain(lambda rows: block_copy(0, 0, sl, rows), count)

    wait_blocks(nblk_smem[slot], slot)
    xs_vmem[slot, :, :D_MODEL] = xs.astype(bf16)
    gate_hi = gate_sorted.astype(bf16)
    gate_lo = (gate_sorted - gate_hi.astype(f32)).astype(bf16)
    xs_vmem[slot, :, D_MODEL:D_MODEL + LANES] = jnp.broadcast_to(gate_hi, (XS_ROWS, LANES))
    xs_vmem[slot, :, D_MODEL + LANES:] = jnp.broadcast_to(gate_lo, (XS_ROWS, LANES))
    total = 0
    for e in range(N_EXPERTS):
        start_e = meta_s[0, e, 0]
        nblk_e = meta_s[1, e, 0]
        base = e * cap + cnt_smem[e]
        tab_ref[t, 3 * e] = base
        tab_ref[t, 3 * e + 1] = start_e
        tab_ref[t, 3 * e + 2] = nblk_e

        def body(i, carry):
            block_copy(start_e + i * XS_BLK, base + i * XS_BLK, slot).start()
            return carry

        lax.fori_loop(0, nblk_e, body, 0)
        cnt_smem[e] = cnt_smem[e] + nblk_e * XS_BLK
        total = total + nblk_e
    nblk_smem[slot] = total

    @pl.when(jnp.logical_not(has_next))
    def _():
        wait_blocks(nblk_smem[slot], slot)
        wait_blocks(nblk_smem[1 - slot], 1 - slot)
        for e in range(N_EXPERTS):
            cnt_ref[0, e] = cnt_smem[e]


def _dispatch_constants():
    t = np.arange(ROW_TILE)
    e = np.arange(N_EXPERTS)
    return ((t[:, None] < t[None, :]).astype(np.float32), (e[None, :] < e[:, None]).astype(np.float32))


def _out_proj(l, a, bmix, x, mod, w_out, ln_g, ln_b, w_router_t, b_router, tri, low, skip_ctx):
    B, L, _ = x.shape
    half = A_WIDTH
    n_tiles = B * (L // ROW_TILE)
    cap = _expert_capacity(n_tiles)
    smem = pl.BlockSpec(memory_space=pltpu.SMEM)
    j_off = CTX_LEN // ROW_TILE if skip_ctx else 0
    rows = functools.partial(_row_spec, j_off=j_off)
    return pl.pallas_call(
        functools.partial(_out_kernel, cap=cap, j_off=j_off),
        grid=(B, L // ROW_TILE - j_off),
        in_specs=[rows(A_WIDTH), rows(B_WIDTH), rows(D_MODEL), _mod_spec(j_off),
                  pl.BlockSpec((None, half, D_MODEL), lambda b, j: (l, 0, 0)),
                  pl.BlockSpec((None, half, D_MODEL), lambda b, j: (l, 1, 0)),
                  _vec_spec(l), _vec_spec(l),
                  pl.BlockSpec((3 * N_EXPERTS, D_MODEL), lambda b, j: (0, 0)),
                  pl.BlockSpec((N_EXPERTS, 1), lambda b, j: (0, 0)),
                  pl.BlockSpec((ROW_TILE, ROW_TILE), lambda b, j: (0, 0)),
                  pl.BlockSpec((N_EXPERTS, N_EXPERTS), lambda b, j: (0, 0))],
        out_specs=[rows(D_MODEL), _pos_spec(j_off), smem, smem, pl.BlockSpec(memory_space=pl.ANY)],
        out_shape=[jax.ShapeDtypeStruct(x.shape, f32), jax.ShapeDtypeStruct((B, 2, L), jnp.int32),
                   jax.ShapeDtypeStruct((n_tiles, 3 * N_EXPERTS), jnp.int32),
                   jax.ShapeDtypeStruct((1, N_EXPERTS), jnp.int32),
                   jax.ShapeDtypeStruct((N_EXPERTS * cap, XS_W), bf16)],
        scratch_shapes=[pltpu.VMEM((N_EXPERTS, ROW_TILE), f32), pltpu.VMEM((N_EXPERTS, ROW_TILE), f32),
                        pltpu.VMEM((2, N_EXPERTS, LANES), jnp.int32), pltpu.SMEM((2, N_EXPERTS, LANES), jnp.int32),
                        pltpu.VMEM((2, XS_ROWS, XS_W), bf16), pltpu.SemaphoreType.DMA((2,)),
                        pltpu.SMEM((N_EXPERTS,), jnp.int32), pltpu.SMEM((2,), jnp.int32)],
        compiler_params=_params("arbitrary", "arbitrary"),
        name="out_proj",
    )(a, bmix, x, mod, w_out, w_out, ln_g, ln_b, w_router_t, b_router, tri, low)


def _expert_capacity(n_tiles):
    rows = n_tiles * (ROW_TILE + XS_BLK)
    return -(-rows // EXP_TILE) * EXP_TILE


def _moe_kernel(te_ref, tb_ref, tv_ref, xs_ref, wg_ref, wu_ref, wd_ref, ys_ref, wg_b, wu_b, wd_b):
    i = pl.program_id(0)
    valid = tv_ref[i]
    new_expert = jnp.logical_or(i == 0, te_ref[i] != te_ref[jnp.maximum(i - 1, 0)])

    @pl.when(new_expert)
    def _():
        wg_b[...] = wg_ref[...].astype(bf16)
        wu_b[...] = wu_ref[...].astype(bf16)
        wd_b[...] = wd_ref[...].astype(bf16)

    @pl.when(valid > 0)
    def _():
        chunk = MOE_CHUNK
        ok = lax.broadcasted_iota(jnp.int32, (EXP_TILE, 1), 0) < valid
        x = jnp.where(ok, xs_ref[:, :D_MODEL], jnp.zeros((), bf16))
        gate_terms = (xs_ref[:, D_MODEL:D_MODEL + 1].astype(f32)
                      + xs_ref[:, D_MODEL + LANES:D_MODEL + LANES + 1].astype(f32))
        gate = jnp.where(ok, gate_terms, 0.0)
        hid = []
        for c in range(D_EXPERT // chunk):
            cols = slice(c * chunk, (c + 1) * chunk)
            hg = jnp.dot(x, wg_b[:, cols], preferred_element_type=f32)
            hu = jnp.dot(x, wu_b[:, cols], preferred_element_type=f32)
            hid.append((hg * _sigmoid(hg) * hu * gate).astype(bf16))
        hid = jnp.concatenate(hid, axis=1)
        for c in range(D_MODEL // chunk):
            cols = slice(c * chunk, (c + 1) * chunk)
            ys_ref[:, cols] = jnp.dot(hid, wd_b[:, cols], preferred_element_type=f32).astype(bf16)


def _expert_tiles(cnt, cap, n_steps):
    cnt = cnt.reshape(N_EXPERTS)
    tiles = (cnt + (EXP_TILE - 1)) // EXP_TILE
    ends = jnp.cumsum(tiles)
    n_valid = ends[-1]
    i = jnp.clip(jnp.arange(n_steps, dtype=jnp.int32), 0, jnp.maximum(n_valid - 1, 0))
    e = jnp.sum((i[:, None] >= ends[None, :]).astype(jnp.int32), axis=1)
    local = i - (ends - tiles)[e]
    rows = jnp.clip(cnt[e] - local * EXP_TILE, 0, EXP_TILE)
    rows = jnp.where(jnp.arange(n_steps) < n_valid, rows, 0)
    return e.astype(jnp.int32), (e * (cap // EXP_TILE) + local).astype(jnp.int32), rows.astype(jnp.int32)


def _moe(l, xs, cnt, n_tiles, w_gate, w_up, w_down):
    cap = _expert_capacity(n_tiles)
    n_steps = n_tiles * XS_ROWS // EXP_TILE + N_EXPERTS
    te, tb, tv = _expert_tiles(cnt, cap, n_steps)
    grid_spec = pltpu.PrefetchScalarGridSpec(
        num_scalar_prefetch=3,
        grid=(n_steps,),
        in_specs=[pl.BlockSpec((EXP_TILE, XS_W), lambda i, te, tb, tv: (tb[i], 0)),
                  pl.BlockSpec((None, None, D_MODEL, D_EXPERT), lambda i, te, tb, tv: (l, te[i], 0, 0)),
                  pl.BlockSpec((None, None, D_MODEL, D_EXPERT), lambda i, te, tb, tv: (l, te[i], 0, 0)),
                  pl.BlockSpec((None, None, D_EXPERT, D_MODEL), lambda i, te, tb, tv: (l, te[i], 0, 0))],
        out_specs=pl.BlockSpec((EXP_TILE, D_MODEL), lambda i, te, tb, tv: (tb[i], 0)),
        scratch_shapes=[pltpu.VMEM((D_MODEL, D_EXPERT), bf16), pltpu.VMEM((D_MODEL, D_EXPERT), bf16),
                        pltpu.VMEM((D_EXPERT, D_MODEL), bf16)],
    )
    return pl.pallas_call(
        _moe_kernel,
        grid_spec=grid_spec,
        out_shape=jax.ShapeDtypeStruct((N_EXPERTS * cap, D_MODEL), bf16),
        compiler_params=_params("arbitrary"),
        name="moe",
    )(te, tb, tv, xs, w_gate, w_up, w_down)


def _final_kernel(tab_ref, x_ref, pos_ref, ys_hbm, mod_ref, g_ref, b_ref, o_ref, y_vmem, sem, *, j_off):
    t, t_next, has_next, _ = _tile_id(j_off)
    step = pl.program_id(0) * pl.num_programs(1) + pl.program_id(1)
    f = _combine_experts(tab_ref, pos_ref, ys_hbm, y_vmem, sem, step, t, t_next, has_next)
    o_ref[...] = _layer_norm(ALPHA * x_ref[...] + mod_ref[5:6, :] * f, g_ref[...], b_ref[...])


def _final_ln(x, moe, mod, ln_g, ln_b):
    B, L, _ = x.shape
    seq = L - CTX_LEN
    skip = CTX_LEN // ROW_TILE
    tab, pos, ys = moe
    lat = pl.BlockSpec((None, ROW_TILE, D_MODEL), lambda b, j: (b, j + skip, 0))
    return pl.pallas_call(
        functools.partial(_final_kernel, j_off=skip),
        grid=(B, seq // ROW_TILE),
        in_specs=[pl.BlockSpec(memory_space=pltpu.SMEM), lat, _pos_spec(skip), pl.BlockSpec(memory_space=pl.ANY),
                  pl.BlockSpec((None, None, 6, D_MODEL), lambda b, j: (b, 1, 0, 0)),
                  _vec_spec(DEPTH - 1), _vec_spec(DEPTH - 1)],
        out_specs=_row_spec(D_MODEL),
        out_shape=jax.ShapeDtypeStruct((B, seq, D_MODEL), f32),
        scratch_shapes=_combine_scratch(),
        compiler_params=_params("arbitrary", "arbitrary"),
        name="final_ln",
    )(tab, x, pos, ys, mod, ln_g, ln_b)


def kernel(x, c, ctx, c_ctx, w_ada, b_ada, w_in, lb_logits, a_norm_g, rpb, w_out, ln1_g, ln1_b,
           w_router, b_router, w_gate, w_up, w_down, ln2_g, ln2_b):
    B, seq, D = x.shape
    L = CTX_LEN + seq
    xs = (ctx, x)

    cc = jnp.concatenate([c, c_ctx[None, :], jnp.zeros((16 - B - 1, D), f32)], axis=0)
    mods = _ada_mods(cc, w_ada, b_ada)
    lat_mod = mods[:, :B].reshape(DEPTH, B, 1, 6, D)
    ctx_mod = jnp.broadcast_to(mods[:, B].reshape(DEPTH, 1, 1, 6, D), (DEPTH, B, 1, 6, D))
    mod_all = jnp.concatenate([ctx_mod, lat_mod], axis=2)

    sm = jax.nn.softmax(lb_logits.astype(f32), axis=1)
    lower = jnp.cumsum(sm, axis=1) - sm[:, :1]

    tri_np, own_np, sgn_np = _scan_constants()
    scan_consts = (jnp.asarray(tri_np, bf16), jnp.asarray(own_np), jnp.asarray(sgn_np))
    cos_np, sin_np = _rope_tables(seq)
    cos_t, sin_t = jnp.asarray(cos_np), jnp.asarray(sin_np)
    mask = jnp.asarray(_window_mask())
    bias_t = _bias_tables(rpb.astype(f32) * LOG2E)

    w_in_b = w_in.astype(bf16)
    w_out_b = w_out.astype(bf16)
    wr_hi = w_router.T.astype(bf16)
    wr_res = w_router.T - wr_hi.astype(f32)
    wr_mid = wr_res.astype(bf16)
    wr_lo = (wr_res - wr_mid.astype(f32)).astype(bf16)
    w_router_t = jnp.concatenate([wr_hi, wr_mid, wr_lo], axis=0)
    b_router_c = b_router.reshape(N_EXPERTS, 1)
    ln1_g3, ln1_b3 = ln1_g.reshape(DEPTH, 1, D), ln1_b.reshape(DEPTH, 1, D)
    ln2_g3, ln2_b3 = ln2_g.reshape(DEPTH, 1, D), ln2_b.reshape(DEPTH, 1, D)

    tri_np, low_np = _dispatch_constants()
    tri, low = jnp.asarray(tri_np, bf16), jnp.asarray(low_np, bf16)
    n_tiles = B * (L // ROW_TILE)

    moe = None
    for l in range(DEPTH):
        pmod = mod_all[l - 1] if l > 0 else None
        xs, u = _in_proj(l, xs, moe, pmod, ln2_g3, ln2_b3, mod_all[l], w_in_b)
        a = _hgrn(u, lower[0, l], lower[1, l], a_norm_g[l], scan_consts)
        bmix = _na(l, u, cos_t, sin_t, bias_t, mask)
        xs, pos, tab, cnt, x_sorted = _out_proj(l, a, bmix, xs, mod_all[l], w_out_b, ln1_g3, ln1_b3,
                                                w_router_t, b_router_c, tri, low, skip_ctx=l == DEPTH - 1)
        y_sorted = _moe(l, x_sorted, cnt, n_tiles, w_gate, w_up, w_down)
        moe = (tab, pos, y_sorted)
    return _final_ln(xs, moe, mod_all[DEPTH - 1], ln2_g3, ln2_b3)
```

```python
import functools

import numpy as np
import jax
import jax.numpy as jnp
from jax import lax
from jax.experimental import pallas as pl
from jax.experimental.pallas import tpu as pltpu

D_MODEL = 1024
DEPTH = 4
GRID_W = 64
CTX_LEN = 256
A_HEADS = 4
A_DK = 128
A_WIDTH = A_HEADS * A_DK
B_HEADS = 8
B_HD = 64
B_WIDTH = B_HEADS * B_HD
D_IN = 5 * A_WIDTH + 3 * B_WIDTH
NA_ROWS = 8
NA_COLS = 16
ROPE_BASE = 10000.0
N_EXPERTS = 16
N_GROUPS = 4
EXPERTS_PER_GROUP = N_EXPERTS // N_GROUPS
D_EXPERT = 512
ALPHA = (2 * DEPTH) ** 0.25
LN_EPS = 1e-5
RMS_EPS = 1e-6
F_FLOOR = 1e-6
MASK_VALUE = -1e30
LOG2E = 1.4426950408889634

LANES = 128
ROW_TILE = 256
XS_BLK = 16
XS_ROWS = 2 * ROW_TILE + N_EXPERTS * XS_BLK
XS_W = D_MODEL + 2 * LANES
DRAIN_GROUP = 8
EXP_TILE = 1024
MOE_CHUNK = 256
SCAN_CHUNK = 256
SCAN_LEVELS = (128, 64, 32, 16, 8, 4)
SCAN_BASE = 4
NA_GROUP = 4
VMEM_LIMIT = 56 * 1024 * 1024

f32 = jnp.float32
bf16 = jnp.bfloat16

_NT = (((1,), (1,)), ((), ()))
_TN = (((0,), (0,)), ((), ()))


def _params(*sem):
    return pltpu.CompilerParams(dimension_semantics=sem, vmem_limit_bytes=VMEM_LIMIT)


def _sigmoid(z, sign=1.0):
    return 1.0 / (1.0 + jnp.exp2(z * (-sign * LOG2E)))


def _layer_norm(h, g, b):
    mu = jnp.mean(h, axis=-1, keepdims=True)
    d = h - mu
    var = jnp.mean(d * d, axis=-1, keepdims=True)
    return d * lax.rsqrt(var + LN_EPS) * g + b


def _ada_kernel(c_ref, w_ref, b_ref, o_ref):
    c = c_ref[...]
    o_ref[...] = jnp.dot(c * _sigmoid(c), w_ref[...], preferred_element_type=f32,
                         precision=lax.Precision.HIGHEST) + b_ref[...]


def _ada_mods(cc, w_ada, b_ada):
    n_col = 4
    tn = 6 * D_MODEL // n_col
    return pl.pallas_call(
        _ada_kernel,
        grid=(DEPTH, n_col),
        in_specs=[
            pl.BlockSpec((16, D_MODEL), lambda l, j: (0, 0)),
            pl.BlockSpec((None, D_MODEL, tn), lambda l, j: (l, 0, j)),
            pl.BlockSpec((None, 1, tn), lambda l, j: (l, 0, j)),
        ],
        out_specs=pl.BlockSpec((None, 16, tn), lambda l, j: (l, 0, j)),
        out_shape=jax.ShapeDtypeStruct((DEPTH, 16, 6 * D_MODEL), f32),
        compiler_params=_params("parallel", "parallel"),
        name="ada_mods",
    )(cc, w_ada, b_ada.reshape(DEPTH, 1, 6 * D_MODEL))


def _drain(copy_of_rows, n_blocks):
    def grouped(i, carry):
        copy_of_rows(DRAIN_GROUP * XS_BLK).wait()
        return carry

    def single(i, carry):
        copy_of_rows(XS_BLK).wait()
        return carry

    lax.fori_loop(0, n_blocks // DRAIN_GROUP, grouped, 0)
    lax.fori_loop(0, lax.rem(n_blocks, DRAIN_GROUP), single, 0)


def _tile_id(j_off):
    b, j = pl.program_id(0), pl.program_id(1)
    nj = pl.num_programs(1)
    tiles_per_batch = nj + j_off
    t = b * tiles_per_batch + j + j_off
    last_j = j == nj - 1
    t_next = jnp.where(last_j, (b + 1) * tiles_per_batch + j_off, t + 1)
    has_next = jnp.logical_not(jnp.logical_and(last_j, b == pl.num_programs(0) - 1))
    return t, t_next, has_next, jnp.logical_and(b == 0, j == 0)


def _combine_experts(tab_ref, pos_ref, ys_hbm, y_vmem, sem, step, t, t_next, has_next):
    is_first = step == 0
    slot = lax.rem(step, 2)

    def block_copy(src_row, dst_row, sl, rows=XS_BLK):
        return pltpu.make_async_copy(ys_hbm.at[pl.ds(pl.multiple_of(src_row, XS_BLK), rows), :],
                                     y_vmem.at[sl, pl.ds(pl.multiple_of(dst_row, XS_BLK), rows), :],
                                     sem.at[sl])

    def fetch(tile, sl):
        @pl.when(is_first)
        def _():
            y_vmem[sl] = jnp.zeros((XS_ROWS, D_MODEL), bf16)

        for e in range(N_EXPERTS):
            base, start, nblk = tab_ref[tile, 3 * e], tab_ref[tile, 3 * e + 1], tab_ref[tile, 3 * e + 2]

            def body(i, carry):
                block_copy(base + 2 * i * XS_BLK, start + 2 * i * XS_BLK, sl).start(priority=0)

                @pl.when(2 * i + 1 < nblk)
                def _():
                    block_copy(base + (2 * i + 1) * XS_BLK, start + (2 * i + 1) * XS_BLK, sl).start(priority=1)

                return carry

            lax.fori_loop(0, (nblk + 1) // 2, body, 0)

    def wait_all(tile, sl):
        total = tab_ref[tile, 2]
        for e in range(1, N_EXPERTS):
            total = total + tab_ref[tile, 3 * e + 2]
        _drain(lambda rows: block_copy(0, 0, sl, rows), total)

    @pl.when(is_first)
    def _():
        fetch(t, slot)

    @pl.when(has_next)
    def _():
        fetch(t_next, 1 - slot)

    wait_all(t, slot)
    y = y_vmem[slot]
    r = lax.broadcasted_iota(jnp.int32, (XS_ROWS, ROW_TILE), 0)
    perm = jnp.where(r == pos_ref[0:1, :], 1.0, jnp.where(r == pos_ref[1:2, :], 1.0, 0.0)).astype(bf16)
    return lax.dot_general(perm, y, _TN, preferred_element_type=f32)


def _in_kernel(*refs, with_ln):
    if with_ln:
        (tab_ref, x_ref, pos_ref, ys_hbm, pmod_ref, g_ref, b_ref, mod_ref, w_ref,
         xo_ref, u_ref, y_vmem, sem) = refs
        t, t_next, has_next, _ = _tile_id(0)
        step = pl.program_id(0) * pl.num_programs(1) + pl.program_id(1)
        f = _combine_experts(tab_ref, pos_ref, ys_hbm, y_vmem, sem, step, t, t_next, has_next)
        h = ALPHA * x_ref[...] + pmod_ref[5:6, :] * f
        x = _layer_norm(h, g_ref[...], b_ref[...])
        xo_ref[...] = x
    else:
        ctx_ref, lat_ref, mod_ref, w_ref, xo_ref, u_ref = refs
        x = jnp.where(pl.program_id(1) == 0, ctx_ref[...], lat_ref[...])
        xo_ref[...] = x
    xm = x * (1.0 + mod_ref[1:2, :]) + mod_ref[0:1, :]
    u_ref[...] = jnp.dot(xm.astype(bf16), w_ref[...], preferred_element_type=f32)


def _mod_spec(j_off=0):
    return pl.BlockSpec((None, None, 6, D_MODEL), lambda b, j: (b, jnp.minimum(j + j_off, 1), 0, 0))


def _row_spec(width, j_off=0):
    return pl.BlockSpec((None, ROW_TILE, width), lambda b, j: (b, j + j_off, 0))


def _vec_spec(l):
    return pl.BlockSpec((None, 1, D_MODEL), lambda b, j: (l, 0, 0))


def _pos_spec(j_off=0):
    return pl.BlockSpec((None, 2, ROW_TILE), lambda b, j: (b, 0, j + j_off))


def _combine_scratch():
    return [pltpu.VMEM((2, XS_ROWS, D_MODEL), bf16), pltpu.SemaphoreType.DMA((2,))]


def _in_proj(l, x, moe, pmod, ln_g, ln_b, mod, w_in):
    if moe is None:
        B, L = x[0].shape[0], x[0].shape[1] + x[1].shape[1]
    else:
        B, L, _ = x.shape
    w_spec = pl.BlockSpec((None, D_MODEL, D_IN), lambda b, j: (l, 0, 0))
    u_shape = jax.ShapeDtypeStruct((B, L, D_IN), f32)
    if moe is None:
        ctx, lat = x
        skip = CTX_LEN // ROW_TILE
        return pl.pallas_call(
            functools.partial(_in_kernel, with_ln=False),
            grid=(B, L // ROW_TILE),
            in_specs=[pl.BlockSpec((None, ROW_TILE, D_MODEL), lambda b, j: (b, 0, 0)),
                      pl.BlockSpec((None, ROW_TILE, D_MODEL), lambda b, j: (b, jnp.maximum(j - skip, 0), 0)),
                      _mod_spec(), w_spec],
            out_specs=[_row_spec(D_MODEL), _row_spec(D_IN)],
            out_shape=[jax.ShapeDtypeStruct((B, L, D_MODEL), f32), u_shape],
            compiler_params=_params("parallel", "parallel"),
            name="in_proj0",
        )(ctx, lat, mod, w_in)
    tab, pos, ys = moe
    return pl.pallas_call(
        functools.partial(_in_kernel, with_ln=True),
        grid=(B, L // ROW_TILE),
        in_specs=[pl.BlockSpec(memory_space=pltpu.SMEM), _row_spec(D_MODEL), _pos_spec(),
                  pl.BlockSpec(memory_space=pl.ANY), _mod_spec(), _vec_spec(l - 1), _vec_spec(l - 1),
                  _mod_spec(), w_spec],
        out_specs=[_row_spec(D_MODEL), _row_spec(D_IN)],
        out_shape=[jax.ShapeDtypeStruct(x.shape, f32), u_shape],
        scratch_shapes=_combine_scratch(),
        compiler_params=_params("arbitrary", "arbitrary"),
        name="in_proj",
    )(tab, x, pos, ys, pmod, ln_g, ln_b, mod, w_in)


def _scan_constants():
    C = SCAN_CHUNK
    p = np.arange(C)
    tri = (p[None, :] <= p[:, None]).astype(np.float32)
    owner = np.full((C, C), -1, np.int32)
    signs = []
    for li, m in enumerate(SCAN_LEVELS):
        same = (p[:, None] // (2 * m)) == (p[None, :] // (2 * m))
        owner[same & ((p[:, None] // m) % 2 == 1) & ((p[None, :] // m) % 2 == 0)] = li
        signs.append(np.where((p // m) % 2 == 1, 1.0, -1.0))
    owner[((p[:, None] // SCAN_BASE) == (p[None, :] // SCAN_BASE)) & (p[None, :] <= p[:, None])] = len(SCAN_LEVELS)
    sgn = np.broadcast_to(np.stack(signs)[:, :, None], (len(SCAN_LEVELS), C, LANES)).astype(np.float32)
    return (np.stack([tri, tri[::-1, ::-1]]), np.stack([owner, owner[::-1, ::-1]]),
            np.stack([sgn, sgn[:, ::-1]]))


def _hgrn_kernel(q_ref, ff_ref, fb_ref, v_ref, g_ref, lbf_ref, lbb_ref, ng_ref, t_ref, own_ref, sgn_ref,
                 a_ref, of_scr, ob_scr, *, n_ctx, n_all):
    C = SCAN_CHUNK
    n_lv = len(SCAN_LEVELS)


    def gates(rows, z_ref, lb, d):
        z = z_ref[rows, :]
        lf = jnp.log2(jnp.maximum(lb + (1.0 - lb) * _sigmoid(z), F_FLOOR))
        k = (1.0 - lb) * _sigmoid(z, -1.0)
        qz = q_ref[rows, :]
        q = qz * _sigmoid(qz)
        v = v_ref[rows, :].astype(bf16)
        hi = lf.astype(bf16)
        r1 = lf - hi.astype(f32)
        mid = r1.astype(bf16)
        lo = (r1 - mid.astype(f32)).astype(bf16)
        d3 = jnp.dot(t_ref[d], jnp.concatenate([hi, mid, lo], axis=1), preferred_element_type=f32)
        cum = d3[:, :LANES] + d3[:, LANES:2 * LANES] + d3[:, 2 * LANES:]
        return q, k, v, cum

    def intra(q, k, cum, d):
        last = cum[C - 1:C, :] if d == 0 else cum[0:1, :]

        def rel(block, ref_rows):
            blocks = cum.reshape(C // block, block, LANES)
            if len(ref_rows) == 1:
                ref = blocks[:, ref_rows[0]:ref_rows[0] + 1, :]
            else:
                first = lax.broadcasted_iota(jnp.int32, (1, block, 1), 1) < block // 2
                ref = jnp.where(first, blocks[:, ref_rows[0]:ref_rows[0] + 1, :],
                                blocks[:, ref_rows[1]:ref_rows[1] + 1, :])
            return (blocks - ref).reshape(C, LANES)

        def to_boundary(m):
            blocks = cum.reshape(C // (2 * m), 2 * m, LANES)
            lo, hi = blocks[:, :m, :], blocks[:, m:, :]
            if d == 0:
                ref = lo[:, m - 1:m, :]
                parts = (ref - lo, hi - ref)
            else:
                ref = hi[:, 0:1, :]
                parts = (lo - ref, ref - hi)
            return jnp.concatenate(parts, axis=1).reshape(C, LANES)

        q_in = (q * jnp.exp2(cum)).astype(bf16)
        own = own_ref[d]
        scores = jnp.zeros((C, C), f32)
        for li in range(n_lv + 1):
            m = SCAN_LEVELS[li] if li < n_lv else 0
            if m and m % 8 == 0:
                def halves(x):
                    blocks = x.reshape(C // (2 * m), 2 * m, x.shape[-1])
                    return blocks[:, :m, :], blocks[:, m:, :]

                def join(lo, hi):
                    return jnp.concatenate([lo, hi], axis=1).reshape(C, lo.shape[-1])

                (q_lo, q_hi), (k_lo, k_hi) = halves(q), halves(k)
                mixed = join(k_lo, q_hi) if d == 0 else join(q_lo, k_hi)
                z = (mixed * jnp.exp2(to_boundary(m))).astype(bf16)
                a = lax.dot_general(z, z, _NT, preferred_element_type=f32)
                (a_lo, a_hi), (s_lo, s_hi), (o_lo, o_hi) = halves(a), halves(scores), halves(own)
                if d == 0:
                    scores = join(s_lo, jnp.where(o_hi == li, a_hi, s_hi))
                else:
                    scores = join(jnp.where(o_lo == li, a_lo, s_lo), s_hi)
                continue
            if m:
                wq = jnp.exp2(rel(2 * m, (m - 1,) if d == 0 else (m,)) * sgn_ref[d, li])
                wk = wq
            else:
                nb = SCAN_BASE
                dl = rel(2 * nb, (0, nb) if d == 0 else (nb - 1, 2 * nb - 1))
                wq = jnp.exp2(dl)
                wk = jnp.exp2(-dl)
            a = lax.dot_general((q * wq).astype(bf16), (k * wk).astype(bf16), _NT,
                                preferred_element_type=f32)
            scores = jnp.where(own == li, a, scores)
        kdec = (k * jnp.exp2(last - cum)).astype(bf16)
        return q_in, scores.astype(bf16), kdec, jnp.exp2(last)

    def finish(pending, S, o_scr):
        q_in, scores, kdec, decay, v, row0 = pending
        o = (lax.dot_general(q_in, S.astype(bf16), _NT, preferred_element_type=f32)
             + jnp.dot(scores, v, preferred_element_type=f32))
        o_scr[pl.ds(pl.multiple_of(row0, C), C), :] = o
        return S * decay + lax.dot_general(v, kdec, _TN, preferred_element_type=f32)

    lbf = lbf_ref[...]
    lbb = lbb_ref[...]

    def bwd_chunk(i):
        return jnp.where(i < n_ctx, n_ctx - 1 - i, n_all - 1 - (i - n_ctx))

    def scan_body(i, carry):
        s_f, s_b, pend_f, pend_b = carry
        row_f = i * C
        row_b = bwd_chunk(i) * C
        s_f = finish(pend_f, s_f, of_scr)
        s_b = finish(pend_b, s_b, ob_scr)
        qf, kf, vf, cum_f = gates(pl.ds(pl.multiple_of(row_f, C), C), ff_ref, lbf, 0)
        qb, kb, vb, cum_b = gates(pl.ds(pl.multiple_of(row_b, C), C), fb_ref, lbb, 1)
        pend_f = intra(qf, kf, cum_f, 0) + (vf, row_f)
        pend_b = intra(qb, kb, cum_b, 1) + (vb, row_b)
        return s_f, s_b, pend_f, pend_b

    def idle(row0):
        zc = jnp.zeros((C, LANES), bf16)
        return (zc, jnp.zeros((C, C), bf16), zc, jnp.ones((1, LANES), f32), zc, row0)

    zero = jnp.zeros((A_DK, A_DK), f32)
    init = (zero, zero, idle(jnp.int32(0)), idle(bwd_chunk(jnp.int32(0)) * C))
    s_f, s_b, pend_f, pend_b = lax.fori_loop(0, n_all, scan_body, init, unroll=3)
    finish(pend_f, s_f, of_scr)
    finish(pend_b, s_b, ob_scr)

    ng = ng_ref[...]

    def readout_body(i, carry):
        rows = pl.ds(pl.multiple_of(i * ROW_TILE, ROW_TILE), ROW_TILE)
        o = of_scr[rows, :] + ob_scr[rows, :]
        o = o * lax.rsqrt(jnp.mean(o * o, axis=-1, keepdims=True) + RMS_EPS) * ng
        gz = g_ref[rows, :]
        a_ref[rows, :] = (o * (gz * _sigmoid(gz))).astype(a_ref.dtype)
        return carry

    lax.fori_loop(0, n_all * C // ROW_TILE, readout_body, 0)


def _hgrn(u, lb_f, lb_b, norm_g, scan_consts):
    B, L, _ = u.shape
    hb = A_WIDTH // LANES

    def col(sec):
        return pl.BlockSpec((None, L, LANES), lambda b, h: (b, 0, sec * hb + h))

    def whole(arr):
        return pl.BlockSpec(arr.shape, lambda b, h: (0,) * arr.ndim)

    lb_spec = pl.BlockSpec((None, 1, LANES), lambda b, h: (h, 0, 0))
    return pl.pallas_call(
        functools.partial(_hgrn_kernel, n_ctx=CTX_LEN // SCAN_CHUNK, n_all=L // SCAN_CHUNK),
        grid=(B, A_HEADS),
        in_specs=[col(0), col(1), col(2), col(3), col(4), lb_spec, lb_spec,
                  pl.BlockSpec((1, LANES), lambda b, h: (0, 0))] + [whole(a) for a in scan_consts],
        out_specs=pl.BlockSpec((None, L, LANES), lambda b, h: (b, 0, h)),
        out_shape=jax.ShapeDtypeStruct((B, L, A_WIDTH), bf16),
        scratch_shapes=[pltpu.VMEM((L, LANES), f32), pltpu.VMEM((L, LANES), f32)],
        compiler_params=_params("parallel", "parallel"),
        name="hgrn2",
    )(u, u, u, u, u, lb_f.reshape(A_HEADS, 1, LANES), lb_b.reshape(A_HEADS, 1, LANES),
      norm_g.reshape(1, LANES), *scan_consts)


def _rope_tables(seq):
    t = np.arange(seq)
    pos_r = (t // GRID_W).astype(np.float32)
    pos_c = (t % GRID_W).astype(np.float32)
    quarter = B_HD // 4
    inv = (ROPE_BASE ** (-np.arange(quarter, dtype=np.float32) / quarter)).astype(np.float32)
    lane = np.arange(LANES)
    in_head = lane % B_HD
    pos = np.where((in_head < B_HD // 2)[None, :], pos_r[:, None], pos_c[:, None])
    ang = (pos * inv[lane % quarter][None, :]).astype(np.float32)
    sign = np.where((lane % (2 * quarter)) < quarter, -1.0, 1.0)
    return np.cos(ang).astype(np.float32), (np.sin(ang) * sign[None, :]).astype(np.float32)


def _window_mask():
    c = np.arange(GRID_W)
    cs = np.clip(c - NA_COLS // 2, 0, GRID_W - NA_COLS)
    in_win = (c[None, :] >= cs[:, None]) & (c[None, :] < cs[:, None] + NA_COLS)
    return np.tile(in_win, (1, NA_ROWS)).astype(np.float32)


def _bias_tables(rpb):
    c = np.arange(GRID_W)
    dc = np.clip(c[None, :] - c[:, None], -(NA_COLS - 1), NA_COLS - 1) + (NA_COLS - 1)
    pick = (np.arange(2 * NA_COLS - 1)[:, None, None] == dc[None]).astype(np.float32)
    cols = jnp.einsum('lhdk,kcx->lhdcx', rpb, jnp.asarray(pick), precision=lax.Precision.HIGHEST)
    n_rel = 2 * NA_ROWS - 1
    n_win = NA_ROWS * GRID_W
    t = pl.pallas_call(
        _bias_expand_kernel,
        grid=(DEPTH * B_HEADS,),
        in_specs=[pl.BlockSpec((None, n_rel, GRID_W, GRID_W), lambda i: (i, 0, 0, 0))],
        out_specs=pl.BlockSpec((None, NA_ROWS, GRID_W, n_win), lambda i: (i, 0, 0, 0)),
        out_shape=jax.ShapeDtypeStruct((DEPTH * B_HEADS, NA_ROWS, GRID_W, n_win), f32),
        compiler_params=_params("parallel"),
        name="bias_expand",
    )(cols.reshape(DEPTH * B_HEADS, n_rel, GRID_W, GRID_W))
    return t.reshape(DEPTH, B_HEADS, NA_ROWS, GRID_W, n_win)


def _bias_expand_kernel(c_ref, o_ref):
    per_vreg = LANES // GRID_W
    for o in range(NA_ROWS):
        for g in range(NA_ROWS // per_vreg):
            parts = [c_ref[o + g * per_vreg + p] for p in range(per_vreg)]
            o_ref[o, :, g * LANES:(g + 1) * LANES] = jnp.concatenate(parts, axis=1)


def _na_kernel(q_ref, k_ref, v_ref, cos_ref, sin_ref, bias_ref, mask_ref, o_ref,
               qr_scr, kr_scr, qp_scr, kc_scr, v_scr, *, seq, rows_grid, ctx_out):
    hd = B_HD
    scale = hd ** -0.5 * LOG2E
    n_win = NA_ROWS * GRID_W
    lane = lax.broadcasted_iota(jnp.int32, (ROW_TILE, LANES), 1)
    first_half = (lane % (hd // 2)) < (hd // 4)
    head0 = lane < hd

    def swap_halves(x):
        return jnp.where(first_half, pltpu.roll(x, LANES - hd // 4, axis=1), pltpu.roll(x, hd // 4, axis=1))

    def store_per_head(dst, rows, x):
        dst[0, rows, :] = jnp.where(head0, x, 0.0).astype(bf16)
        dst[1, rows, :] = jnp.where(head0, 0.0, x).astype(bf16)

    ctx_rows = pl.ds(0, CTX_LEN)
    if ctx_out:
        store_per_head(qp_scr, ctx_rows, q_ref[ctx_rows, :] * scale)
    kc_scr[...] = k_ref[ctx_rows, :].astype(bf16)
    v_scr[ctx_rows, :] = v_ref[ctx_rows, :].astype(bf16)

    def prep(i, carry):
        lrows = pl.ds(pl.multiple_of(i * ROW_TILE, ROW_TILE), ROW_TILE)
        rows = pl.ds(pl.multiple_of(CTX_LEN + i * ROW_TILE, ROW_TILE), ROW_TILE)
        q = q_ref[rows, :] * scale
        k = k_ref[rows, :]
        cos = cos_ref[lrows, :]
        sin = sin_ref[lrows, :]
        store_per_head(qp_scr, rows, q)
        v_scr[rows, :] = v_ref[rows, :].astype(bf16)
        store_per_head(qr_scr, lrows, q * cos + swap_halves(q) * sin)
        kr_scr[lrows, :] = (k * cos + swap_halves(k) * sin).astype(bf16)
        return carry

    lax.fori_loop(0, seq // ROW_TILE, prep, 0)

    def pick_heads(o0, o1):
        keep0 = lax.broadcasted_iota(jnp.int32, o0.shape, 1) < hd
        return jnp.where(keep0, o0, o1)

    if ctx_out:
        outs = []
        for h in range(2):
            s = lax.dot_general(qp_scr[h, ctx_rows, :], kc_scr[...], _NT, preferred_element_type=f32)
            e = jnp.exp2(s - jnp.max(s, axis=-1, keepdims=True))
            o = jnp.dot(e.astype(bf16), v_scr[ctx_rows, :], preferred_element_type=f32)
            outs.append(o / jnp.sum(e, axis=-1, keepdims=True))
        o_ref[ctx_rows, :] = pick_heads(*outs).astype(o_ref.dtype)

    def rows_body(it, carry):
        chains = []
        for g in range(NA_GROUP):
            r = it * NA_GROUP + g
            rs = jnp.clip(r - NA_ROWS // 2, 0, rows_grid - NA_ROWS)
            for h in range(2):
                chains.append(dict(
                    h=h, off=rs - r + (NA_ROWS - 1),
                    qrows=pl.ds(pl.multiple_of(r * GRID_W, GRID_W), GRID_W),
                    krows=pl.ds(pl.multiple_of(rs * GRID_W, GRID_W), n_win),
                    arows=pl.ds(pl.multiple_of(CTX_LEN + r * GRID_W, GRID_W), GRID_W),
                    vrows=pl.ds(pl.multiple_of(CTX_LEN + rs * GRID_W, GRID_W), n_win)))
        for c in chains:
            h = c["h"]
            c["s_lat"] = lax.dot_general(qr_scr[h, c["qrows"], :], kr_scr[c["krows"], :], _NT,
                                         preferred_element_type=f32)
            c["s_ctx"] = lax.dot_general(qp_scr[h, c["arows"], :], kc_scr[...], _NT, preferred_element_type=f32)
        in_win = mask_ref[...] > 0.5
        for c in chains:
            s_lat = jnp.where(in_win, c["s_lat"] + bias_ref[c["h"], c["off"]], MASK_VALUE)
            m = jnp.maximum(jnp.max(s_lat, axis=-1, keepdims=True), jnp.max(c["s_ctx"], axis=-1, keepdims=True))
            e_lat = jnp.exp2(s_lat - m)
            e_ctx = jnp.exp2(c["s_ctx"] - m)
            c["den"] = jnp.sum(e_lat, axis=-1, keepdims=True) + jnp.sum(e_ctx, axis=-1, keepdims=True)
            c["e_lat"] = e_lat.astype(bf16)
            c["e_ctx"] = e_ctx.astype(bf16)
        for c in chains:
            c["o"] = (jnp.dot(c["e_lat"], v_scr[c["vrows"], :], preferred_element_type=f32)
                      + jnp.dot(c["e_ctx"], v_scr[ctx_rows, :], preferred_element_type=f32))
        for g in range(NA_GROUP):
            c0, c1 = chains[2 * g:2 * g + 2]
            o_ref[c0["arows"], :] = pick_heads(c0["o"] / c0["den"], c1["o"] / c1["den"]).astype(o_ref.dtype)
        return carry

    lax.fori_loop(0, rows_grid // NA_GROUP, rows_body, 0)


def _na(l, u, cos_t, sin_t, bias_t, mask):
    B, L, _ = u.shape
    seq = L - CTX_LEN
    first = 5 * A_WIDTH // LANES
    pairs = B_WIDTH // LANES

    def col(sec):
        return pl.BlockSpec((None, L, LANES), lambda b, h: (b, 0, first + sec * pairs + h))

    n_win = NA_ROWS * GRID_W
    return pl.pallas_call(
        functools.partial(_na_kernel, seq=seq, rows_grid=seq // GRID_W, ctx_out=l < DEPTH - 1),
        grid=(B, pairs),
        in_specs=[col(0), col(1), col(2),
                  pl.BlockSpec((seq, LANES), lambda b, h: (0, 0)),
                  pl.BlockSpec((seq, LANES), lambda b, h: (0, 0)),
                  pl.BlockSpec((None, 2, NA_ROWS, GRID_W, n_win), lambda b, h: (l, h, 0, 0, 0)),
                  pl.BlockSpec((GRID_W, n_win), lambda b, h: (0, 0))],
        out_specs=pl.BlockSpec((None, L, LANES), lambda b, h: (b, 0, h)),
        out_shape=jax.ShapeDtypeStruct((B, L, B_WIDTH), bf16),
        scratch_shapes=[pltpu.VMEM((2, seq, LANES), bf16), pltpu.VMEM((seq, LANES), bf16),
                        pltpu.VMEM((2, L, LANES), bf16), pltpu.VMEM((CTX_LEN, LANES), bf16),
                        pltpu.VMEM((L, LANES), bf16)],
        compiler_params=_params("parallel", "parallel"),
        name="na_attn",
    )(u, u, u, cos_t, sin_t, bias_t, mask)


def _route(logits, b_router, sel_ref, gate_ref):
    aff = _sigmoid(logits)
    sel = aff + b_router
    s = [sel[e:e + 1, :] for e in range(N_EXPERTS)]
    one = lambda cond: jnp.where(cond, 1.0, 0.0)
    scores = []
    for g in range(N_GROUPS):
        s0, s1, s2, s3 = s[g * EXPERTS_PER_GROUP:(g + 1) * EXPERTS_PER_GROUP]
        hi01, lo01 = jnp.maximum(s0, s1), jnp.minimum(s0, s1)
        hi23, lo23 = jnp.maximum(s2, s3), jnp.minimum(s2, s3)
        scores.append(jnp.maximum(hi01, hi23) + jnp.maximum(jnp.minimum(hi01, hi23), jnp.maximum(lo01, lo23)))
    weights = []
    chosen = []
    for g in range(N_GROUPS):
        gsel = None
        for g2 in range(N_GROUPS):
            if g2 == g:
                continue
            t = one(scores[g2] < scores[g]) if g2 < g else one(scores[g2] <= scores[g])
            gsel = t if gsel is None else gsel * t
        for i in range(EXPERTS_PER_GROUP):
            e = g * EXPERTS_PER_GROUP + i
            rank = None
            for j in range(EXPERTS_PER_GROUP):
                if j == i:
                    continue
                e2 = g * EXPERTS_PER_GROUP + j
                t = one(s[e2] >= s[e]) if j < i else one(s[e2] > s[e])
                rank = t if rank is None else rank + t
            chosen.append(gsel * one(rank < 1.5))
            weights.append(chosen[-1] * aff[e:e + 1, :])
    total = weights[0]
    for w in weights[1:]:
        total = total + w
    inv = 1.0 / total
    for e in range(N_EXPERTS):
        sel_ref[e:e + 1, :] = chosen[e]
        gate_ref[e:e + 1, :] = weights[e] * inv


def _out_kernel(a_ref, b_ref, x_ref, mod_ref, wa_ref, wb_ref, g_ref, bt_ref, wr_ref, br_ref, tri_ref, low_ref,
                x1_ref, pos_ref, tab_ref, cnt_ref, xs_hbm,
                sel_scr, gate_scr, meta_v, meta_s, xs_vmem, sem, cnt_smem, nblk_smem, *, cap, j_off):
    t, _, has_next, is_first = _tile_id(j_off)
    slot = lax.rem(pl.program_id(0) * pl.num_programs(1) + pl.program_id(1), 2)

    @pl.when(is_first)
    def _():
        for e in range(N_EXPERTS):
            cnt_smem[e] = 0
        nblk_smem[0] = 0
        nblk_smem[1] = 0

    y = (jnp.dot(a_ref[...], wa_ref[...], preferred_element_type=f32)
         + jnp.dot(b_ref[...], wb_ref[...], preferred_element_type=f32))
    x1 = _layer_norm(ALPHA * x_ref[...] + mod_ref[2:3, :] * y, g_ref[...], bt_ref[...])
    x1_ref[...] = x1
    xm = x1 * (1.0 + mod_ref[4:5, :]) + mod_ref[3:4, :]
    xm_hi = xm.astype(bf16)
    xm_mid = (xm - xm_hi.astype(f32)).astype(bf16)
    E = N_EXPERTS
    r_hi = lax.dot_general(wr_ref[...], xm_hi, _NT, preferred_element_type=f32)
    r_mid = lax.dot_general(wr_ref[0:2 * E, :], xm_mid, _NT, preferred_element_type=f32)
    logits = ((r_hi[0:E] + r_mid[0:E]) + (r_hi[E:2 * E] + r_mid[E:2 * E])) + r_hi[2 * E:3 * E]
    _route(logits, br_ref[...], sel_scr, gate_scr)

    sel = sel_scr[...]
    gate = gate_scr[...]
    sel_b = sel.astype(bf16)
    rank = jnp.dot(sel_b, tri_ref[...], preferred_element_type=f32)
    below = jnp.dot(low_ref[...], sel_b, preferred_element_type=f32)
    n = jnp.sum(sel, axis=1, keepdims=True)
    npad = jnp.broadcast_to(jnp.floor((n + (XS_BLK - 1)) * (1.0 / XS_BLK)) * XS_BLK, (N_EXPERTS, LANES))
    start = jnp.dot(low_ref[...], npad.astype(bf16), preferred_element_type=f32)
    pos = start[:, :1] + rank
    first = jnp.where(below < 0.5, sel, 0.0)
    second = sel - first
    pos0 = jnp.sum(first * pos, axis=0, keepdims=True).astype(jnp.int32)
    pos1 = jnp.sum(second * pos, axis=0, keepdims=True).astype(jnp.int32)
    g0 = jnp.sum(first * gate, axis=0, keepdims=True)
    g1 = jnp.sum(second * gate, axis=0, keepdims=True)
    pos_ref[0:1, :] = pos0
    pos_ref[1:2, :] = pos1
    meta_v[0] = start.astype(jnp.int32)
    meta_v[1] = (npad * (1.0 / XS_BLK)).astype(jnp.int32)
    pltpu.sync_copy(meta_v, meta_s)

    r = lax.broadcasted_iota(jnp.int32, (XS_ROWS, ROW_TILE), 0)
    hit0 = r == pos0
    hit1 = r == pos1
    perm = jnp.where(hit0, 1.0, jnp.where(hit1, 1.0, 0.0)).astype(bf16)
    gate_sorted = jnp.sum(jnp.where(hit0, g0, jnp.where(hit1, g1, 0.0)), axis=1, keepdims=True)
    xs = jnp.dot(perm, xm_hi, preferred_element_type=f32)

    def block_copy(src_row, dst_row, sl, rows=XS_BLK):
        return pltpu.make_async_copy(xs_vmem.at[sl, pl.ds(pl.multiple_of(src_row, XS_BLK), rows), :],
                                     xs_hbm.at[pl.ds(pl.multiple_of(dst_row, XS_BLK), rows), :],
                                     sem.at[sl])

    def wait_blocks(count, sl):
        _drain(lambda rows: block_copy(0, 0, sl, rows), count)

    wait_blocks(nblk_smem[slot], slot)
    xs_vmem[slot, :, :D_MODEL] = xs.astype(bf16)
    gate_hi = gate_sorted.astype(bf16)
    gate_lo = (gate_sorted - gate_hi.astype(f32)).astype(bf16)
    xs_vmem[slot, :, D_MODEL:D_MODEL + LANES] = jnp.broadcast_to(gate_hi, (XS_ROWS, LANES))
    xs_vmem[slot, :, D_MODEL + LANES:] = jnp.broadcast_to(gate_lo, (XS_ROWS, LANES))
    total = 0
    for e in range(N_EXPERTS):
        start_e = meta_s[0, e, 0]
        nblk_e = meta_s[1, e, 0]
        base = e * cap + cnt_smem[e]
        tab_ref[t, 3 * e] = base
        tab_ref[t, 3 * e + 1] = start_e
        tab_ref[t, 3 * e + 2] = nblk_e

        def body(i, carry):
            block_copy(start_e + 2 * i * XS_BLK, base + 2 * i * XS_BLK, slot).start(priority=0)

            @pl.when(2 * i + 1 < nblk_e)
            def _():
                block_copy(start_e + (2 * i + 1) * XS_BLK, base + (2 * i + 1) * XS_BLK, slot).start(priority=1)

            return carry

        lax.fori_loop(0, (nblk_e + 1) // 2, body, 0)
        cnt_smem[e] = cnt_smem[e] + nblk_e * XS_BLK
        total = total + nblk_e
    nblk_smem[slot] = total

    @pl.when(jnp.logical_not(has_next))
    def _():
        wait_blocks(nblk_smem[slot], slot)
        wait_blocks(nblk_smem[1 - slot], 1 - slot)
        for e in range(N_EXPERTS):
            cnt_ref[0, e] = cnt_smem[e]


def _dispatch_constants():
    t = np.arange(ROW_TILE)
    e = np.arange(N_EXPERTS)
    return ((t[:, None] < t[None, :]).astype(np.float32), (e[None, :] < e[:, None]).astype(np.float32))


def _out_proj(l, a, bmix, x, mod, w_out, ln_g, ln_b, w_router_t, b_router, tri, low, skip_ctx):
    B, L, _ = x.shape
    half = A_WIDTH
    n_tiles = B * (L // ROW_TILE)
    cap = _expert_capacity(n_tiles)
    smem = pl.BlockSpec(memory_space=pltpu.SMEM)
    j_off = CTX_LEN // ROW_TILE if skip_ctx else 0
    rows = functools.partial(_row_spec, j_off=j_off)
    return pl.pallas_call(
        functools.partial(_out_kernel, cap=cap, j_off=j_off),
        grid=(B, L // ROW_TILE - j_off),
        in_specs=[rows(A_WIDTH), rows(B_WIDTH), rows(D_MODEL), _mod_spec(j_off),
                  pl.BlockSpec((None, half, D_MODEL), lambda b, j: (l, 0, 0)),
                  pl.BlockSpec((None, half, D_MODEL), lambda b, j: (l, 1, 0)),
                  _vec_spec(l), _vec_spec(l),
                  pl.BlockSpec((3 * N_EXPERTS, D_MODEL), lambda b, j: (0, 0)),
                  pl.BlockSpec((N_EXPERTS, 1), lambda b, j: (0, 0)),
                  pl.BlockSpec((ROW_TILE, ROW_TILE), lambda b, j: (0, 0)),
                  pl.BlockSpec((N_EXPERTS, N_EXPERTS), lambda b, j: (0, 0))],
        out_specs=[rows(D_MODEL), _pos_spec(j_off), smem, smem, pl.BlockSpec(memory_space=pl.ANY)],
        out_shape=[jax.ShapeDtypeStruct(x.shape, f32), jax.ShapeDtypeStruct((B, 2, L), jnp.int32),
                   jax.ShapeDtypeStruct((n_tiles, 3 * N_EXPERTS), jnp.int32),
                   jax.ShapeDtypeStruct((1, N_EXPERTS), jnp.int32),
                   jax.ShapeDtypeStruct((N_EXPERTS * cap, XS_W), bf16)],
        scratch_shapes=[pltpu.VMEM((N_EXPERTS, ROW_TILE), f32), pltpu.VMEM((N_EXPERTS, ROW_TILE), f32),
                        pltpu.VMEM((2, N_EXPERTS, LANES), jnp.int32), pltpu.SMEM((2, N_EXPERTS, LANES), jnp.int32),
                        pltpu.VMEM((2, XS_ROWS, XS_W), bf16), pltpu.SemaphoreType.DMA((2,)),
                        pltpu.SMEM((N_EXPERTS,), jnp.int32), pltpu.SMEM((2,), jnp.int32)],
        compiler_params=_params("arbitrary", "arbitrary"),
        name="out_proj",
    )(a, bmix, x, mod, w_out, w_out, ln_g, ln_b, w_router_t, b_router, tri, low)


def _expert_capacity(n_tiles):
    rows = n_tiles * (ROW_TILE + XS_BLK)
    return -(-rows // EXP_TILE) * EXP_TILE


def _moe_kernel(te_ref, tb_ref, tv_ref, xs_ref, wg_ref, wu_ref, wd_ref, ys_ref, wg_b, wu_b, wd_b):
    i = pl.program_id(0)
    valid = tv_ref[i]
    new_expert = jnp.logical_or(i == 0, te_ref[i] != te_ref[jnp.maximum(i - 1, 0)])

    @pl.when(new_expert)
    def _():
        wg_b[...] = wg_ref[...].astype(bf16)
        wu_b[...] = wu_ref[...].astype(bf16)
        wd_b[...] = wd_ref[...].astype(bf16)

    @pl.when(valid > 0)
    def _():
        chunk = MOE_CHUNK
        ok = lax.broadcasted_iota(jnp.int32, (EXP_TILE, 1), 0) < valid
        x = jnp.where(ok, xs_ref[:, :D_MODEL], jnp.zeros((), bf16))
        gate_terms = (xs_ref[:, D_MODEL:D_MODEL + 1].astype(f32)
                      + xs_ref[:, D_MODEL + LANES:D_MODEL + LANES + 1].astype(f32))
        gate = jnp.where(ok, gate_terms, 0.0)
        hid = []
        for c in range(D_EXPERT // chunk):
            cols = slice(c * chunk, (c + 1) * chunk)
            hg = jnp.dot(x, wg_b[:, cols], preferred_element_type=f32)
            hu = jnp.dot(x, wu_b[:, cols], preferred_element_type=f32)
            hid.append((hg * _sigmoid(hg) * hu * gate).astype(bf16))
        hid = jnp.concatenate(hid, axis=1)
        for c in range(D_MODEL // chunk):
            cols = slice(c * chunk, (c + 1) * chunk)
            ys_ref[:, cols] = jnp.dot(hid, wd_b[:, cols], preferred_element_type=f32).astype(bf16)


def _expert_tiles(cnt, cap, n_steps):
    cnt = cnt.reshape(N_EXPERTS)
    tiles = (cnt + (EXP_TILE - 1)) // EXP_TILE
    ends = jnp.cumsum(tiles)
    n_valid = ends[-1]
    i = jnp.clip(jnp.arange(n_steps, dtype=jnp.int32), 0, jnp.maximum(n_valid - 1, 0))
    e = jnp.sum((i[:, None] >= ends[None, :]).astype(jnp.int32), axis=1)
    local = i - (ends - tiles)[e]
    rows = jnp.clip(cnt[e] - local * EXP_TILE, 0, EXP_TILE)
    rows = jnp.where(jnp.arange(n_steps) < n_valid, rows, 0)
    return e.astype(jnp.int32), (e * (cap // EXP_TILE) + local).astype(jnp.int32), rows.astype(jnp.int32)


def _moe(l, xs, cnt, n_tiles, w_gate, w_up, w_down):
    cap = _expert_capacity(n_tiles)
    n_steps = n_tiles * XS_ROWS // EXP_TILE + N_EXPERTS
    te, tb, tv = _expert_tiles(cnt, cap, n_steps)
    grid_spec = pltpu.PrefetchScalarGridSpec(
        num_scalar_prefetch=3,
        grid=(n_steps,),
        in_specs=[pl.BlockSpec((EXP_TILE, XS_W), lambda i, te, tb, tv: (tb[i], 0)),
                  pl.BlockSpec((None, None, D_MODEL, D_EXPERT), lambda i, te, tb, tv: (l, te[i], 0, 0)),
                  pl.BlockSpec((None, None, D_MODEL, D_EXPERT), lambda i, te, tb, tv: (l, te[i], 0, 0)),
                  pl.BlockSpec((None, None, D_EXPERT, D_MODEL), lambda i, te, tb, tv: (l, te[i], 0, 0))],
        out_specs=pl.BlockSpec((EXP_TILE, D_MODEL), lambda i, te, tb, tv: (tb[i], 0)),
        scratch_shapes=[pltpu.VMEM((D_MODEL, D_EXPERT), bf16), pltpu.VMEM((D_MODEL, D_EXPERT), bf16),
                        pltpu.VMEM((D_EXPERT, D_MODEL), bf16)],
    )
    return pl.pallas_call(
        _moe_kernel,
        grid_spec=grid_spec,
        out_shape=jax.ShapeDtypeStruct((N_EXPERTS * cap, D_MODEL), bf16),
        compiler_params=_params("arbitrary"),
        name="moe",
    )(te, tb, tv, xs, w_gate, w_up, w_down)


def _final_kernel(tab_ref, x_ref, pos_ref, ys_hbm, mod_ref, g_ref, b_ref, o_ref, y_vmem, sem, *, j_off):
    t, t_next, has_next, _ = _tile_id(j_off)
    step = pl.program_id(0) * pl.num_programs(1) + pl.program_id(1)
    f = _combine_experts(tab_ref, pos_ref, ys_hbm, y_vmem, sem, step, t, t_next, has_next)
    o_ref[...] = _layer_norm(ALPHA * x_ref[...] + mod_ref[5:6, :] * f, g_ref[...], b_ref[...])


def _final_ln(x, moe, mod, ln_g, ln_b):
    B, L, _ = x.shape
    seq = L - CTX_LEN
    skip = CTX_LEN // ROW_TILE
    tab, pos, ys = moe
    lat = pl.BlockSpec((None, ROW_TILE, D_MODEL), lambda b, j: (b, j + skip, 0))
    return pl.pallas_call(
        functools.partial(_final_kernel, j_off=skip),
        grid=(B, seq // ROW_TILE),
        in_specs=[pl.BlockSpec(memory_space=pltpu.SMEM), lat, _pos_spec(skip), pl.BlockSpec(memory_space=pl.ANY),
                  pl.BlockSpec((None, None, 6, D_MODEL), lambda b, j: (b, 1, 0, 0)),
                  _vec_spec(DEPTH - 1), _vec_spec(DEPTH - 1)],
        out_specs=_row_spec(D_MODEL),
        out_shape=jax.ShapeDtypeStruct((B, seq, D_MODEL), f32),
        scratch_shapes=_combine_scratch(),
        compiler_params=_params("arbitrary", "arbitrary"),
        name="final_ln",
    )(tab, x, pos, ys, mod, ln_g, ln_b)


def kernel(x, c, ctx, c_ctx, w_ada, b_ada, w_in, lb_logits, a_norm_g, rpb, w_out, ln1_g, ln1_b,
           w_router, b_router, w_gate, w_up, w_down, ln2_g, ln2_b):
    B, seq, D = x.shape
    L = CTX_LEN + seq
    xs = (ctx, x)

    cc = jnp.concatenate([c, c_ctx[None, :], jnp.zeros((16 - B - 1, D), f32)], axis=0)
    mods = _ada_mods(cc, w_ada, b_ada)
    lat_mod = mods[:, :B].reshape(DEPTH, B, 1, 6, D)
    ctx_mod = jnp.broadcast_to(mods[:, B].reshape(DEPTH, 1, 1, 6, D), (DEPTH, B, 1, 6, D))
    mod_all = jnp.concatenate([ctx_mod, lat_mod], axis=2)

    sm = jax.nn.softmax(lb_logits.astype(f32), axis=1)
    lower = jnp.cumsum(sm, axis=1) - sm[:, :1]

    tri_np, own_np, sgn_np = _scan_constants()
    scan_consts = (jnp.asarray(tri_np, bf16), jnp.asarray(own_np), jnp.asarray(sgn_np))
    cos_np, sin_np = _rope_tables(seq)
    cos_t, sin_t = jnp.asarray(cos_np), jnp.asarray(sin_np)
    mask = jnp.asarray(_window_mask())
    bias_t = _bias_tables(rpb.astype(f32) * LOG2E)

    w_in_b = w_in.astype(bf16)
    w_out_b = w_out.astype(bf16)
    wr_hi = w_router.T.astype(bf16)
    wr_res = w_router.T - wr_hi.astype(f32)
    wr_mid = wr_res.astype(bf16)
    wr_lo = (wr_res - wr_mid.astype(f32)).astype(bf16)
    w_router_t = jnp.concatenate([wr_hi, wr_mid, wr_lo], axis=0)
    b_router_c = b_router.reshape(N_EXPERTS, 1)
    ln1_g3, ln1_b3 = ln1_g.reshape(DEPTH, 1, D), ln1_b.reshape(DEPTH, 1, D)
    ln2_g3, ln2_b3 = ln2_g.reshape(DEPTH, 1, D), ln2_b.reshape(DEPTH, 1, D)

    tri_np, low_np = _dispatch_constants()
    tri, low = jnp.asarray(tri_np, bf16), jnp.asarray(low_np, bf16)
    n_tiles = B * (L // ROW_TILE)

    moe = None
    for l in range(DEPTH):
        pmod = mod_all[l - 1] if l > 0 else None
        xs, u = _in_proj(l, xs, moe, pmod, ln2_g3, ln2_b3, mod_all[l], w_in_b)
        a = _hgrn(u, lower[0, l], lower[1, l], a_norm_g[l], scan_consts)
        bmix = _na(l, u, cos_t, sin_t, bias_t, mask)
        xs, pos, tab, cnt, x_sorted = _out_proj(l, a, bmix, xs, mod_all[l], w_out_b, ln1_g3, ln1_b3,
                                                w_router_t, b_router_c, tri, low, skip_ctx=l == DEPTH - 1)
        y_sorted = _moe(l, x_sorted, cnt, n_tiles, w_gate, w_up, w_down)
        moe = (tab, pos, y_sorted)
    return _final_ln(xs, moe, mod_all[DEPTH - 1], ln2_g3, ln2_b3)
```
